```python
import math
import jax, jax.numpy as jnp
from jax import lax
import numpy as np

D_MODEL = 1024
BATCH = 2
SEQ = 8192
DEPTH = 1
DEC_BATCH = 128
DEC_SEQ = 1
PAST_LEN = 8192
PAGE_SIZE = 128

N_HEADS = 8
N_KV = 2
GROUP = N_HEADS // N_KV
HEAD_DIM = 64
NSA_WIDTH = N_HEADS * HEAD_DIM
KV_WIDTH = N_KV * HEAD_DIM
L_CMP = 32
D_CMP = 16
CMP_RATIO = L_CMP // D_CMP
PHI_HIDDEN = 2 * HEAD_DIM
L_SEL = 64
K_SEL = 16
WINDOW = 512
Q_BLOCK = 128
FORCE_SCORE = 1000.0
N_BUCKETS = 32
MAX_DIST = 128
M_HEADS = 4
M_DH = 128
M_WIDTH = M_HEADS * M_DH
CONV_W = 4
M_CHUNK = 64
D_FF = 2816
EPS = 1e-6
MIX_SIZES = (NSA_WIDTH, KV_WIDTH, KV_WIDTH, KV_WIDTH, KV_WIDTH, KV_WIDTH, KV_WIDTH, 3 * N_HEADS,
             M_WIDTH, M_WIDTH, 2 * M_HEADS, M_WIDTH, 2 * D_MODEL)
MIX_IN_COLS = sum(MIX_SIZES)

kernel_name = 'hybrid_nsa_mlstm_macaron_step'


def rmsnorm(x, g):
    x32 = x.astype(jnp.float32)
    y = x32 * lax.rsqrt(jnp.mean(x32 * x32, axis=-1, keepdims=True) + EPS)
    return (y * g.astype(jnp.float32)).astype(x.dtype)


def swiglu(x, w_in, w_out):
    gate, up = jnp.split(x @ w_in, 2, axis=-1)
    return (jax.nn.silu(gate) * up) @ w_out


def t5_bucket(dist):
    n = jnp.maximum(dist, 0)
    max_exact = N_BUCKETS // 2
    nf = jnp.maximum(n, 1).astype(jnp.float32)
    large = max_exact + (jnp.log(nf / max_exact) / math.log(MAX_DIST / max_exact)
                         * (N_BUCKETS - max_exact)).astype(jnp.int32)
    return jnp.where(n < max_exact, n, jnp.minimum(large, N_BUCKETS - 1))


def masked_softmax(logits, mask):
    logits = jnp.where(mask, logits.astype(jnp.float32), -jnp.inf)
    m = jnp.max(logits, axis=-1, keepdims=True)
    m = jnp.where(jnp.isfinite(m), m, 0.0)
    p = jnp.exp(logits - m)
    return p / jnp.maximum(jnp.sum(p, axis=-1, keepdims=True), 1e-30)


def split_mix(z):
    idx = np.cumsum(MIX_SIZES)[:-1].tolist()
    return jnp.split(z, idx, axis=-1)


def project_mix(h, w_mix_in):
    b, t, _ = h.shape
    zq, zkc, zvc, zks, zvs, zkw, zvw, zg, zmqk, zmv, zmif, zmo, zmerge = split_mix(h @ w_mix_in)
    kv = lambda z: z.reshape(b, t, N_KV, HEAD_DIM)
    gates = jax.nn.sigmoid(zg.astype(jnp.float32)).reshape(b, t, N_HEADS, 3)
    return (zq.reshape(b, t, N_HEADS, HEAD_DIM), kv(zkc), kv(zvc), kv(zks), kv(zvs), kv(zkw), kv(zvw),
            gates, zmqk, zmv, zmif, zmo, zmerge)


def compress(rows, pos_emb, w1, w2):
    b, t, g, d = rows.shape
    n_cmp = (t - L_CMP) // D_CMP + 1
    n_sub = n_cmp + CMP_RATIO - 1
    sub = rows[:, :n_sub * D_CMP].reshape(b, n_sub, D_CMP, g, d)
    hidden = None
    for r in range(CMP_RATIO):
        xr = sub[:, r:r + n_cmp] + pos_emb[r * D_CMP:(r + 1) * D_CMP][None, None, :, None, :]
        part = jnp.einsum('bnlgd,ldh->bngh', xr, w1[r * D_CMP:(r + 1) * D_CMP])
        hidden = part if hidden is None else hidden + part
    return jnp.einsum('bngh,hd->bngd', jax.nn.gelu(hidden), w2)


def nsa_attend(q, qpos, kc, vc, gather_sel, kw, vw, kw_pos, gates, rel_bias, seq_len):
    b, tq = q.shape[:2]
    scale = HEAD_DIM ** -0.5
    qg = q.reshape(b, tq, N_KV, GROUP, HEAD_DIM)
    rb = rel_bias.astype(jnp.float32).reshape(N_BUCKETS, N_KV, GROUP)
    n_cmp = kc.shape[1]
    c_start = jnp.arange(n_cmp, dtype=jnp.int32) * D_CMP
    dist_c = qpos[:, None] - (c_start + L_CMP - 1)[None, :]
    bias_c = jnp.transpose(rb[t5_bucket(dist_c)], (2, 3, 0, 1))
    s_c = jnp.einsum('bqgrd,bngd->bgrqn', qg, kc).astype(jnp.float32) * scale + bias_c
    p_c = masked_softmax(s_c, dist_c >= 0)
    o_c = jnp.einsum('bgrqn,bngd->bqgrd', p_c, vc.astype(jnp.float32))
    n_sel = -(-seq_len // L_SEL)
    s_start = jnp.arange(n_sel, dtype=jnp.int32) * L_SEL
    overlap = ((c_start[:, None] <= s_start[None, :] + L_SEL - 1)
               & (c_start[:, None] + L_CMP - 1 >= s_start[None, :])).astype(jnp.float32)
    imp = jnp.einsum('bgrqn,ns->bgqs', p_c, overlap)
    cur = qpos // L_SEL
    blk = jnp.arange(n_sel, dtype=jnp.int32)
    forced = (blk[None, :] == 0) | (blk[None, :] == cur[:, None]) | (blk[None, :] == cur[:, None] - 1)
    valid = s_start[None, :] <= qpos[:, None]
    imp = jnp.where(valid, imp + jnp.where(forced, FORCE_SCORE, 0.0), -jnp.inf)
    _, top = lax.top_k(imp, min(K_SEL, n_sel))
    tok = (top[..., None] * L_SEL + jnp.arange(L_SEL, dtype=jnp.int32)).reshape(b, N_KV, tq, -1)
    ks, vs = gather_sel(tok)
    dist_s = qpos[None, None, :, None] - tok
    g_idx = jnp.arange(N_KV)[None, :, None, None]
    bias_s = jnp.moveaxis(jnp.transpose(rb, (1, 0, 2))[g_idx, t5_bucket(dist_s)], -1, 2)
    s_s = jnp.einsum('bqgrd,bgqkd->bgrqk', qg, ks).astype(jnp.float32) * scale + bias_s
    p_s = masked_softmax(s_s, (dist_s >= 0)[:, :, None])
    o_s = jnp.einsum('bgrqk,bgqkd->bqgrd', p_s, vs.astype(jnp.float32))
    dist_w = qpos[:, None] - kw_pos[None, :]
    mask_w = (dist_w >= 0) & (dist_w <= WINDOW) & (kw_pos[None, :] >= 0)
    bias_w = jnp.transpose(rb[t5_bucket(dist_w)], (2, 3, 0, 1))
    s_w = jnp.einsum('bqgrd,bkgd->bgrqk', qg, kw).astype(jnp.float32) * scale + bias_w
    p_w = masked_softmax(s_w, mask_w)
    o_w = jnp.einsum('bgrqk,bkgd->bqgrd', p_w, vw.astype(jnp.float32))
    g = gates.reshape(b, tq, N_KV, GROUP, 3)
    o = o_c * g[..., 0:1] + o_s * g[..., 1:2] + o_w * g[..., 2:3]
    return o.reshape(b, tq, NSA_WIDTH).astype(q.dtype)


def mlstm_inputs(conv_in, v_src, if_logits, o_logits, conv_w, conv_b, wq, wk, gate_bias):
    b, t = v_src.shape[:2]
    conv = conv_b
    for j in range(CONV_W):
        conv = conv + conv_in[:, j:j + t] * conv_w[j]
    a = jax.nn.silu(conv).astype(jnp.float32).reshape(b, t, M_HEADS, M_DH)
    q = jnp.einsum('bthd,hde->bhte', a, wq.astype(jnp.float32))
    k = jnp.einsum('bthd,hde->bhte', a, wk.astype(jnp.float32)) * (M_DH ** -0.5)
    v = v_src.astype(jnp.float32).reshape(b, t, M_HEADS, M_DH).transpose(0, 2, 1, 3)
    gl = if_logits.astype(jnp.float32).reshape(b, t, 2, M_HEADS) + gate_bias.astype(jnp.float32)
    ig = gl[:, :, 0].transpose(0, 2, 1)
    lf = jax.nn.log_sigmoid(gl[:, :, 1]).transpose(0, 2, 1)
    o = jax.nn.sigmoid(o_logits.astype(jnp.float32))
    return q, k, v, ig, lf, o


def mlstm_chunk(carry, xs):
    C, n, m = carry
    q, k, v, ig, lf = xs
    L = q.shape[2]
    bcum = jnp.cumsum(lf, axis=-1)
    causal = jnp.tril(jnp.ones((L, L), dtype=bool))
    dmat = jnp.where(causal, bcum[..., :, None] - bcum[..., None, :] + ig[..., None, :], -jnp.inf)
    inter = bcum + m[..., None]
    m_t = jnp.maximum(jnp.max(dmat, axis=-1), inter)
    sc = jnp.einsum('bhtd,bhsd->bhts', q, k) * jnp.exp(dmat - m_t[..., None])
    decay = jnp.exp(inter - m_t)
    num = decay[..., None] * jnp.einsum('bhvd,bhtd->bhtv', C, q) + jnp.einsum('bhts,bhsv->bhtv', sc, v)
    den = decay * jnp.einsum('bhd,bhtd->bht', n, q) + jnp.sum(sc, axis=-1)
    h = num / jnp.maximum(jnp.abs(den), jnp.exp(-m_t))[..., None]
    b_last = bcum[..., -1]
    w_log = b_last[..., None] - bcum + ig
    m_new = jnp.maximum(b_last + m, jnp.max(w_log, axis=-1))
    w = jnp.exp(w_log - m_new[..., None])
    carry_decay = jnp.exp(b_last + m - m_new)
    C_new = carry_decay[..., None, None] * C + jnp.einsum('bhs,bhsv,bhsd->bhvd', w, v, k)
    n_new = carry_decay[..., None] * n + jnp.einsum('bhs,bhsd->bhd', w, k)
    return (C_new, n_new, m_new), h


def mlstm_output(h, o, norm_g):
    b, _, t, _ = h.shape
    h = h.transpose(0, 2, 1, 3)
    hn = h * lax.rsqrt(jnp.mean(h * h, axis=-1, keepdims=True) + EPS) * norm_g.astype(jnp.float32).reshape(M_HEADS, M_DH)
    return hn.reshape(b, t, M_WIDTH) * o


def merge_branches(o_nsa, o_mlstm, zmerge, w_proj_nsa, w_proj_mlstm, w_out):
    g_a, g_b = jnp.split(jax.nn.sigmoid(zmerge.astype(jnp.float32)), 2, axis=-1)
    y = g_a * (o_nsa @ w_proj_nsa) + g_b * (o_mlstm.astype(o_nsa.dtype) @ w_proj_mlstm)
    return y.astype(o_nsa.dtype) @ w_out


def mix_prompt(h, w_mix_in, cmp_pos_k, cmp_pos_v, cmp_phi_k1, cmp_phi_k2, cmp_phi_v1, cmp_phi_v2, rel_bias,
               mlstm_conv_w, mlstm_conv_b, mlstm_wq, mlstm_wk, mlstm_gate_bias, mlstm_norm,
               w_proj_nsa, w_proj_mlstm, w_out):
    b, t, _ = h.shape
    (q, kc_rows, vc_rows, ks_rows, vs_rows, kw_rows, vw_rows, gates,
     zmqk, zmv, zmif, zmo, zmerge) = project_mix(h, w_mix_in)
    kc = compress(kc_rows, cmp_pos_k, cmp_phi_k1, cmp_phi_k2)
    vc = compress(vc_rows, cmp_pos_v, cmp_phi_v1, cmp_phi_v2)
    ks_t = ks_rows.transpose(0, 2, 1, 3)
    vs_t = vs_rows.transpose(0, 2, 1, 3)
    bi = jnp.arange(b)[:, None, None, None]
    gi = jnp.arange(N_KV)[None, :, None, None]

    def gather_sel(tok):
        return ks_t[bi, gi, tok], vs_t[bi, gi, tok]

    kw_pad = jnp.pad(kw_rows, ((0, 0), (WINDOW, 0), (0, 0), (0, 0)))
    vw_pad = jnp.pad(vw_rows, ((0, 0), (WINDOW, 0), (0, 0), (0, 0)))
    n_qb = t // Q_BLOCK

    def block_fn(args):
        qb, gb, start = args
        qpos = start + jnp.arange(Q_BLOCK, dtype=jnp.int32)
        kw = lax.dynamic_slice_in_dim(kw_pad, start, WINDOW + Q_BLOCK, axis=1)
        vw = lax.dynamic_slice_in_dim(vw_pad, start, WINDOW + Q_BLOCK, axis=1)
        kw_pos = start - WINDOW + jnp.arange(WINDOW + Q_BLOCK, dtype=jnp.int32)
        return nsa_attend(qb, qpos, kc, vc, gather_sel, kw, vw, kw_pos, gb, rel_bias, t)

    q_blocks = q.reshape(b, n_qb, Q_BLOCK, N_HEADS, HEAD_DIM).swapaxes(0, 1)
    g_blocks = gates.reshape(b, n_qb, Q_BLOCK, N_HEADS, 3).swapaxes(0, 1)
    starts = jnp.arange(n_qb, dtype=jnp.int32) * Q_BLOCK
    o_nsa = lax.map(block_fn, (q_blocks, g_blocks, starts)).swapaxes(0, 1).reshape(b, t, NSA_WIDTH)

    conv_in = jnp.pad(zmqk, ((0, 0), (CONV_W - 1, 0), (0, 0)))
    q_m, k_m, v_m, ig, lf, o_m = mlstm_inputs(conv_in, zmv, zmif, zmo, mlstm_conv_w, mlstm_conv_b,
                                              mlstm_wq, mlstm_wk, mlstm_gate_bias)
    nc = t // M_CHUNK

    def to_chunks(a):
        return jnp.moveaxis(a.reshape(a.shape[:2] + (nc, M_CHUNK) + a.shape[3:]), 2, 0)

    carry0 = (jnp.zeros((b, M_HEADS, M_DH, M_DH), jnp.float32), jnp.zeros((b, M_HEADS, M_DH), jnp.float32),
              jnp.zeros((b, M_HEADS), jnp.float32))
    (C_f, n_f, m_f), h_chunks = lax.scan(mlstm_chunk, carry0,
                                         (to_chunks(q_m), to_chunks(k_m), to_chunks(v_m), to_chunks(ig), to_chunks(lf)))
    h_m = jnp.moveaxis(h_chunks, 0, 2).reshape(b, M_HEADS, t, M_DH)
    o_mlstm = mlstm_output(h_m, o_m, mlstm_norm)
    mix = merge_branches(o_nsa, o_mlstm, zmerge, w_proj_nsa, w_proj_mlstm, w_out)
    n_keep = min(WINDOW, t)
    states = (kc_rows, vc_rows, ks_rows, vs_rows, kw_rows[:, t - n_keep:], vw_rows[:, t - n_keep:],
              C_f, n_f, m_f, conv_in[:, conv_in.shape[1] - (CONV_W - 1):])
    return mix, states


def mix_sample(h, cache_k_cmp, cache_v_cmp, cache_k_sel, cache_v_sel, cache_k_win, cache_v_win,
               state_mlstm_C, state_mlstm_n, state_mlstm_m, state_mlstm_conv, page_table,
               w_mix_in, cmp_pos_k, cmp_pos_v, cmp_phi_k1, cmp_phi_k2, cmp_phi_v1, cmp_phi_v2, rel_bias,
               mlstm_conv_w, mlstm_conv_b, mlstm_wq, mlstm_wk, mlstm_gate_bias, mlstm_norm,
               w_proj_nsa, w_proj_mlstm, w_out):
    b, s, _ = h.shape
    (q, kc_new, vc_new, ks_new, vs_new, kw_new, vw_new, gates,
     zmqk, zmv, zmif, zmo, zmerge) = project_mix(h, w_mix_in)
    n_pages = PAST_LEN // PAGE_SIZE
    t_tot = PAST_LEN + s

    def past_rows(cache):
        return cache[page_table].reshape(b, n_pages * PAGE_SIZE, N_KV, HEAD_DIM)

    kc = compress(jnp.concatenate([past_rows(cache_k_cmp).astype(kc_new.dtype), kc_new], axis=1),
                  cmp_pos_k, cmp_phi_k1, cmp_phi_k2)
    vc = compress(jnp.concatenate([past_rows(cache_v_cmp).astype(vc_new.dtype), vc_new], axis=1),
                  cmp_pos_v, cmp_phi_v1, cmp_phi_v2)
    bi = jnp.arange(b)[:, None, None, None]
    gi = jnp.arange(N_KV)[None, :, None, None]

    def gather_sel(tok):
        in_past = (tok < PAST_LEN)[..., None]
        phys = page_table[bi, jnp.minimum(tok // PAGE_SIZE, n_pages - 1)]
        off = tok % PAGE_SIZE
        new_i = jnp.clip(tok - PAST_LEN, 0, s - 1)
        ks = jnp.where(in_past, cache_k_sel[phys, off, gi].astype(ks_new.dtype), ks_new[bi, new_i, gi])
        vs = jnp.where(in_past, cache_v_sel[phys, off, gi].astype(vs_new.dtype), vs_new[bi, new_i, gi])
        return ks, vs

    kw = jnp.concatenate([cache_k_win.astype(kw_new.dtype), kw_new], axis=1)
    vw = jnp.concatenate([cache_v_win.astype(vw_new.dtype), vw_new], axis=1)
    wb = cache_k_win.shape[1]
    kw_pos = PAST_LEN - wb + jnp.arange(wb + s, dtype=jnp.int32)
    qpos = PAST_LEN + jnp.arange(s, dtype=jnp.int32)
    o_nsa = nsa_attend(q, qpos, kc, vc, gather_sel, kw, vw, kw_pos, gates, rel_bias, t_tot)

    conv_in = jnp.concatenate([state_mlstm_conv.astype(zmqk.dtype), zmqk], axis=1)
    q_m, k_m, v_m, ig, lf, o_m = mlstm_inputs(conv_in, zmv, zmif, zmo, mlstm_conv_w, mlstm_conv_b,
                                              mlstm_wq, mlstm_wk, mlstm_gate_bias)
    carry = (state_mlstm_C.astype(jnp.float32), state_mlstm_n.astype(jnp.float32), state_mlstm_m.astype(jnp.float32))
    (C_n, n_n, m_n), h_m = mlstm_chunk(carry, (q_m, k_m, v_m, ig, lf))
    o_mlstm = mlstm_output(h_m, o_m, mlstm_norm)
    mix = merge_branches(o_nsa, o_mlstm, zmerge, w_proj_nsa, w_proj_mlstm, w_out)
    n_keep = min(WINDOW, wb + s)
    states = (kc_new, vc_new, ks_new, vs_new, kw[:, wb + s - n_keep:], vw[:, wb + s - n_keep:],
              C_n, n_n, m_n, conv_in[:, conv_in.shape[1] - (CONV_W - 1):])
    return mix, states


def trunk_layer(x, mix_fn, norm_ffn1, ffn1_w_in, ffn1_w_out, norm_mix, norm_ffn2, ffn2_w_in, ffn2_w_out):
    x = x + (0.5 * swiglu(rmsnorm(x, norm_ffn1), ffn1_w_in, ffn1_w_out)).astype(x.dtype)
    mix, states = mix_fn(rmsnorm(x, norm_mix))
    x = x + mix.astype(x.dtype)
    x = x + (0.5 * swiglu(rmsnorm(x, norm_ffn2), ffn2_w_in, ffn2_w_out)).astype(x.dtype)
    return x, states


def setup_inputs(seed: int = 0) -> dict:
    key = jax.random.key(seed)
    it = iter(list(jax.random.split(key, 64)))
    nrm = lambda shape, scale=1.0: jax.random.normal(next(it), shape, jnp.float32) * scale
    gain = lambda n: 1.0 + nrm((n,), 0.01)
    n_pages = PAST_LEN // PAGE_SIZE
    n_pool = (DEC_BATCH * n_pages * 5) // 4
    win_buf = min(WINDOW, PAST_LEN)
    pool_shape = (n_pool, PAGE_SIZE, N_KV, HEAD_DIM)
    page_table = jax.random.permutation(next(it), n_pool)[:DEC_BATCH * n_pages].reshape(DEC_BATCH, n_pages).astype(jnp.int32)
    return {
        'x_prompt': nrm((BATCH, SEQ, D_MODEL)),
        'x_sample': nrm((DEC_BATCH, DEC_SEQ, D_MODEL)),
        'cache_k_cmp': nrm(pool_shape),
        'cache_v_cmp': nrm(pool_shape),
        'cache_k_sel': nrm(pool_shape),
        'cache_v_sel': nrm(pool_shape),
        'cache_k_win': nrm((DEC_BATCH, win_buf, N_KV, HEAD_DIM)),
        'cache_v_win': nrm((DEC_BATCH, win_buf, N_KV, HEAD_DIM)),
        'state_mlstm_C': nrm((DEC_BATCH, M_HEADS, M_DH, M_DH), 0.3),
        'state_mlstm_n': nrm((DEC_BATCH, M_HEADS, M_DH), 0.3),
        'state_mlstm_m': nrm((DEC_BATCH, M_HEADS)),
        'state_mlstm_conv': nrm((DEC_BATCH, CONV_W - 1, M_WIDTH)),
        'page_table': page_table,
        'norm_ffn1': gain(D_MODEL),
        'ffn1_w_in': nrm((D_MODEL, 2 * D_FF), D_MODEL ** -0.5),
        'ffn1_w_out': nrm((D_FF, D_MODEL), D_FF ** -0.5),
        'norm_mix': gain(D_MODEL),
        'w_mix_in': nrm((D_MODEL, MIX_IN_COLS), D_MODEL ** -0.5),
        'cmp_pos_k': nrm((L_CMP, HEAD_DIM), 0.02),
        'cmp_pos_v': nrm((L_CMP, HEAD_DIM), 0.02),
        'cmp_phi_k1': nrm((L_CMP, HEAD_DIM, PHI_HIDDEN), (L_CMP * HEAD_DIM) ** -0.5),
        'cmp_phi_k2': nrm((PHI_HIDDEN, HEAD_DIM), PHI_HIDDEN ** -0.5),
        'cmp_phi_v1': nrm((L_CMP, HEAD_DIM, PHI_HIDDEN), (L_CMP * HEAD_DIM) ** -0.5),
        'cmp_phi_v2': nrm((PHI_HIDDEN, HEAD_DIM), PHI_HIDDEN ** -0.5),
        'rel_bias': nrm((N_BUCKETS, N_HEADS), 0.5),
        'mlstm_conv_w': nrm((CONV_W, M_WIDTH), CONV_W ** -0.5),
        'mlstm_conv_b': nrm((M_WIDTH,), 0.01),
        'mlstm_wq': nrm((M_HEADS, M_DH, M_DH), M_DH ** -0.5),
        'mlstm_wk': nrm((M_HEADS, M_DH, M_DH), M_DH ** -0.5),
        'mlstm_gate_bias': jnp.stack([nrm((M_HEADS,), 0.1),
                                      jnp.linspace(3.0, 6.0, M_HEADS, dtype=jnp.float32) + nrm((M_HEADS,), 0.1)]),
        'mlstm_norm': gain(M_WIDTH),
        'w_proj_nsa': nrm((NSA_WIDTH, D_MODEL), NSA_WIDTH ** -0.5),
        'w_proj_mlstm': nrm((M_WIDTH, D_MODEL), M_WIDTH ** -0.5),
        'w_out': nrm((D_MODEL, D_MODEL), D_MODEL ** -0.5),
        'norm_ffn2': gain(D_MODEL),
        'ffn2_w_in': nrm((D_MODEL, 2 * D_FF), D_MODEL ** -0.5),
        'ffn2_w_out': nrm((D_FF, D_MODEL), D_FF ** -0.5),
        'norm_final': gain(D_MODEL),
    }


def reference(x_prompt, x_sample, cache_k_cmp, cache_v_cmp, cache_k_sel, cache_v_sel, cache_k_win, cache_v_win,
              state_mlstm_C, state_mlstm_n, state_mlstm_m, state_mlstm_conv, page_table,
              norm_ffn1, ffn1_w_in, ffn1_w_out, norm_mix, w_mix_in, cmp_pos_k, cmp_pos_v,
              cmp_phi_k1, cmp_phi_k2, cmp_phi_v1, cmp_phi_v2, rel_bias,
              mlstm_conv_w, mlstm_conv_b, mlstm_wq, mlstm_wk, mlstm_gate_bias, mlstm_norm,
              w_proj_nsa, w_proj_mlstm, w_out, norm_ffn2, ffn2_w_in, ffn2_w_out, norm_final):
    mw = (w_mix_in, cmp_pos_k, cmp_pos_v, cmp_phi_k1, cmp_phi_k2, cmp_phi_v1, cmp_phi_v2, rel_bias,
          mlstm_conv_w, mlstm_conv_b, mlstm_wq, mlstm_wk, mlstm_gate_bias, mlstm_norm,
          w_proj_nsa, w_proj_mlstm, w_out)
    ffw = (norm_ffn1, ffn1_w_in, ffn1_w_out, norm_mix, norm_ffn2, ffn2_w_in, ffn2_w_out)
    xp, xs = x_prompt, x_sample
    for _ in range(DEPTH):
        xp, st_p = trunk_layer(xp, lambda h: mix_prompt(h, *mw), *ffw)
        xs, st_s = trunk_layer(xs, lambda h: mix_sample(h, cache_k_cmp, cache_v_cmp, cache_k_sel, cache_v_sel,
                                                        cache_k_win, cache_v_win, state_mlstm_C, state_mlstm_n,
                                                        state_mlstm_m, state_mlstm_conv, page_table, *mw), *ffw)
    y_prompt = rmsnorm(xp, norm_final)
    y_sample = rmsnorm(xs, norm_final)
    k_cmp_p, v_cmp_p, k_sel_p, v_sel_p, k_win_p, v_win_p, C_p, n_p, m_p, conv_p = st_p
    k_cmp_s, v_cmp_s, k_sel_s, v_sel_s, k_win_s, v_win_s, C_s, n_s, m_s, conv_s = st_s
    return (y_prompt, y_sample, k_cmp_p, k_cmp_s, v_cmp_p, v_cmp_s, k_sel_p, k_sel_s, v_sel_p, v_sel_s,
            k_win_p, k_win_s, v_win_p, v_win_s, C_p, C_s, n_p, n_s, m_p, m_s, conv_p, conv_s)
```

```python
import math
from functools import partial

import numpy as np
import jax
import jax.numpy as jnp
from jax import lax
from jax.experimental import pallas as pl
from jax.experimental.pallas import tpu as pltpu

D_MODEL = 1024
SEQ = 8192
DEC_SEQ = 1
PAST_LEN = 8192
PAGE_SIZE = 128
N_HEADS = 8
N_KV = 2
GROUP = N_HEADS // N_KV
HEAD_DIM = 64
NSA_WIDTH = N_HEADS * HEAD_DIM
KV_WIDTH = N_KV * HEAD_DIM
L_CMP = 32
D_CMP = 16
CMP_RATIO = L_CMP // D_CMP
L_SEL = 64
K_SEL = 16
WINDOW = 512
Q_BLOCK = 128
FORCE_SCORE = 1000.0
N_BUCKETS = 32
MAX_DIST = 128
M_HEADS = 4
M_DH = 128
M_WIDTH = M_HEADS * M_DH
CONV_W = 4
M_CHUNK = 64
D_FF = 2816
EPS = 1e-6

V7X_VMEM_LIMIT_BYTES = 56 * 1024 * 1024
LANES = 128

FFN_TOKEN_TILE = 512
FFN_FF_TILE = 1408
PROJ_TOKEN_TILE = 512

_MIX_SIZES = (NSA_WIDTH, KV_WIDTH, KV_WIDTH, KV_WIDTH, KV_WIDTH, KV_WIDTH, KV_WIDTH, 3 * N_HEADS,
              M_WIDTH, M_WIDTH, 2 * M_HEADS, M_WIDTH, 2 * D_MODEL)
_MIX_OFFS = np.concatenate([[0], np.cumsum(_MIX_SIZES)]).tolist()
_PROJ_SEGS = (("q", NSA_WIDTH), ("kv", 6 * KV_WIDTH), ("mqk", M_WIDTH), ("mv", M_WIDTH),
              ("mo", M_WIDTH), ("merge", 2 * D_MODEL), ("small", LANES))


def _rms(x, g):
    return x * lax.rsqrt(jnp.mean(x * x, axis=-1, keepdims=True) + EPS) * g


def _ffn_kernel(x_ref, g_ref, wg_ref, wu_ref, wo_ref, gf_ref, o_ref, h_ref, acc_ref, *, final_norm):
    j = pl.program_id(1)

    @pl.when(j == 0)
    def _():
        h_ref[...] = _rms(x_ref[...], g_ref[...]).astype(jnp.bfloat16)
        acc_ref[...] = jnp.zeros_like(acc_ref)

    h = h_ref[...]
    gate = jnp.dot(h, wg_ref[...], preferred_element_type=jnp.float32)
    up = jnp.dot(h, wu_ref[...], preferred_element_type=jnp.float32)
    a = (jax.nn.silu(gate) * up).astype(jnp.bfloat16)
    acc_ref[...] += jnp.dot(a, wo_ref[...], preferred_element_type=jnp.float32)

    @pl.when(j == pl.num_programs(1) - 1)
    def _():
        y = x_ref[...] + 0.5 * acc_ref[...]
        if final_norm:
            y = _rms(y, gf_ref[...])
        o_ref[...] = y


def _ffn(x, g, w_in, w_out, g_final, *, final_norm):
    n = x.shape[0]
    tm = min(FFN_TOKEN_TILE, n)
    nj = D_FF // FFN_FF_TILE
    return pl.pallas_call(
        partial(_ffn_kernel, final_norm=final_norm),
        out_shape=jax.ShapeDtypeStruct((n, D_MODEL), jnp.float32),
        grid=(n // tm, nj),
        in_specs=[
            pl.BlockSpec((tm, D_MODEL), lambda i, j: (i, 0)),
            pl.BlockSpec((1, D_MODEL), lambda i, j: (0, 0)),
            pl.BlockSpec((D_MODEL, FFN_FF_TILE), lambda i, j: (0, j)),
            pl.BlockSpec((D_MODEL, FFN_FF_TILE), lambda i, j: (0, j + D_FF // FFN_FF_TILE)),
            pl.BlockSpec((FFN_FF_TILE, D_MODEL), lambda i, j: (j, 0)),
            pl.BlockSpec((1, D_MODEL), lambda i, j: (0, 0)),
        ],
        out_specs=pl.BlockSpec((tm, D_MODEL), lambda i, j: (i, 0)),
        scratch_shapes=[pltpu.VMEM((tm, D_MODEL), jnp.bfloat16), pltpu.VMEM((tm, D_MODEL), jnp.float32)],
        compiler_params=pltpu.CompilerParams(dimension_semantics=("arbitrary", "arbitrary"),
                                             vmem_limit_bytes=V7X_VMEM_LIMIT_BYTES),
        name="ffn",
    )(x, g.reshape(1, D_MODEL), w_in, w_in, w_out, g_final.reshape(1, D_MODEL))


def _proj_kernel(x_ref, g_ref, w_ref, *o_refs):
    h = _rms(x_ref[...], g_ref[...]).astype(jnp.bfloat16)
    off = 0
    for (_, width), o_ref in zip(_PROJ_SEGS, o_refs):
        o_ref[...] = jnp.dot(h, w_ref[:, off:off + width], preferred_element_type=jnp.float32)
        off += width


def _regroup_mix_weight(w_mix_in):
    seg = lambda i: w_mix_in[:, _MIX_OFFS[i]:_MIX_OFFS[i + 1]]
    small = jnp.concatenate([seg(7), seg(10), jnp.zeros((D_MODEL, LANES - 3 * N_HEADS - 2 * M_HEADS), w_mix_in.dtype)], axis=1)
    cols = [seg(0)] + [seg(i) for i in range(1, 7)] + [seg(8), seg(9), seg(11), seg(12), small]
    return jnp.concatenate(cols, axis=1).astype(jnp.bfloat16)


def _project(x, g, w_regrouped):
    n = x.shape[0]
    tm = min(PROJ_TOKEN_TILE, n)
    wcols = w_regrouped.shape[1]
    return pl.pallas_call(
        _proj_kernel,
        out_shape=[jax.ShapeDtypeStruct((n, width), jnp.float32) for _, width in _PROJ_SEGS],
        grid=(n // tm,),
        in_specs=[
            pl.BlockSpec((tm, D_MODEL), lambda i: (i, 0)),
            pl.BlockSpec((1, D_MODEL), lambda i: (0, 0)),
            pl.BlockSpec((D_MODEL, wcols), lambda i: (0, 0)),
        ],
        out_specs=[pl.BlockSpec((tm, width), lambda i: (i, 0)) for _, width in _PROJ_SEGS],
        compiler_params=pltpu.CompilerParams(dimension_semantics=("arbitrary",),
                                             vmem_limit_bytes=V7X_VMEM_LIMIT_BYTES),
        name="mix_in_proj",
    )(x, g.reshape(1, D_MODEL), w_regrouped)


def _merge_kernel(x_ref, on_ref, om_ref, zm_ref, wn_ref, wm_ref, wo_ref, o_ref):
    zm = zm_ref[...]
    g_a = jax.nn.sigmoid(zm[:, :D_MODEL])
    g_b = jax.nn.sigmoid(zm[:, D_MODEL:])
    ya = jnp.dot(on_ref[...].astype(jnp.bfloat16), wn_ref[...], preferred_element_type=jnp.float32)
    yb = jnp.dot(om_ref[...].astype(jnp.bfloat16), wm_ref[...], preferred_element_type=jnp.float32)
    y = (g_a * ya + g_b * yb).astype(jnp.bfloat16)
    o_ref[...] = x_ref[...] + jnp.dot(y, wo_ref[...], preferred_element_type=jnp.float32)


def _merge(x, o_nsa, o_mlstm, zmerge, w_proj_nsa, w_proj_mlstm, w_out):
    n = x.shape[0]
    tm = min(PROJ_TOKEN_TILE, n)
    row = lambda width: pl.BlockSpec((tm, width), lambda i: (i, 0))
    full = lambda a: pl.BlockSpec(a.shape, lambda i: (0, 0))
    return pl.pallas_call(
        _merge_kernel,
        out_shape=jax.ShapeDtypeStruct((n, D_MODEL), jnp.float32),
        grid=(n // tm,),
        in_specs=[row(D_MODEL), row(NSA_WIDTH), row(M_WIDTH), row(2 * D_MODEL),
                  full(w_proj_nsa), full(w_proj_mlstm), full(w_out)],
        out_specs=row(D_MODEL),
        compiler_params=pltpu.CompilerParams(dimension_semantics=("arbitrary",),
                                             vmem_limit_bytes=V7X_VMEM_LIMIT_BYTES),
        name="merge_out_proj",
    )(x, o_nsa, o_mlstm, zmerge, w_proj_nsa, w_proj_mlstm, w_out)


def _t5_bucket(dist):
    n = jnp.maximum(dist, 0)
    max_exact = N_BUCKETS // 2
    nf = jnp.maximum(n, 1).astype(jnp.float32)
    large = max_exact + (jnp.log(nf / max_exact) / math.log(MAX_DIST / max_exact)
                         * (N_BUCKETS - max_exact)).astype(jnp.int32)
    return jnp.where(n < max_exact, n, jnp.minimum(large, N_BUCKETS - 1))


def _masked_softmax(logits, mask):
    logits = jnp.where(mask, logits.astype(jnp.float32), -jnp.inf)
    m = jnp.max(logits, axis=-1, keepdims=True)
    m = jnp.where(jnp.isfinite(m), m, 0.0)
    p = jnp.exp(logits - m)
    return p / jnp.maximum(jnp.sum(p, axis=-1, keepdims=True), 1e-30)


def _compress(rows, pos_emb, w1, w2):
    b, t, g, d = rows.shape
    n_cmp = (t - L_CMP) // D_CMP + 1
    n_sub = n_cmp + CMP_RATIO - 1
    sub = rows[:, :n_sub * D_CMP].reshape(b, n_sub, D_CMP, g, d)
    hidden = None
    for r in range(CMP_RATIO):
        xr = sub[:, r:r + n_cmp] + pos_emb[r * D_CMP:(r + 1) * D_CMP][None, None, :, None, :]
        part = jnp.einsum('bnlgd,ldh->bngh', xr, w1[r * D_CMP:(r + 1) * D_CMP])
        hidden = part if hidden is None else hidden + part
    return jnp.einsum('bngh,hd->bngd', jax.nn.gelu(hidden), w2)


def _nsa_attend(q, qpos, kc, vc, gather_sel, kw, vw, kw_pos, gates, rel_bias, seq_len):
    b, tq = q.shape[:2]
    scale = HEAD_DIM ** -0.5
    qg = q.reshape(b, tq, N_KV, GROUP, HEAD_DIM)
    rb = rel_bias.astype(jnp.float32).reshape(N_BUCKETS, N_KV, GROUP)
    n_cmp = kc.shape[1]
    c_start = jnp.arange(n_cmp, dtype=jnp.int32) * D_CMP
    dist_c = qpos[:, None] - (c_start + L_CMP - 1)[None, :]
    bias_c = jnp.transpose(rb[_t5_bucket(dist_c)], (2, 3, 0, 1))
    s_c = jnp.einsum('bqgrd,bngd->bgrqn', qg, kc).astype(jnp.float32) * scale + bias_c
    p_c = _masked_softmax(s_c, dist_c >= 0)
    o_c = jnp.einsum('bgrqn,bngd->bqgrd', p_c, vc.astype(jnp.float32))
    n_sel = -(-seq_len // L_SEL)
    s_start = jnp.arange(n_sel, dtype=jnp.int32) * L_SEL
    overlap = ((c_start[:, None] <= s_start[None, :] + L_SEL - 1)
               & (c_start[:, None] + L_CMP - 1 >= s_start[None, :])).astype(jnp.float32)
    imp = jnp.einsum('bgrqn,ns->bgqs', p_c, overlap)
    cur = qpos // L_SEL
    blk = jnp.arange(n_sel, dtype=jnp.int32)
    forced = (blk[None, :] == 0) | (blk[None, :] == cur[:, None]) | (blk[None, :] == cur[:, None] - 1)
    valid = s_start[None, :] <= qpos[:, None]
    imp = jnp.where(valid, imp + jnp.where(forced, FORCE_SCORE, 0.0), -jnp.inf)
    _, top = lax.top_k(imp, min(K_SEL, n_sel))
    tok = (top[..., None] * L_SEL + jnp.arange(L_SEL, dtype=jnp.int32)).reshape(b, N_KV, tq, -1)
    ks, vs = gather_sel(tok)
    dist_s = qpos[None, None, :, None] - tok
    g_idx = jnp.arange(N_KV)[None, :, None, None]
    bias_s = jnp.moveaxis(jnp.transpose(rb, (1, 0, 2))[g_idx, _t5_bucket(dist_s)], -1, 2)
    s_s = jnp.einsum('bqgrd,bgqkd->bgrqk', qg, ks).astype(jnp.float32) * scale + bias_s
    p_s = _masked_softmax(s_s, (dist_s >= 0)[:, :, None])
    o_s = jnp.einsum('bgrqk,bgqkd->bqgrd', p_s, vs.astype(jnp.float32))
    dist_w = qpos[:, None] - kw_pos[None, :]
    mask_w = (dist_w >= 0) & (dist_w <= WINDOW) & (kw_pos[None, :] >= 0)
    bias_w = jnp.transpose(rb[_t5_bucket(dist_w)], (2, 3, 0, 1))
    s_w = jnp.einsum('bqgrd,bkgd->bgrqk', qg, kw).astype(jnp.float32) * scale + bias_w
    p_w = _masked_softmax(s_w, mask_w)
    o_w = jnp.einsum('bgrqk,bkgd->bqgrd', p_w, vw.astype(jnp.float32))
    g = gates.reshape(b, tq, N_KV, GROUP, 3)
    o = o_c * g[..., 0:1] + o_s * g[..., 1:2] + o_w * g[..., 2:3]
    return o.reshape(b, tq, NSA_WIDTH).astype(q.dtype)


def _mlstm_inputs(conv_in, v_src, if_logits, o_logits, conv_w, conv_b, wq, wk, gate_bias):
    b, t = v_src.shape[:2]
    conv = conv_b
    for j in range(CONV_W):
        conv = conv + conv_in[:, j:j + t] * conv_w[j]
    a = jax.nn.silu(conv).astype(jnp.float32).reshape(b, t, M_HEADS, M_DH)
    q = jnp.einsum('bthd,hde->bhte', a, wq.astype(jnp.float32))
    k = jnp.einsum('bthd,hde->bhte', a, wk.astype(jnp.float32)) * (M_DH ** -0.5)
    v = v_src.astype(jnp.float32).reshape(b, t, M_HEADS, M_DH).transpose(0, 2, 1, 3)
    gl = if_logits.astype(jnp.float32).reshape(b, t, 2, M_HEADS) + gate_bias.astype(jnp.float32)
    ig = gl[:, :, 0].transpose(0, 2, 1)
    lf = jax.nn.log_sigmoid(gl[:, :, 1]).transpose(0, 2, 1)
    o = jax.nn.sigmoid(o_logits.astype(jnp.float32))
    return q, k, v, ig, lf, o


def _mlstm_chunk(carry, xs):
    C, n, m = carry
    q, k, v, ig, lf = xs
    L = q.shape[2]
    bcum = jnp.cumsum(lf, axis=-1)
    causal = jnp.tril(jnp.ones((L, L), dtype=bool))
    dmat = jnp.where(causal, bcum[..., :, None] - bcum[..., None, :] + ig[..., None, :], -jnp.inf)
    inter = bcum + m[..., None]
    m_t = jnp.maximum(jnp.max(dmat, axis=-1), inter)
    sc = jnp.einsum('bhtd,bhsd->bhts', q, k) * jnp.exp(dmat - m_t[..., None])
    decay = jnp.exp(inter - m_t)
    num = decay[..., None] * jnp.einsum('bhvd,bhtd->bhtv', C, q) + jnp.einsum('bhts,bhsv->bhtv', sc, v)
    den = decay * jnp.einsum('bhd,bhtd->bht', n, q) + jnp.sum(sc, axis=-1)
    h = num / jnp.maximum(jnp.abs(den), jnp.exp(-m_t))[..., None]
    b_last = bcum[..., -1]
    w_log = b_last[..., None] - bcum + ig
    m_new = jnp.maximum(b_last + m, jnp.max(w_log, axis=-1))
    w = jnp.exp(w_log - m_new[..., None])
    carry_decay = jnp.exp(b_last + m - m_new)
    C_new = carry_decay[..., None, None] * C + jnp.einsum('bhs,bhsv,bhsd->bhvd', w, v, k)
    n_new = carry_decay[..., None] * n + jnp.einsum('bhs,bhsd->bhd', w, k)
    return (C_new, n_new, m_new), h


def _mlstm_output(h, o, norm_g):
    b, _, t, _ = h.shape
    h = h.transpose(0, 2, 1, 3)
    hn = h * lax.rsqrt(jnp.mean(h * h, axis=-1, keepdims=True) + EPS) * norm_g.astype(jnp.float32).reshape(M_HEADS, M_DH)
    return hn.reshape(b, t, M_WIDTH) * o


def _split_proj(outs, b, t):
    q, kv, zmqk, zmv, zmo, zmerge, small = outs
    r3 = lambda a: a.reshape(b, t, a.shape[-1])
    kvs = [kv[:, i * KV_WIDTH:(i + 1) * KV_WIDTH].reshape(b, t, N_KV, HEAD_DIM) for i in range(6)]
    gates = jax.nn.sigmoid(small[:, :3 * N_HEADS]).reshape(b, t, N_HEADS, 3)
    zmif = small[:, 3 * N_HEADS:3 * N_HEADS + 2 * M_HEADS].reshape(b, t, 2 * M_HEADS)
    return (q.reshape(b, t, N_HEADS, HEAD_DIM), *kvs, gates, r3(zmqk), r3(zmv), zmif, r3(zmo), zmerge)


def _mix_prompt(proj, b, t, cmp_pos_k, cmp_pos_v, cmp_phi_k1, cmp_phi_k2, cmp_phi_v1, cmp_phi_v2, rel_bias,
                mlstm_conv_w, mlstm_conv_b, mlstm_wq, mlstm_wk, mlstm_gate_bias, mlstm_norm):
    (q, kc_rows, vc_rows, ks_rows, vs_rows, kw_rows, vw_rows, gates,
     zmqk, zmv, zmif, zmo, zmerge) = _split_proj(proj, b, t)
    kc = _compress(kc_rows, cmp_pos_k, cmp_phi_k1, cmp_phi_k2)
    vc = _compress(vc_rows, cmp_pos_v, cmp_phi_v1, cmp_phi_v2)
    ks_t = ks_rows.transpose(0, 2, 1, 3)
    vs_t = vs_rows.transpose(0, 2, 1, 3)
    bi = jnp.arange(b)[:, None, None, None]
    gi = jnp.arange(N_KV)[None, :, None, None]

    def gather_sel(tok):
        return ks_t[bi, gi, tok], vs_t[bi, gi, tok]

    kw_pad = jnp.pad(kw_rows, ((0, 0), (WINDOW, 0), (0, 0), (0, 0)))
    vw_pad = jnp.pad(vw_rows, ((0, 0), (WINDOW, 0), (0, 0), (0, 0)))
    n_qb = t // Q_BLOCK

    def block_fn(args):
        qb, gb, start = args
        qpos = start + jnp.arange(Q_BLOCK, dtype=jnp.int32)
        kw = lax.dynamic_slice_in_dim(kw_pad, start, WINDOW + Q_BLOCK, axis=1)
        vw = lax.dynamic_slice_in_dim(vw_pad, start, WINDOW + Q_BLOCK, axis=1)
        kw_pos = start - WINDOW + jnp.arange(WINDOW + Q_BLOCK, dtype=jnp.int32)
        return _nsa_attend(qb, qpos, kc, vc, gather_sel, kw, vw, kw_pos, gb, rel_bias, t)

    q_blocks = q.reshape(b, n_qb, Q_BLOCK, N_HEADS, HEAD_DIM).swapaxes(0, 1)
    g_blocks = gates.reshape(b, n_qb, Q_BLOCK, N_HEADS, 3).swapaxes(0, 1)
    starts = jnp.arange(n_qb, dtype=jnp.int32) * Q_BLOCK
    o_nsa = lax.map(block_fn, (q_blocks, g_blocks, starts)).swapaxes(0, 1).reshape(b, t, NSA_WIDTH)

    conv_in = jnp.pad(zmqk, ((0, 0), (CONV_W - 1, 0), (0, 0)))
    q_m, k_m, v_m, ig, lf, o_m = _mlstm_inputs(conv_in, zmv, zmif, zmo, mlstm_conv_w, mlstm_conv_b,
                                               mlstm_wq, mlstm_wk, mlstm_gate_bias)
    nc = t // M_CHUNK

    def to_chunks(a):
        return jnp.moveaxis(a.reshape(a.shape[:2] + (nc, M_CHUNK) + a.shape[3:]), 2, 0)

    carry0 = (jnp.zeros((b, M_HEADS, M_DH, M_DH), jnp.float32), jnp.zeros((b, M_HEADS, M_DH), jnp.float32),
              jnp.zeros((b, M_HEADS), jnp.float32))
    (C_f, n_f, m_f), h_chunks = lax.scan(_mlstm_chunk, carry0,
                                         (to_chunks(q_m), to_chunks(k_m), to_chunks(v_m), to_chunks(ig), to_chunks(lf)))
    h_m = jnp.moveaxis(h_chunks, 0, 2).reshape(b, M_HEADS, t, M_DH)
    o_mlstm = _mlstm_output(h_m, o_m, mlstm_norm)
    n_keep = min(WINDOW, t)
    states = (kc_rows, vc_rows, ks_rows, vs_rows, kw_rows[:, t - n_keep:], vw_rows[:, t - n_keep:],
              C_f, n_f, m_f, conv_in[:, conv_in.shape[1] - (CONV_W - 1):])
    return o_nsa, o_mlstm, zmerge, states


def _mix_sample(proj, b, s, cache_k_cmp, cache_v_cmp, cache_k_sel, cache_v_sel, cache_k_win, cache_v_win,
                state_mlstm_C, state_mlstm_n, state_mlstm_m, state_mlstm_conv, page_table,
                cmp_pos_k, cmp_pos_v, cmp_phi_k1, cmp_phi_k2, cmp_phi_v1, cmp_phi_v2, rel_bias,
                mlstm_conv_w, mlstm_conv_b, mlstm_wq, mlstm_wk, mlstm_gate_bias, mlstm_norm):
    (q, kc_new, vc_new, ks_new, vs_new, kw_new, vw_new, gates,
     zmqk, zmv, zmif, zmo, zmerge) = _split_proj(proj, b, s)
    n_pages = PAST_LEN // PAGE_SIZE
    t_tot = PAST_LEN + s

    def past_rows(cache):
        return cache[page_table].reshape(b, n_pages * PAGE_SIZE, N_KV, HEAD_DIM)

    kc = _compress(jnp.concatenate([past_rows(cache_k_cmp), kc_new], axis=1), cmp_pos_k, cmp_phi_k1, cmp_phi_k2)
    vc = _compress(jnp.concatenate([past_rows(cache_v_cmp), vc_new], axis=1), cmp_pos_v, cmp_phi_v1, cmp_phi_v2)
    bi = jnp.arange(b)[:, None, None, None]
    gi = jnp.arange(N_KV)[None, :, None, None]

    def gather_sel(tok):
        in_past = (tok < PAST_LEN)[..., None]
        phys = page_table[bi, jnp.minimum(tok // PAGE_SIZE, n_pages - 1)]
        off = tok % PAGE_SIZE
        new_i = jnp.clip(tok - PAST_LEN, 0, s - 1)
        ks = jnp.where(in_past, cache_k_sel[phys, off, gi], ks_new[bi, new_i, gi])
        vs = jnp.where(in_past, cache_v_sel[phys, off, gi], vs_new[bi, new_i, gi])
        return ks, vs

    kw = jnp.concatenate([cache_k_win, kw_new], axis=1)
    vw = jnp.concatenate([cache_v_win, vw_new], axis=1)
    wb = cache_k_win.shape[1]
    kw_pos = PAST_LEN - wb + jnp.arange(wb + s, dtype=jnp.int32)
    qpos = PAST_LEN + jnp.arange(s, dtype=jnp.int32)
    o_nsa = _nsa_attend(q, qpos, kc, vc, gather_sel, kw, vw, kw_pos, gates, rel_bias, t_tot)

    conv_in = jnp.concatenate([state_mlstm_conv, zmqk], axis=1)
    q_m, k_m, v_m, ig, lf, o_m = _mlstm_inputs(conv_in, zmv, zmif, zmo, mlstm_conv_w, mlstm_conv_b,
                                               mlstm_wq, mlstm_wk, mlstm_gate_bias)
    carry = (state_mlstm_C, state_mlstm_n, state_mlstm_m)
    (C_n, n_n, m_n), h_m = _mlstm_chunk(carry, (q_m, k_m, v_m, ig, lf))
    o_mlstm = _mlstm_output(h_m, o_m, mlstm_norm)
    n_keep = min(WINDOW, wb + s)
    states = (kc_new, vc_new, ks_new, vs_new, kw[:, wb + s - n_keep:], vw[:, wb + s - n_keep:],
              C_n, n_n, m_n, conv_in[:, conv_in.shape[1] - (CONV_W - 1):])
    return o_nsa, o_mlstm, zmerge, states


def kernel(x_prompt, x_sample, cache_k_cmp, cache_v_cmp, cache_k_sel, cache_v_sel, cache_k_win, cache_v_win, state_mlstm_C, state_mlstm_n, state_mlstm_m, state_mlstm_conv, page_table, norm_ffn1, ffn1_w_in, ffn1_w_out, norm_mix, w_mix_in, cmp_pos_k, cmp_pos_v, cmp_phi_k1, cmp_phi_k2, cmp_phi_v1, cmp_phi_v2, rel_bias, mlstm_conv_w, mlstm_conv_b, mlstm_wq, mlstm_wk, mlstm_gate_bias, mlstm_norm, w_proj_nsa, w_proj_mlstm, w_out, norm_ffn2, ffn2_w_in, ffn2_w_out, norm_final):
    bf = lambda w: w.astype(jnp.bfloat16)
    w1i, w1o, w2i, w2o = bf(ffn1_w_in), bf(ffn1_w_out), bf(ffn2_w_in), bf(ffn2_w_out)
    w_mix = _regroup_mix_weight(w_mix_in)
    wn, wm, wo = bf(w_proj_nsa), bf(w_proj_mlstm), bf(w_out)
    mixw = (cmp_pos_k, cmp_pos_v, cmp_phi_k1, cmp_phi_k2, cmp_phi_v1, cmp_phi_v2, rel_bias,
            mlstm_conv_w, mlstm_conv_b, mlstm_wq, mlstm_wk, mlstm_gate_bias, mlstm_norm)

    def layer(x3, mix_fn):
        b, t, _ = x3.shape
        x = x3.reshape(b * t, D_MODEL)
        x1 = _ffn(x, norm_ffn1, w1i, w1o, norm_final, final_norm=False)
        proj = _project(x1, norm_mix, w_mix)
        o_nsa, o_mlstm, zmerge, states = mix_fn(proj, b, t)
        x2 = _merge(x1, o_nsa.reshape(b * t, NSA_WIDTH), o_mlstm.reshape(b * t, M_WIDTH), zmerge, wn, wm, wo)
        y = _ffn(x2, norm_ffn2, w2i, w2o, norm_final, final_norm=True)
        return y.reshape(b, t, D_MODEL), states

    y_prompt, st_p = layer(x_prompt, lambda proj, b, t: _mix_prompt(proj, b, t, *mixw))
    y_sample, st_s = layer(x_sample, lambda proj, b, t: _mix_sample(
        proj, b, t, cache_k_cmp, cache_v_cmp, cache_k_sel, cache_v_sel, cache_k_win, cache_v_win,
        state_mlstm_C, state_mlstm_n, state_mlstm_m, state_mlstm_conv, page_table, *mixw))
    k_cmp_p, v_cmp_p, k_sel_p, v_sel_p, k_win_p, v_win_p, C_p, n_p, m_p, conv_p = st_p
    k_cmp_s, v_cmp_s, k_sel_s, v_sel_s, k_win_s, v_win_s, C_s, n_s, m_s, conv_s = st_s
    return (y_prompt, y_sample, k_cmp_p, k_cmp_s, v_cmp_p, v_cmp_s, k_sel_p, k_sel_s, v_sel_p, v_sel_s,
            k_win_p, k_win_s, v_win_p, v_win_s, C_p, C_s, n_p, n_s, m_p, m_s, conv_p, conv_s)
```

```python
import math
from functools import partial

import numpy as np
import jax
import jax.numpy as jnp
from jax import lax
from jax.experimental import pallas as pl
from jax.experimental.pallas import tpu as pltpu

D_MODEL = 1024
SEQ = 8192
DEC_SEQ = 1
PAST_LEN = 8192
PAGE_SIZE = 128
N_HEADS = 8
N_KV = 2
GROUP = N_HEADS // N_KV
HEAD_DIM = 64
NSA_WIDTH = N_HEADS * HEAD_DIM
KV_WIDTH = N_KV * HEAD_DIM
L_CMP = 32
D_CMP = 16
CMP_RATIO = L_CMP // D_CMP
L_SEL = 64
K_SEL = 16
WINDOW = 512
Q_BLOCK = 128
FORCE_SCORE = 1000.0
N_BUCKETS = 32
MAX_DIST = 128
M_HEADS = 4
M_DH = 128
M_WIDTH = M_HEADS * M_DH
CONV_W = 4
M_CHUNK = 64
D_FF = 2816
EPS = 1e-6

V7X_VMEM_LIMIT_BYTES = 56 * 1024 * 1024
LANES = 128

QPAD_WIDTH = N_HEADS * LANES
N_SEL_PAD = 128
NEG_BIG = -1e30
SEL_OFF = -32768.0

FFN_TOKEN_TILE = 512
FFN_FF_TILE = 1408
PROJ_TOKEN_TILE = 512

_MIX_SIZES = (NSA_WIDTH, KV_WIDTH, KV_WIDTH, KV_WIDTH, KV_WIDTH, KV_WIDTH, KV_WIDTH, 3 * N_HEADS,
              M_WIDTH, M_WIDTH, 2 * M_HEADS, M_WIDTH, 2 * D_MODEL)
_MIX_OFFS = np.concatenate([[0], np.cumsum(_MIX_SIZES)]).tolist()
_PROJ_SEGS = (("q", NSA_WIDTH), ("kv", 6 * KV_WIDTH), ("mqk", M_WIDTH), ("mv", M_WIDTH),
              ("mo", M_WIDTH), ("merge", 2 * D_MODEL), ("small", LANES))


def _rms(x, g):
    return x * lax.rsqrt(jnp.mean(x * x, axis=-1, keepdims=True) + EPS) * g


def _ffn_kernel(x_ref, g_ref, wg_ref, wu_ref, wo_ref, gf_ref, o_ref, h_ref, acc_ref, *, final_norm):
    j = pl.program_id(1)

    @pl.when(j == 0)
    def _():
        h_ref[...] = _rms(x_ref[...], g_ref[...]).astype(jnp.bfloat16)
        acc_ref[...] = jnp.zeros_like(acc_ref)

    h = h_ref[...]
    gate = jnp.dot(h, wg_ref[...], preferred_element_type=jnp.float32)
    up = jnp.dot(h, wu_ref[...], preferred_element_type=jnp.float32)
    a = (jax.nn.silu(gate) * up).astype(jnp.bfloat16)
    acc_ref[...] += jnp.dot(a, wo_ref[...], preferred_element_type=jnp.float32)

    @pl.when(j == pl.num_programs(1) - 1)
    def _():
        y = x_ref[...] + 0.5 * acc_ref[...]
        if final_norm:
            y = _rms(y, gf_ref[...])
        o_ref[...] = y


def _ffn(x, g, w_in, w_out, g_final, *, final_norm):
    n = x.shape[0]
    tm = min(FFN_TOKEN_TILE, n)
    nj = D_FF // FFN_FF_TILE
    return pl.pallas_call(
        partial(_ffn_kernel, final_norm=final_norm),
        out_shape=jax.ShapeDtypeStruct((n, D_MODEL), jnp.float32),
        grid=(n // tm, nj),
        in_specs=[
            pl.BlockSpec((tm, D_MODEL), lambda i, j: (i, 0)),
            pl.BlockSpec((1, D_MODEL), lambda i, j: (0, 0)),
            pl.BlockSpec((D_MODEL, FFN_FF_TILE), lambda i, j: (0, j)),
            pl.BlockSpec((D_MODEL, FFN_FF_TILE), lambda i, j: (0, j + D_FF // FFN_FF_TILE)),
            pl.BlockSpec((FFN_FF_TILE, D_MODEL), lambda i, j: (j, 0)),
            pl.BlockSpec((1, D_MODEL), lambda i, j: (0, 0)),
        ],
        out_specs=pl.BlockSpec((tm, D_MODEL), lambda i, j: (i, 0)),
        scratch_shapes=[pltpu.VMEM((tm, D_MODEL), jnp.bfloat16), pltpu.VMEM((tm, D_MODEL), jnp.float32)],
        compiler_params=pltpu.CompilerParams(dimension_semantics=("arbitrary", "arbitrary"),
                                             vmem_limit_bytes=V7X_VMEM_LIMIT_BYTES),
        name="ffn",
    )(x, g.reshape(1, D_MODEL), w_in, w_in, w_out, g_final.reshape(1, D_MODEL))


def _proj_kernel(x_ref, g_ref, w_ref, qpad_ref, kvb_ref, *o_refs):
    h = _rms(x_ref[...], g_ref[...]).astype(jnp.bfloat16)
    zq = jnp.dot(h, w_ref[:, :QPAD_WIDTH], preferred_element_type=jnp.float32)
    qpad_ref[...] = (zq * (HEAD_DIM ** -0.5)).astype(jnp.bfloat16)
    off = QPAD_WIDTH
    for (name, width), o_ref in zip(_PROJ_SEGS, o_refs):
        z = jnp.dot(h, w_ref[:, off:off + width], preferred_element_type=jnp.float32)
        o_ref[...] = z
        if name == "kv":
            kvb_ref[...] = z.astype(jnp.bfloat16)
        off += width


def _regroup_mix_weight(w_mix_in):
    seg = lambda i: w_mix_in[:, _MIX_OFFS[i]:_MIX_OFFS[i + 1]]
    wq = seg(0)
    zeros = jnp.zeros((D_MODEL, HEAD_DIM), w_mix_in.dtype)
    qpad = []
    for h in range(N_HEADS):
        wh = wq[:, h * HEAD_DIM:(h + 1) * HEAD_DIM]
        qpad += [wh, zeros] if h // GROUP == 0 else [zeros, wh]
    small = jnp.concatenate([seg(7), seg(10), jnp.zeros((D_MODEL, LANES - 3 * N_HEADS - 2 * M_HEADS), w_mix_in.dtype)], axis=1)
    cols = qpad + [wq] + [seg(i) for i in range(1, 7)] + [seg(8), seg(9), seg(11), seg(12), small]
    return jnp.concatenate(cols, axis=1).astype(jnp.bfloat16)


def _project(x, g, w_regrouped):
    n = x.shape[0]
    tm = min(PROJ_TOKEN_TILE, n)
    wcols = w_regrouped.shape[1]
    row = lambda width: pl.BlockSpec((tm, width), lambda i: (i, 0))
    return pl.pallas_call(
        _proj_kernel,
        out_shape=[jax.ShapeDtypeStruct((n, QPAD_WIDTH), jnp.bfloat16),
                   jax.ShapeDtypeStruct((n, 6 * KV_WIDTH), jnp.bfloat16)]
                  + [jax.ShapeDtypeStruct((n, width), jnp.float32) for _, width in _PROJ_SEGS],
        grid=(n // tm,),
        in_specs=[
            row(D_MODEL),
            pl.BlockSpec((1, D_MODEL), lambda i: (0, 0)),
            pl.BlockSpec((D_MODEL, wcols), lambda i: (0, 0)),
        ],
        out_specs=[row(QPAD_WIDTH), row(6 * KV_WIDTH)] + [row(width) for _, width in _PROJ_SEGS],
        compiler_params=pltpu.CompilerParams(dimension_semantics=("arbitrary",),
                                             vmem_limit_bytes=V7X_VMEM_LIMIT_BYTES),
        name="mix_in_proj",
    )(x, g.reshape(1, D_MODEL), w_regrouped)


def _dot_nt(a, b):
    return lax.dot_general(a, b, (((1,), (1,)), ((), ())), preferred_element_type=jnp.float32)


def _bucket_table():
    d = np.arange(MAX_DIST + 1)
    max_exact = N_BUCKETS // 2
    nf = np.maximum(d, 1).astype(np.float64)
    large = max_exact + (np.log(nf / max_exact) / math.log(MAX_DIST / max_exact) * (N_BUCKETS - max_exact)).astype(np.int64)
    return np.where(d < max_exact, d, np.minimum(large, N_BUCKETS - 1)).astype(np.int32)


def _bias_tables(rel_bias, n_c):
    tab = rel_bias.astype(jnp.float32)[_bucket_table()].T
    i = np.arange(Q_BLOCK)[:, None]
    j = np.arange(Q_BLOCK)[None, :]
    t0 = tab[:, np.clip(i - j, 0, MAX_DIST)]
    t1 = tab[:, np.minimum(Q_BLOCK + i - j, MAX_DIST)]
    far = jnp.broadcast_to(tab[:, MAX_DIST][:, None, None], t0.shape)
    tiles = jnp.stack([t0, t1, far]).reshape(3, N_HEADS * Q_BLOCK, Q_BLOCK)
    cp = np.arange(-n_c, n_c)[:, None]
    dist = np.arange(Q_BLOCK)[None, :] - (L_CMP - 1) - D_CMP * cp
    bc = jnp.where(jnp.asarray(dist >= 0), tab[:, np.clip(dist, 0, MAX_DIST)], NEG_BIG)
    bct = bc.transpose(1, 0, 2).reshape(2 * n_c, N_HEADS * Q_BLOCK)
    return tiles, bct


def _nsa_prompt_kernel(q_ref, gate_ref, kc_ref, vct_ref, ks_ref, vs_ref, kw_ref, vw_ref, tiles_ref, bct_ref,
                       o_ref, m_ref, l_ref, acc_ref, imp_ref, *, n_c):
    f32, bf16 = jnp.float32, jnp.bfloat16
    qb = pl.program_id(1)
    rows_all = N_HEADS * Q_BLOCK
    qi = lax.broadcasted_iota(jnp.int32, (rows_all, Q_BLOCK), 0) & (Q_BLOCK - 1)
    kj = lax.broadcasted_iota(jnp.int32, (rows_all, Q_BLOCK), 1)
    causal = qi >= kj
    window_edge = kj >= qi
    sig = jax.nn.sigmoid(gate_ref[...])

    s_io = lax.broadcasted_iota(jnp.int32, (N_SEL_PAD, N_KV * Q_BLOCK), 0)
    i_io = lax.broadcasted_iota(jnp.int32, (N_SEL_PAD, N_KV * Q_BLOCK), 1) & (Q_BLOCK - 1)
    qpos = qb * Q_BLOCK + i_io
    cur = 2 * qb + (i_io >= L_SEL).astype(jnp.int32)
    blk_valid = s_io * L_SEL <= qpos
    blk_forced = (s_io == 0) | (s_io == cur) | (s_io == cur - 1)
    ov_s = lax.broadcasted_iota(jnp.int32, (N_SEL_PAD, n_c), 0)
    ov_c = lax.broadcasted_iota(jnp.int32, (N_SEL_PAD, n_c), 1)
    overlap_t = jnp.where((ov_c >= 4 * ov_s - 1) & (ov_c <= 4 * ov_s + 3), 1.0, 0.0).astype(bf16)
    key_half = (lax.broadcasted_iota(jnp.int32, (Q_BLOCK, N_SEL_PAD), 0) >= L_SEL).astype(jnp.int32)
    blk_lane = lax.broadcasted_iota(jnp.int32, (Q_BLOCK, N_SEL_PAD), 1)

    def flash_init():
        m_ref[...] = jnp.full(m_ref.shape, NEG_BIG, f32)
        l_ref[...] = jnp.zeros(l_ref.shape, f32)
        acc_ref[...] = jnp.zeros(acc_ref.shape, f32)

    def flash_step(s, v_tile):
        m_old = m_ref[...]
        m_new = jnp.maximum(m_old, jnp.max(s, axis=-1, keepdims=True))
        alpha = jnp.exp(m_old - m_new)
        p = jnp.exp(s - m_new)
        l_ref[...] = alpha * l_ref[...] + jnp.sum(p, axis=-1, keepdims=True)
        acc_ref[...] = alpha * acc_ref[...] + jnp.dot(p.astype(bf16), v_tile, preferred_element_type=f32)
        m_ref[...] = m_new

    def key_rows(kt):
        return pl.ds(pl.multiple_of(kt * Q_BLOCK, Q_BLOCK), Q_BLOCK)

    q_all = jnp.concatenate([q_ref[:, h * LANES:(h + 1) * LANES] for h in range(N_HEADS)], axis=0)

    st = _dot_nt(kc_ref[0], q_all)
    st = st + bct_ref[pl.ds(pl.multiple_of(n_c - 8 * qb, 8), n_c), :]
    mx = jnp.max(st, axis=0, keepdims=True)
    mx = jnp.where(mx < 0.1 * NEG_BIG, 0.0, mx)
    p = jnp.exp(st - mx)
    p = p / jnp.maximum(jnp.sum(p, axis=0, keepdims=True), 1e-30)
    pb = p.astype(bf16)
    oc_t = jnp.dot(vct_ref[0], pb, preferred_element_type=f32)

    imps = []
    for g in range(N_KV):
        imp_g = None
        for r in range(GROUP):
            h = GROUP * g + r
            part = jnp.dot(overlap_t, pb[:, h * Q_BLOCK:(h + 1) * Q_BLOCK], preferred_element_type=f32)
            imp_g = part if imp_g is None else imp_g + part
        imps.append(imp_g)
    imp = jnp.concatenate(imps, axis=1)
    imp = jnp.where(blk_valid, imp + jnp.where(blk_forced, FORCE_SCORE, 0.0), -jnp.inf)
    imp_ref[...] = imp

    def rank_body(s, rank):
        other = imp_ref[pl.ds(s, 1), :]
        beats = (other > imp) | ((other == imp) & (s_io > s))
        return rank + jnp.where(beats, 1, 0)

    rank = lax.fori_loop(0, 2 * qb + 2, rank_body, jnp.zeros(imp.shape, jnp.int32))
    sel_off = jnp.where(rank < K_SEL, 0.0, SEL_OFF)
    sel_rows = []
    for g in range(N_KV):
        sel_rows += [sel_off[:, g * Q_BLOCK:(g + 1) * Q_BLOCK].T.astype(bf16)] * GROUP
    lhs_sel = jnp.concatenate([jnp.concatenate(sel_rows, axis=0), q_all], axis=1)

    def sel_scores(kt):
        onehot = jnp.where(blk_lane == 2 * kt + key_half, 1.0, 0.0).astype(bf16)
        rhs = jnp.concatenate([onehot, ks_ref[0, key_rows(kt), :]], axis=1)
        return _dot_nt(lhs_sel, rhs)

    flash_init()

    def far_body(kt, carry):
        flash_step(sel_scores(kt) + tiles_ref[2], vs_ref[0, key_rows(kt), :])
        return carry

    lax.fori_loop(0, jnp.maximum(qb - 1, 0), far_body, 0)

    @pl.when(qb >= 1)
    def _():
        flash_step(sel_scores(qb - 1) + tiles_ref[1], vs_ref[0, key_rows(qb - 1), :])

    flash_step(jnp.where(causal, sel_scores(qb) + tiles_ref[0], NEG_BIG), vs_ref[0, key_rows(qb), :])
    o_sel = acc_ref[...] / l_ref[...]

    flash_init()
    for dt in range(WINDOW // Q_BLOCK + 1):
        @pl.when(qb >= dt)
        def _():
            s = _dot_nt(q_all, kw_ref[0, key_rows(qb - dt), :]) + tiles_ref[min(dt, 2)]
            if dt == 0:
                s = jnp.where(causal, s, NEG_BIG)
            if dt == WINDOW // Q_BLOCK:
                s = jnp.where(window_edge, s, NEG_BIG)
            flash_step(s, vw_ref[0, key_rows(qb - dt), :])
    o_win = acc_ref[...] / l_ref[...]

    for h in range(N_HEADS):
        rs = slice(h * Q_BLOCK, (h + 1) * Q_BLOCK)
        o = (oc_t[:, rs].T * sig[:, 3 * h:3 * h + 1] + o_sel[rs] * sig[:, 3 * h + 1:3 * h + 2]
             + o_win[rs] * sig[:, 3 * h + 2:3 * h + 3])
        o_ref[:, h * LANES:(h + 1) * LANES] = o.astype(bf16)


def _nsa_prompt(qpad, small, kvb, kc, vc, rel_bias, b, t):
    n_c = t // D_CMP
    n_qb = t // Q_BLOCK
    pad = lambda a: jnp.pad(a.reshape(b, a.shape[1], KV_WIDTH), ((0, 0), (0, n_c - a.shape[1]), (0, 0)))
    kcb = pad(kc).astype(jnp.bfloat16)
    vct = pad(vc).astype(jnp.bfloat16).transpose(0, 2, 1)
    tiles, bct = _bias_tables(rel_bias, n_c)
    kv3 = kvb.reshape(b, t, 6 * KV_WIDTH)
    rows = lambda width: pl.BlockSpec((Q_BLOCK, width), lambda bi, qi: (bi * n_qb + qi, 0))
    seq = lambda lane_block: pl.BlockSpec((1, t, KV_WIDTH), lambda bi, qi: (bi, 0, lane_block))
    full = lambda a: pl.BlockSpec(a.shape, lambda bi, qi: (0,) * a.ndim)
    rows_all = N_HEADS * Q_BLOCK
    return pl.pallas_call(
        partial(_nsa_prompt_kernel, n_c=n_c),
        out_shape=jax.ShapeDtypeStruct((b * t, QPAD_WIDTH), jnp.bfloat16),
        grid=(b, n_qb),
        in_specs=[rows(QPAD_WIDTH), rows(LANES),
                  pl.BlockSpec((1, n_c, KV_WIDTH), lambda bi, qi: (bi, 0, 0)),
                  pl.BlockSpec((1, KV_WIDTH, n_c), lambda bi, qi: (bi, 0, 0)),
                  seq(2), seq(3), seq(4), seq(5), full(tiles), full(bct)],
        out_specs=rows(QPAD_WIDTH),
        scratch_shapes=[pltpu.VMEM((rows_all, LANES), jnp.float32), pltpu.VMEM((rows_all, LANES), jnp.float32),
                        pltpu.VMEM((rows_all, KV_WIDTH), jnp.float32),
                        pltpu.VMEM((N_SEL_PAD, N_KV * Q_BLOCK), jnp.float32)],
        compiler_params=pltpu.CompilerParams(dimension_semantics=("arbitrary", "arbitrary"),
                                             vmem_limit_bytes=V7X_VMEM_LIMIT_BYTES),
        name="nsa_prompt",
    )(qpad, small, kcb, vct, kv3, kv3, kv3, kv3, tiles, bct)


def _pad_nsa_out_weight(w_proj_nsa):
    zeros = jnp.zeros((HEAD_DIM, D_MODEL), w_proj_nsa.dtype)
    rows = []
    for h in range(N_HEADS):
        wh = w_proj_nsa[h * HEAD_DIM:(h + 1) * HEAD_DIM]
        rows += [wh, zeros] if h // GROUP == 0 else [zeros, wh]
    return jnp.concatenate(rows, axis=0)


def _merge_kernel(x_ref, on_ref, om_ref, zm_ref, wn_ref, wm_ref, wo_ref, o_ref):
    zm = zm_ref[...]
    g_a = jax.nn.sigmoid(zm[:, :D_MODEL])
    g_b = jax.nn.sigmoid(zm[:, D_MODEL:])
    ya = jnp.dot(on_ref[...].astype(jnp.bfloat16), wn_ref[...], preferred_element_type=jnp.float32)
    yb = jnp.dot(om_ref[...].astype(jnp.bfloat16), wm_ref[...], preferred_element_type=jnp.float32)
    y = (g_a * ya + g_b * yb).astype(jnp.bfloat16)
    o_ref[...] = x_ref[...] + jnp.dot(y, wo_ref[...], preferred_element_type=jnp.float32)


def _merge(x, o_nsa, o_mlstm, zmerge, w_proj_nsa, w_proj_mlstm, w_out):
    n = x.shape[0]
    tm = min(PROJ_TOKEN_TILE, n)
    row = lambda width: pl.BlockSpec((tm, width), lambda i: (i, 0))
    full = lambda a: pl.BlockSpec(a.shape, lambda i: (0, 0))
    return pl.pallas_call(
        _merge_kernel,
        out_shape=jax.ShapeDtypeStruct((n, D_MODEL), jnp.float32),
        grid=(n // tm,),
        in_specs=[row(D_MODEL), row(o_nsa.shape[1]), row(M_WIDTH), row(2 * D_MODEL),
                  full(w_proj_nsa), full(w_proj_mlstm), full(w_out)],
        out_specs=row(D_MODEL),
        compiler_params=pltpu.CompilerParams(dimension_semantics=("arbitrary",),
                                             vmem_limit_bytes=V7X_VMEM_LIMIT_BYTES),
        name="merge_out_proj",
    )(x, o_nsa, o_mlstm, zmerge, w_proj_nsa, w_proj_mlstm, w_out)


def _t5_bucket(dist):
    n = jnp.maximum(dist, 0)
    max_exact = N_BUCKETS // 2
    nf = jnp.maximum(n, 1).astype(jnp.float32)
    large = max_exact + (jnp.log(nf / max_exact) / math.log(MAX_DIST / max_exact)
                         * (N_BUCKETS - max_exact)).astype(jnp.int32)
    return jnp.where(n < max_exact, n, jnp.minimum(large, N_BUCKETS - 1))


def _masked_softmax(logits, mask):
    logits = jnp.where(mask, logits.astype(jnp.float32), -jnp.inf)
    m = jnp.max(logits, axis=-1, keepdims=True)
    m = jnp.where(jnp.isfinite(m), m, 0.0)
    p = jnp.exp(logits - m)
    return p / jnp.maximum(jnp.sum(p, axis=-1, keepdims=True), 1e-30)


def _compress(rows, pos_emb, w1, w2):
    b, t, g, d = rows.shape
    n_cmp = (t - L_CMP) // D_CMP + 1
    n_sub = n_cmp + CMP_RATIO - 1
    sub = rows[:, :n_sub * D_CMP].reshape(b, n_sub, D_CMP, g, d)
    hidden = None
    for r in range(CMP_RATIO):
        xr = sub[:, r:r + n_cmp] + pos_emb[r * D_CMP:(r + 1) * D_CMP][None, None, :, None, :]
        part = jnp.einsum('bnlgd,ldh->bngh', xr, w1[r * D_CMP:(r + 1) * D_CMP])
        hidden = part if hidden is None else hidden + part
    return jnp.einsum('bngh,hd->bngd', jax.nn.gelu(hidden), w2)


def _nsa_attend(q, qpos, kc, vc, gather_sel, kw, vw, kw_pos, gates, rel_bias, seq_len):
    b, tq = q.shape[:2]
    scale = HEAD_DIM ** -0.5
    qg = q.reshape(b, tq, N_KV, GROUP, HEAD_DIM)
    rb = rel_bias.astype(jnp.float32).reshape(N_BUCKETS, N_KV, GROUP)
    n_cmp = kc.shape[1]
    c_start = jnp.arange(n_cmp, dtype=jnp.int32) * D_CMP
    dist_c = qpos[:, None] - (c_start + L_CMP - 1)[None, :]
    bias_c = jnp.transpose(rb[_t5_bucket(dist_c)], (2, 3, 0, 1))
    s_c = jnp.einsum('bqgrd,bngd->bgrqn', qg, kc).astype(jnp.float32) * scale + bias_c
    p_c = _masked_softmax(s_c, dist_c >= 0)
    o_c = jnp.einsum('bgrqn,bngd->bqgrd', p_c, vc.astype(jnp.float32))
    n_sel = -(-seq_len // L_SEL)
    s_start = jnp.arange(n_sel, dtype=jnp.int32) * L_SEL
    overlap = ((c_start[:, None] <= s_start[None, :] + L_SEL - 1)
               & (c_start[:, None] + L_CMP - 1 >= s_start[None, :])).astype(jnp.float32)
    imp = jnp.einsum('bgrqn,ns->bgqs', p_c, overlap)
    cur = qpos // L_SEL
    blk = jnp.arange(n_sel, dtype=jnp.int32)
    forced = (blk[None, :] == 0) | (blk[None, :] == cur[:, None]) | (blk[None, :] == cur[:, None] - 1)
    valid = s_start[None, :] <= qpos[:, None]
    imp = jnp.where(valid, imp + jnp.where(forced, FORCE_SCORE, 0.0), -jnp.inf)
    _, top = lax.top_k(imp, min(K_SEL, n_sel))
    tok = (top[..., None] * L_SEL + jnp.arange(L_SEL, dtype=jnp.int32)).reshape(b, N_KV, tq, -1)
    ks, vs = gather_sel(tok)
    dist_s = qpos[None, None, :, None] - tok
    g_idx = jnp.arange(N_KV)[None, :, None, None]
    bias_s = jnp.moveaxis(jnp.transpose(rb, (1, 0, 2))[g_idx, _t5_bucket(dist_s)], -1, 2)
    s_s = jnp.einsum('bqgrd,bgqkd->bgrqk', qg, ks).astype(jnp.float32) * scale + bias_s
    p_s = _masked_softmax(s_s, (dist_s >= 0)[:, :, None])
    o_s = jnp.einsum('bgrqk,bgqkd->bqgrd', p_s, vs.astype(jnp.float32))
    dist_w = qpos[:, None] - kw_pos[None, :]
    mask_w = (dist_w >= 0) & (dist_w <= WINDOW) & (kw_pos[None, :] >= 0)
    bias_w = jnp.transpose(rb[_t5_bucket(dist_w)], (2, 3, 0, 1))
    s_w = jnp.einsum('bqgrd,bkgd->bgrqk', qg, kw).astype(jnp.float32) * scale + bias_w
    p_w = _masked_softmax(s_w, mask_w)
    o_w = jnp.einsum('bgrqk,bkgd->bqgrd', p_w, vw.astype(jnp.float32))
    g = gates.reshape(b, tq, N_KV, GROUP, 3)
    o = o_c * g[..., 0:1] + o_s * g[..., 1:2] + o_w * g[..., 2:3]
    return o.reshape(b, tq, NSA_WIDTH).astype(q.dtype)


def _mlstm_inputs(conv_in, v_src, if_logits, o_logits, conv_w, conv_b, wq, wk, gate_bias):
    b, t = v_src.shape[:2]
    conv = conv_b
    for j in range(CONV_W):
        conv = conv + conv_in[:, j:j + t] * conv_w[j]
    a = jax.nn.silu(conv).astype(jnp.float32).reshape(b, t, M_HEADS, M_DH)
    q = jnp.einsum('bthd,hde->bhte', a, wq.astype(jnp.float32))
    k = jnp.einsum('bthd,hde->bhte', a, wk.astype(jnp.float32)) * (M_DH ** -0.5)
    v = v_src.astype(jnp.float32).reshape(b, t, M_HEADS, M_DH).transpose(0, 2, 1, 3)
    gl = if_logits.astype(jnp.float32).reshape(b, t, 2, M_HEADS) + gate_bias.astype(jnp.float32)
    ig = gl[:, :, 0].transpose(0, 2, 1)
    lf = jax.nn.log_sigmoid(gl[:, :, 1]).transpose(0, 2, 1)
    o = jax.nn.sigmoid(o_logits.astype(jnp.float32))
    return q, k, v, ig, lf, o


def _mlstm_chunk(carry, xs):
    C, n, m = carry
    q, k, v, ig, lf = xs
    L = q.shape[2]
    bcum = jnp.cumsum(lf, axis=-1)
    causal = jnp.tril(jnp.ones((L, L), dtype=bool))
    dmat = jnp.where(causal, bcum[..., :, None] - bcum[..., None, :] + ig[..., None, :], -jnp.inf)
    inter = bcum + m[..., None]
    m_t = jnp.maximum(jnp.max(dmat, axis=-1), inter)
    sc = jnp.einsum('bhtd,bhsd->bhts', q, k) * jnp.exp(dmat - m_t[..., None])
    decay = jnp.exp(inter - m_t)
    num = decay[..., None] * jnp.einsum('bhvd,bhtd->bhtv', C, q) + jnp.einsum('bhts,bhsv->bhtv', sc, v)
    den = decay * jnp.einsum('bhd,bhtd->bht', n, q) + jnp.sum(sc, axis=-1)
    h = num / jnp.maximum(jnp.abs(den), jnp.exp(-m_t))[..., None]
    b_last = bcum[..., -1]
    w_log = b_last[..., None] - bcum + ig
    m_new = jnp.maximum(b_last + m, jnp.max(w_log, axis=-1))
    w = jnp.exp(w_log - m_new[..., None])
    carry_decay = jnp.exp(b_last + m - m_new)
    C_new = carry_decay[..., None, None] * C + jnp.einsum('bhs,bhsv,bhsd->bhvd', w, v, k)
    n_new = carry_decay[..., None] * n + jnp.einsum('bhs,bhsd->bhd', w, k)
    return (C_new, n_new, m_new), h


def _mlstm_output(h, o, norm_g):
    b, _, t, _ = h.shape
    h = h.transpose(0, 2, 1, 3)
    hn = h * lax.rsqrt(jnp.mean(h * h, axis=-1, keepdims=True) + EPS) * norm_g.astype(jnp.float32).reshape(M_HEADS, M_DH)
    return hn.reshape(b, t, M_WIDTH) * o


def _split_proj(outs, b, t):
    _, _, q, kv, zmqk, zmv, zmo, zmerge, small = outs
    r3 = lambda a: a.reshape(b, t, a.shape[-1])
    kvs = [kv[:, i * KV_WIDTH:(i + 1) * KV_WIDTH].reshape(b, t, N_KV, HEAD_DIM) for i in range(6)]
    gates = jax.nn.sigmoid(small[:, :3 * N_HEADS]).reshape(b, t, N_HEADS, 3)
    zmif = small[:, 3 * N_HEADS:3 * N_HEADS + 2 * M_HEADS].reshape(b, t, 2 * M_HEADS)
    return (q.reshape(b, t, N_HEADS, HEAD_DIM), *kvs, gates, r3(zmqk), r3(zmv), zmif, r3(zmo), zmerge)


def _mix_prompt(proj, b, t, cmp_pos_k, cmp_pos_v, cmp_phi_k1, cmp_phi_k2, cmp_phi_v1, cmp_phi_v2, rel_bias,
                mlstm_conv_w, mlstm_conv_b, mlstm_wq, mlstm_wk, mlstm_gate_bias, mlstm_norm):
    (q, kc_rows, vc_rows, ks_rows, vs_rows, kw_rows, vw_rows, gates,
     zmqk, zmv, zmif, zmo, zmerge) = _split_proj(proj, b, t)
    qpad, kvb, small = proj[0], proj[1], proj[-1]
    kc = _compress(kc_rows, cmp_pos_k, cmp_phi_k1, cmp_phi_k2)
    vc = _compress(vc_rows, cmp_pos_v, cmp_phi_v1, cmp_phi_v2)
    o_nsa = _nsa_prompt(qpad, small, kvb, kc, vc, rel_bias, b, t)

    conv_in = jnp.pad(zmqk, ((0, 0), (CONV_W - 1, 0), (0, 0)))
    q_m, k_m, v_m, ig, lf, o_m = _mlstm_inputs(conv_in, zmv, zmif, zmo, mlstm_conv_w, mlstm_conv_b,
                                               mlstm_wq, mlstm_wk, mlstm_gate_bias)
    nc = t // M_CHUNK

    def to_chunks(a):
        return jnp.moveaxis(a.reshape(a.shape[:2] + (nc, M_CHUNK) + a.shape[3:]), 2, 0)

    carry0 = (jnp.zeros((b, M_HEADS, M_DH, M_DH), jnp.float32), jnp.zeros((b, M_HEADS, M_DH), jnp.float32),
              jnp.zeros((b, M_HEADS), jnp.float32))
    (C_f, n_f, m_f), h_chunks = lax.scan(_mlstm_chunk, carry0,
                                         (to_chunks(q_m), to_chunks(k_m), to_chunks(v_m), to_chunks(ig), to_chunks(lf)))
    h_m = jnp.moveaxis(h_chunks, 0, 2).reshape(b, M_HEADS, t, M_DH)
    o_mlstm = _mlstm_output(h_m, o_m, mlstm_norm)
    n_keep = min(WINDOW, t)
    states = (kc_rows, vc_rows, ks_rows, vs_rows, kw_rows[:, t - n_keep:], vw_rows[:, t - n_keep:],
              C_f, n_f, m_f, conv_in[:, conv_in.shape[1] - (CONV_W - 1):])
    return o_nsa, o_mlstm, zmerge, states


def _mix_sample(proj, b, s, cache_k_cmp, cache_v_cmp, cache_k_sel, cache_v_sel, cache_k_win, cache_v_win,
                state_mlstm_C, state_mlstm_n, state_mlstm_m, state_mlstm_conv, page_table,
                cmp_pos_k, cmp_pos_v, cmp_phi_k1, cmp_phi_k2, cmp_phi_v1, cmp_phi_v2, rel_bias,
                mlstm_conv_w, mlstm_conv_b, mlstm_wq, mlstm_wk, mlstm_gate_bias, mlstm_norm):
    (q, kc_new, vc_new, ks_new, vs_new, kw_new, vw_new, gates,
     zmqk, zmv, zmif, zmo, zmerge) = _split_proj(proj, b, s)
    n_pages = PAST_LEN // PAGE_SIZE
    t_tot = PAST_LEN + s

    def past_rows(cache):
        return cache[page_table].reshape(b, n_pages * PAGE_SIZE, N_KV, HEAD_DIM)

    kc = _compress(jnp.concatenate([past_rows(cache_k_cmp), kc_new], axis=1), cmp_pos_k, cmp_phi_k1, cmp_phi_k2)
    vc = _compress(jnp.concatenate([past_rows(cache_v_cmp), vc_new], axis=1), cmp_pos_v, cmp_phi_v1, cmp_phi_v2)
    bi = jnp.arange(b)[:, None, None, None]
    gi = jnp.arange(N_KV)[None, :, None, None]

    def gather_sel(tok):
        in_past = (tok < PAST_LEN)[..., None]
        phys = page_table[bi, jnp.minimum(tok // PAGE_SIZE, n_pages - 1)]
        off = tok % PAGE_SIZE
        new_i = jnp.clip(tok - PAST_LEN, 0, s - 1)
        ks = jnp.where(in_past, cache_k_sel[phys, off, gi], ks_new[bi, new_i, gi])
        vs = jnp.where(in_past, cache_v_sel[phys, off, gi], vs_new[bi, new_i, gi])
        return ks, vs

    kw = jnp.concatenate([cache_k_win, kw_new], axis=1)
    vw = jnp.concatenate([cache_v_win, vw_new], axis=1)
    wb = cache_k_win.shape[1]
    kw_pos = PAST_LEN - wb + jnp.arange(wb + s, dtype=jnp.int32)
    qpos = PAST_LEN + jnp.arange(s, dtype=jnp.int32)
    o_nsa = _nsa_attend(q, qpos, kc, vc, gather_sel, kw, vw, kw_pos, gates, rel_bias, t_tot)

    conv_in = jnp.concatenate([state_mlstm_conv, zmqk], axis=1)
    q_m, k_m, v_m, ig, lf, o_m = _mlstm_inputs(conv_in, zmv, zmif, zmo, mlstm_conv_w, mlstm_conv_b,
                                               mlstm_wq, mlstm_wk, mlstm_gate_bias)
    carry = (state_mlstm_C, state_mlstm_n, state_mlstm_m)
    (C_n, n_n, m_n), h_m = _mlstm_chunk(carry, (q_m, k_m, v_m, ig, lf))
    o_mlstm = _mlstm_output(h_m, o_m, mlstm_norm)
    n_keep = min(WINDOW, wb + s)
    states = (kc_new, vc_new, ks_new, vs_new, kw[:, wb + s - n_keep:], vw[:, wb + s - n_keep:],
              C_n, n_n, m_n, conv_in[:, conv_in.shape[1] - (CONV_W - 1):])
    return o_nsa, o_mlstm, zmerge, states


def kernel(x_prompt, x_sample, cache_k_cmp, cache_v_cmp, cache_k_sel, cache_v_sel, cache_k_win, cache_v_win, state_mlstm_C, state_mlstm_n, state_mlstm_m, state_mlstm_conv, page_table, norm_ffn1, ffn1_w_in, ffn1_w_out, norm_mix, w_mix_in, cmp_pos_k, cmp_pos_v, cmp_phi_k1, cmp_phi_k2, cmp_phi_v1, cmp_phi_v2, rel_bias, mlstm_conv_w, mlstm_conv_b, mlstm_wq, mlstm_wk, mlstm_gate_bias, mlstm_norm, w_proj_nsa, w_proj_mlstm, w_out, norm_ffn2, ffn2_w_in, ffn2_w_out, norm_final):
    bf = lambda w: w.astype(jnp.bfloat16)
    w1i, w1o, w2i, w2o = bf(ffn1_w_in), bf(ffn1_w_out), bf(ffn2_w_in), bf(ffn2_w_out)
    w_mix = _regroup_mix_weight(w_mix_in)
    wn, wm, wo = bf(w_proj_nsa), bf(w_proj_mlstm), bf(w_out)
    mixw = (cmp_pos_k, cmp_pos_v, cmp_phi_k1, cmp_phi_k2, cmp_phi_v1, cmp_phi_v2, rel_bias,
            mlstm_conv_w, mlstm_conv_b, mlstm_wq, mlstm_wk, mlstm_gate_bias, mlstm_norm)

    def layer(x3, mix_fn, w_nsa_out):
        b, t, _ = x3.shape
        x = x3.reshape(b * t, D_MODEL)
        x1 = _ffn(x, norm_ffn1, w1i, w1o, norm_final, final_norm=False)
        proj = _project(x1, norm_mix, w_mix)
        o_nsa, o_mlstm, zmerge, states = mix_fn(proj, b, t)
        x2 = _merge(x1, o_nsa.reshape(b * t, -1), o_mlstm.reshape(b * t, M_WIDTH), zmerge, w_nsa_out, wm, wo)
        y = _ffn(x2, norm_ffn2, w2i, w2o, norm_final, final_norm=True)
        return y.reshape(b, t, D_MODEL), states

    y_prompt, st_p = layer(x_prompt, lambda proj, b, t: _mix_prompt(proj, b, t, *mixw), bf(_pad_nsa_out_weight(w_proj_nsa)))
    y_sample, st_s = layer(x_sample, lambda proj, b, t: _mix_sample(
        proj, b, t, cache_k_cmp, cache_v_cmp, cache_k_sel, cache_v_sel, cache_k_win, cache_v_win,
        state_mlstm_C, state_mlstm_n, state_mlstm_m, state_mlstm_conv, page_table, *mixw), wn)
    k_cmp_p, v_cmp_p, k_sel_p, v_sel_p, k_win_p, v_win_p, C_p, n_p, m_p, conv_p = st_p
    k_cmp_s, v_cmp_s, k_sel_s, v_sel_s, k_win_s, v_win_s, C_s, n_s, m_s, conv_s = st_s
    return (y_prompt, y_sample, k_cmp_p, k_cmp_s, v_cmp_p, v_cmp_s, k_sel_p, k_sel_s, v_sel_p, v_sel_s,
            k_win_p, k_win_s, v_win_p, v_win_s, C_p, C_s, n_p, n_s, m_p, m_s, conv_p, conv_s)
```

```python
import math
from functools import partial

import numpy as np
import jax
import jax.numpy as jnp
from jax import lax
from jax.experimental import pallas as pl
from jax.experimental.pallas import tpu as pltpu

D_MODEL = 1024
SEQ = 8192
DEC_SEQ = 1
PAST_LEN = 8192
PAGE_SIZE = 128
N_HEADS = 8
N_KV = 2
GROUP = N_HEADS // N_KV
HEAD_DIM = 64
NSA_WIDTH = N_HEADS * HEAD_DIM
KV_WIDTH = N_KV * HEAD_DIM
L_CMP = 32
D_CMP = 16
CMP_RATIO = L_CMP // D_CMP
L_SEL = 64
K_SEL = 16
WINDOW = 512
Q_BLOCK = 128
FORCE_SCORE = 1000.0
N_BUCKETS = 32
MAX_DIST = 128
M_HEADS = 4
M_DH = 128
M_WIDTH = M_HEADS * M_DH
CONV_W = 4
M_CHUNK = 64
D_FF = 2816
EPS = 1e-6

V7X_VMEM_LIMIT_BYTES = 56 * 1024 * 1024
LANES = 128

QPAD_WIDTH = N_HEADS * LANES
N_SEL_PAD = 128
NEG_BIG = -1e30
SEL_OFF = -32768.0

FFN_TOKEN_TILE = 512
FFN_FF_TILE = 1408
PROJ_TOKEN_TILE = 512

_MIX_SIZES = (NSA_WIDTH, KV_WIDTH, KV_WIDTH, KV_WIDTH, KV_WIDTH, KV_WIDTH, KV_WIDTH, 3 * N_HEADS,
              M_WIDTH, M_WIDTH, 2 * M_HEADS, M_WIDTH, 2 * D_MODEL)
_MIX_OFFS = np.concatenate([[0], np.cumsum(_MIX_SIZES)]).tolist()
_PROJ_SEGS = (("kv", 6 * KV_WIDTH), ("mqk", M_WIDTH), ("mv", M_WIDTH),
              ("mo", M_WIDTH), ("merge", 2 * D_MODEL), ("small", LANES))


def _rms(x, g):
    return x * lax.rsqrt(jnp.mean(x * x, axis=-1, keepdims=True) + EPS) * g


def _ffn_kernel(x_ref, g_ref, wg_ref, wu_ref, wo_ref, gf_ref, o_ref, h_ref, acc_ref, *, final_norm):
    j = pl.program_id(1)

    @pl.when(j == 0)
    def _():
        h_ref[...] = _rms(x_ref[...], g_ref[...]).astype(jnp.bfloat16)
        acc_ref[...] = jnp.zeros_like(acc_ref)

    h = h_ref[...]
    gate = jnp.dot(h, wg_ref[...], preferred_element_type=jnp.float32)
    up = jnp.dot(h, wu_ref[...], preferred_element_type=jnp.float32)
    a = (jax.nn.silu(gate) * up).astype(jnp.bfloat16)
    acc_ref[...] += jnp.dot(a, wo_ref[...], preferred_element_type=jnp.float32)

    @pl.when(j == pl.num_programs(1) - 1)
    def _():
        y = x_ref[...] + 0.5 * acc_ref[...]
        if final_norm:
            y = _rms(y, gf_ref[...])
        o_ref[...] = y


def _ffn(x, g, w_in, w_out, g_final, *, final_norm):
    n = x.shape[0]
    tm = min(FFN_TOKEN_TILE, n)
    nj = D_FF // FFN_FF_TILE
    return pl.pallas_call(
        partial(_ffn_kernel, final_norm=final_norm),
        out_shape=jax.ShapeDtypeStruct((n, D_MODEL), jnp.float32),
        grid=(n // tm, nj),
        in_specs=[
            pl.BlockSpec((tm, D_MODEL), lambda i, j: (i, 0)),
            pl.BlockSpec((1, D_MODEL), lambda i, j: (0, 0)),
            pl.BlockSpec((D_MODEL, FFN_FF_TILE), lambda i, j: (0, j)),
            pl.BlockSpec((D_MODEL, FFN_FF_TILE), lambda i, j: (0, j + D_FF // FFN_FF_TILE)),
            pl.BlockSpec((FFN_FF_TILE, D_MODEL), lambda i, j: (j, 0)),
            pl.BlockSpec((1, D_MODEL), lambda i, j: (0, 0)),
        ],
        out_specs=pl.BlockSpec((tm, D_MODEL), lambda i, j: (i, 0)),
        scratch_shapes=[pltpu.VMEM((tm, D_MODEL), jnp.bfloat16), pltpu.VMEM((tm, D_MODEL), jnp.float32)],
        compiler_params=pltpu.CompilerParams(dimension_semantics=("arbitrary", "arbitrary"),
                                             vmem_limit_bytes=V7X_VMEM_LIMIT_BYTES),
        name="ffn",
    )(x, g.reshape(1, D_MODEL), w_in, w_in, w_out, g_final.reshape(1, D_MODEL))


def _proj_kernel(x_ref, g_ref, w_ref, qpad_ref, kvb_ref, *o_refs):
    h = _rms(x_ref[...], g_ref[...]).astype(jnp.bfloat16)
    zq = jnp.dot(h, w_ref[:, :QPAD_WIDTH], preferred_element_type=jnp.float32)
    qpad_ref[...] = (zq * (HEAD_DIM ** -0.5)).astype(jnp.bfloat16)
    off = QPAD_WIDTH
    o_refs = list(o_refs)
    for name, width in _PROJ_SEGS:
        z = jnp.dot(h, w_ref[:, off:off + width], preferred_element_type=jnp.float32)
        if name == "kv":
            kvb_ref[...] = z.astype(jnp.bfloat16)
            for i in range(6):
                o_refs.pop(0)[...] = z[:, i * KV_WIDTH:(i + 1) * KV_WIDTH]
        else:
            o_refs.pop(0)[...] = z
        off += width


def _regroup_mix_weight(w_mix_in):
    seg = lambda i: w_mix_in[:, _MIX_OFFS[i]:_MIX_OFFS[i + 1]]
    wq = seg(0)
    zeros = jnp.zeros((D_MODEL, HEAD_DIM), w_mix_in.dtype)
    qpad = []
    for h in range(N_HEADS):
        wh = wq[:, h * HEAD_DIM:(h + 1) * HEAD_DIM]
        qpad += [wh, zeros] if h // GROUP == 0 else [zeros, wh]
    small = jnp.concatenate([seg(7), seg(10), jnp.zeros((D_MODEL, LANES - 3 * N_HEADS - 2 * M_HEADS), w_mix_in.dtype)], axis=1)
    cols = qpad + [seg(i) for i in range(1, 7)] + [seg(8), seg(9), seg(11), seg(12), small]
    return jnp.concatenate(cols, axis=1).astype(jnp.bfloat16)


def _project(x, g, w_regrouped):
    n = x.shape[0]
    tm = min(PROJ_TOKEN_TILE, n)
    wcols = w_regrouped.shape[1]
    row = lambda width: pl.BlockSpec((tm, width), lambda i: (i, 0))
    widths = []
    for name, width in _PROJ_SEGS:
        widths += [KV_WIDTH] * 6 if name == "kv" else [width]
    return pl.pallas_call(
        _proj_kernel,
        out_shape=[jax.ShapeDtypeStruct((n, QPAD_WIDTH), jnp.bfloat16),
                   jax.ShapeDtypeStruct((n, 6 * KV_WIDTH), jnp.bfloat16)]
                  + [jax.ShapeDtypeStruct((n, width), jnp.float32) for width in widths],
        grid=(n // tm,),
        in_specs=[
            row(D_MODEL),
            pl.BlockSpec((1, D_MODEL), lambda i: (0, 0)),
            pl.BlockSpec((D_MODEL, wcols), lambda i: (0, 0)),
        ],
        out_specs=[row(QPAD_WIDTH), row(6 * KV_WIDTH)] + [row(width) for width in widths],
        compiler_params=pltpu.CompilerParams(dimension_semantics=("arbitrary",),
                                             vmem_limit_bytes=V7X_VMEM_LIMIT_BYTES),
        name="mix_in_proj",
    )(x, g.reshape(1, D_MODEL), w_regrouped)


def _dot_nt(a, b):
    return lax.dot_general(a, b, (((1,), (1,)), ((), ())), preferred_element_type=jnp.float32)


def _bucket_table():
    d = np.arange(MAX_DIST + 1)
    max_exact = N_BUCKETS // 2
    nf = np.maximum(d, 1).astype(np.float64)
    large = max_exact + (np.log(nf / max_exact) / math.log(MAX_DIST / max_exact) * (N_BUCKETS - max_exact)).astype(np.int64)
    return np.where(d < max_exact, d, np.minimum(large, N_BUCKETS - 1)).astype(np.int32)


def _bias_tables(rel_bias, n_c):
    tab = rel_bias.astype(jnp.float32)[_bucket_table()].T
    i = np.arange(Q_BLOCK)[:, None]
    j = np.arange(Q_BLOCK)[None, :]
    t0 = tab[:, np.clip(i - j, 0, MAX_DIST)]
    t1 = tab[:, np.minimum(Q_BLOCK + i - j, MAX_DIST)]
    far = jnp.broadcast_to(tab[:, MAX_DIST][:, None, None], t0.shape)
    tiles = jnp.stack([t0, t1, far]).reshape(3, N_HEADS * Q_BLOCK, Q_BLOCK)
    cp = np.arange(-n_c, n_c)[:, None]
    dist = np.arange(Q_BLOCK)[None, :] - (L_CMP - 1) - D_CMP * cp
    bc = jnp.where(jnp.asarray(dist >= 0), tab[:, np.clip(dist, 0, MAX_DIST)], NEG_BIG)
    bct = bc.transpose(1, 0, 2).reshape(2 * n_c, N_HEADS * Q_BLOCK)
    return tiles, bct


def _nsa_prompt_kernel(q_ref, gate_ref, kc_ref, vct_ref, ks_ref, vs_ref, kw_ref, vw_ref, tiles_ref, bct_ref,
                       o_ref, m_ref, l_ref, acc_ref, imp_ref, *, n_c):
    f32, bf16 = jnp.float32, jnp.bfloat16
    qb = pl.program_id(1)
    rows_all = N_HEADS * Q_BLOCK
    qi = lax.broadcasted_iota(jnp.int32, (rows_all, Q_BLOCK), 0) & (Q_BLOCK - 1)
    kj = lax.broadcasted_iota(jnp.int32, (rows_all, Q_BLOCK), 1)
    causal = qi >= kj
    window_edge = kj >= qi
    sig = jax.nn.sigmoid(gate_ref[...])

    s_io = lax.broadcasted_iota(jnp.int32, (N_SEL_PAD, N_KV * Q_BLOCK), 0)
    i_io = lax.broadcasted_iota(jnp.int32, (N_SEL_PAD, N_KV * Q_BLOCK), 1) & (Q_BLOCK - 1)
    qpos = qb * Q_BLOCK + i_io
    cur = 2 * qb + (i_io >= L_SEL).astype(jnp.int32)
    blk_valid = s_io * L_SEL <= qpos
    blk_forced = (s_io == 0) | (s_io == cur) | (s_io == cur - 1)
    ov_s = lax.broadcasted_iota(jnp.int32, (N_SEL_PAD, n_c), 0)
    ov_c = lax.broadcasted_iota(jnp.int32, (N_SEL_PAD, n_c), 1)
    overlap_t = jnp.where((ov_c >= 4 * ov_s - 1) & (ov_c <= 4 * ov_s + 3), 1.0, 0.0).astype(bf16)
    key_half = (lax.broadcasted_iota(jnp.int32, (Q_BLOCK, N_SEL_PAD), 0) >= L_SEL).astype(jnp.int32)
    blk_lane = lax.broadcasted_iota(jnp.int32, (Q_BLOCK, N_SEL_PAD), 1)

    def flash_init():
        m_ref[...] = jnp.full(m_ref.shape, NEG_BIG, f32)
        l_ref[...] = jnp.zeros(l_ref.shape, f32)
        acc_ref[...] = jnp.zeros(acc_ref.shape, f32)

    def flash_step(s, v_tile):
        m_old = m_ref[...]
        m_new = jnp.maximum(m_old, jnp.max(s, axis=-1, keepdims=True))
        alpha = jnp.exp(m_old - m_new)
        p = jnp.exp(s - m_new)
        l_ref[...] = alpha * l_ref[...] + jnp.sum(p, axis=-1, keepdims=True)
        acc_ref[...] = alpha * acc_ref[...] + jnp.dot(p.astype(bf16), v_tile, preferred_element_type=f32)
        m_ref[...] = m_new

    def key_rows(kt):
        return pl.ds(pl.multiple_of(kt * Q_BLOCK, Q_BLOCK), Q_BLOCK)

    q_all = jnp.concatenate([q_ref[:, h * LANES:(h + 1) * LANES] for h in range(N_HEADS)], axis=0)

    st = _dot_nt(kc_ref[0], q_all)
    st = st + bct_ref[pl.ds(pl.multiple_of(n_c - 8 * qb, 8), n_c), :]
    mx = jnp.max(st, axis=0, keepdims=True)
    mx = jnp.where(mx < 0.1 * NEG_BIG, 0.0, mx)
    p = jnp.exp(st - mx)
    p = p / jnp.maximum(jnp.sum(p, axis=0, keepdims=True), 1e-30)
    pb = p.astype(bf16)
    oc_t = jnp.dot(vct_ref[0], pb, preferred_element_type=f32)

    imps = []
    for g in range(N_KV):
        imp_g = None
        for r in range(GROUP):
            h = GROUP * g + r
            part = jnp.dot(overlap_t, pb[:, h * Q_BLOCK:(h + 1) * Q_BLOCK], preferred_element_type=f32)
            imp_g = part if imp_g is None else imp_g + part
        imps.append(imp_g)
    imp = jnp.concatenate(imps, axis=1)
    imp = jnp.where(blk_valid, imp + jnp.where(blk_forced, FORCE_SCORE, 0.0), -jnp.inf)
    imp_ref[...] = imp

    def rank_body(s, rank):
        other = imp_ref[pl.ds(s, 1), :]
        beats = (other > imp) | ((other == imp) & (s_io > s))
        return rank + jnp.where(beats, 1, 0)

    rank = lax.fori_loop(0, 2 * qb + 2, rank_body, jnp.zeros(imp.shape, jnp.int32))
    sel_off = jnp.where(rank < K_SEL, 0.0, SEL_OFF)
    sel_rows = []
    for g in range(N_KV):
        sel_rows += [sel_off[:, g * Q_BLOCK:(g + 1) * Q_BLOCK].T.astype(bf16)] * GROUP
    lhs_sel = jnp.concatenate([jnp.concatenate(sel_rows, axis=0), q_all], axis=1)

    def sel_scores(kt):
        onehot = jnp.where(blk_lane == 2 * kt + key_half, 1.0, 0.0).astype(bf16)
        rhs = jnp.concatenate([onehot, ks_ref[0, key_rows(kt), :]], axis=1)
        return _dot_nt(lhs_sel, rhs)

    flash_init()

    def far_body(kt, carry):
        flash_step(sel_scores(kt) + tiles_ref[2], vs_ref[0, key_rows(kt), :])
        return carry

    lax.fori_loop(0, jnp.maximum(qb - 1, 0), far_body, 0)

    @pl.when(qb >= 1)
    def _():
        flash_step(sel_scores(qb - 1) + tiles_ref[1], vs_ref[0, key_rows(qb - 1), :])

    flash_step(jnp.where(causal, sel_scores(qb) + tiles_ref[0], NEG_BIG), vs_ref[0, key_rows(qb), :])
    o_sel = acc_ref[...] / l_ref[...]

    flash_init()
    for dt in range(WINDOW // Q_BLOCK + 1):
        @pl.when(qb >= dt)
        def _():
            s = _dot_nt(q_all, kw_ref[0, key_rows(qb - dt), :]) + tiles_ref[min(dt, 2)]
            if dt == 0:
                s = jnp.where(causal, s, NEG_BIG)
            if dt == WINDOW // Q_BLOCK:
                s = jnp.where(window_edge, s, NEG_BIG)
            flash_step(s, vw_ref[0, key_rows(qb - dt), :])
    o_win = acc_ref[...] / l_ref[...]

    for h in range(N_HEADS):
        rs = slice(h * Q_BLOCK, (h + 1) * Q_BLOCK)
        o = (oc_t[:, rs].T * sig[:, 3 * h:3 * h + 1] + o_sel[rs] * sig[:, 3 * h + 1:3 * h + 2]
             + o_win[rs] * sig[:, 3 * h + 2:3 * h + 3])
        o_ref[:, h * LANES:(h + 1) * LANES] = o.astype(bf16)


def _nsa_prompt(qpad, small, kvb, kcb, vct, rel_bias, b, t):
    n_c = t // D_CMP
    n_qb = t // Q_BLOCK
    tiles, bct = _bias_tables(rel_bias, n_c)
    kv3 = kvb.reshape(b, t, 6 * KV_WIDTH)
    rows = lambda width: pl.BlockSpec((Q_BLOCK, width), lambda bi, qi: (bi * n_qb + qi, 0))
    seq = lambda lane_block: pl.BlockSpec((1, t, KV_WIDTH), lambda bi, qi: (bi, 0, lane_block))
    full = lambda a: pl.BlockSpec(a.shape, lambda bi, qi: (0,) * a.ndim)
    rows_all = N_HEADS * Q_BLOCK
    return pl.pallas_call(
        partial(_nsa_prompt_kernel, n_c=n_c),
        out_shape=jax.ShapeDtypeStruct((b * t, QPAD_WIDTH), jnp.bfloat16),
        grid=(b, n_qb),
        in_specs=[rows(QPAD_WIDTH), rows(LANES),
                  pl.BlockSpec((1, n_c, KV_WIDTH), lambda bi, qi: (bi, 0, 0)),
                  pl.BlockSpec((1, KV_WIDTH, n_c), lambda bi, qi: (bi, 0, 0)),
                  seq(2), seq(3), seq(4), seq(5), full(tiles), full(bct)],
        out_specs=rows(QPAD_WIDTH),
        scratch_shapes=[pltpu.VMEM((rows_all, LANES), jnp.float32), pltpu.VMEM((rows_all, LANES), jnp.float32),
                        pltpu.VMEM((rows_all, KV_WIDTH), jnp.float32),
                        pltpu.VMEM((N_SEL_PAD, N_KV * Q_BLOCK), jnp.float32)],
        compiler_params=pltpu.CompilerParams(dimension_semantics=("arbitrary", "arbitrary"),
                                             vmem_limit_bytes=V7X_VMEM_LIMIT_BYTES),
        name="nsa_prompt",
    )(qpad, small, kcb, vct, kv3, kv3, kv3, kv3, tiles, bct)


def _pad_nsa_out_weight(w_proj_nsa):
    zeros = jnp.zeros((HEAD_DIM, D_MODEL), w_proj_nsa.dtype)
    rows = []
    for h in range(N_HEADS):
        wh = w_proj_nsa[h * HEAD_DIM:(h + 1) * HEAD_DIM]
        rows += [wh, zeros] if h // GROUP == 0 else [zeros, wh]
    return jnp.concatenate(rows, axis=0)


SUB_WIDTH = D_CMP * KV_WIDTH
PHI_HIDDEN = 2 * HEAD_DIM
PAGE_SUBS = PAGE_SIZE // D_CMP
CMP_PAGES_PER_STEP = 16


def _compress_weights(pos_emb, w1, w2):
    eye = jnp.eye(N_KV, dtype=w1.dtype)
    halves = []
    for r in range(CMP_RATIO):
        w1r = w1[r * D_CMP:(r + 1) * D_CMP]
        halves.append(jnp.einsum('ldh,gk->lgdkh', w1r, eye).reshape(SUB_WIDTH, N_KV * PHI_HIDDEN))
    w1big = jnp.concatenate(halves, axis=1).astype(jnp.bfloat16)
    pos = jnp.broadcast_to(pos_emb.reshape(CMP_RATIO, D_CMP, 1, HEAD_DIM), (CMP_RATIO, D_CMP, N_KV, HEAD_DIM))
    w2big = jnp.einsum('hd,gk->ghkd', w2, eye).reshape(N_KV * PHI_HIDDEN, KV_WIDTH).astype(jnp.bfloat16)
    return w1big, pos.reshape(CMP_RATIO, 1, SUB_WIDTH), w2big


def _compress_tokens(lhs0, lhs1, w1big, w2big):
    hw = N_KV * PHI_HIDDEN
    p0 = jnp.dot(lhs0, w1big[:, :hw], preferred_element_type=jnp.float32)
    p1 = jnp.dot(lhs1, w1big[:, hw:], preferred_element_type=jnp.float32)
    n_sub = p1.shape[0]
    hidden = p0 + pltpu.roll(p1, n_sub - 1, 0)
    return jnp.dot(jax.nn.gelu(hidden).astype(jnp.bfloat16), w2big, preferred_element_type=jnp.float32)


def _compress_rows_kernel(rk_ref, rv_ref, w1k_ref, pk_ref, w2k_ref, w1v_ref, pv_ref, w2v_ref, kc_ref, vc_ref):
    for r_ref, w1_ref, p_ref, w2_ref, o_ref in ((rk_ref, w1k_ref, pk_ref, w2k_ref, kc_ref),
                                                 (rv_ref, w1v_ref, pv_ref, w2v_ref, vc_ref)):
        x = r_ref[0]
        lhs = [(x + p_ref[r]).astype(jnp.bfloat16) for r in range(CMP_RATIO)]
        o_ref[0] = _compress_tokens(lhs[0], lhs[1], w1_ref[...], w2_ref[...]).astype(o_ref.dtype)


def _compress_rows(k_rows, v_rows, b, t, wk, wv):
    n_sub = t // D_CMP
    view = lambda a: a.reshape(b, n_sub, SUB_WIDTH)
    seq = pl.BlockSpec((1, n_sub, SUB_WIDTH), lambda bi: (bi, 0, 0))
    full = lambda a: pl.BlockSpec(a.shape, lambda bi: (0,) * a.ndim)
    out = pl.BlockSpec((1, n_sub, KV_WIDTH), lambda bi: (bi, 0, 0))
    return pl.pallas_call(
        _compress_rows_kernel,
        out_shape=[jax.ShapeDtypeStruct((b, n_sub, KV_WIDTH), jnp.bfloat16)] * 2,
        grid=(b,),
        in_specs=[seq, seq] + [full(a) for a in (*wk, *wv)],
        out_specs=[out, out],
        compiler_params=pltpu.CompilerParams(dimension_semantics=("arbitrary",),
                                             vmem_limit_bytes=V7X_VMEM_LIMIT_BYTES),
        name="compress_rows",
    )(view(k_rows), view(v_rows), *wk, *wv)


def _compress_pages_kernel(pt_ref, *refs):
    n_pg = CMP_PAGES_PER_STEP
    k_pages, v_pages = refs[:n_pg], refs[n_pg:2 * n_pg]
    w1k_ref, pk_ref, w2k_ref, w1v_ref, pv_ref, w2v_ref, kc_ref, vc_ref, lhs_ref = refs[2 * n_pg:]
    step = pl.program_id(1)
    for c, (pages, p_ref) in enumerate(((k_pages, pk_ref), (v_pages, pv_ref))):
        for j in range(0, n_pg, 2):
            x = jnp.concatenate([pages[j][0], pages[j + 1][0]], axis=0)
            rows = pl.ds(pl.multiple_of(step * n_pg * PAGE_SUBS + j * PAGE_SUBS, 2 * PAGE_SUBS), 2 * PAGE_SUBS)
            for r in range(CMP_RATIO):
                lhs_ref[c, r, rows, :] = (x + p_ref[r]).astype(jnp.bfloat16)

    @pl.when(step == pl.num_programs(1) - 1)
    def _():
        for c, (w1_ref, w2_ref, o_ref) in enumerate(((w1k_ref, w2k_ref, kc_ref), (w1v_ref, w2v_ref, vc_ref))):
            o_ref[0] = _compress_tokens(lhs_ref[c, 0], lhs_ref[c, 1], w1_ref[...], w2_ref[...]).astype(o_ref.dtype)


def _compress_pages(cache_k, cache_v, page_table, wk, wv):
    b, n_pages = page_table.shape
    n_pool = cache_k.shape[0]
    n_sub = n_pages * PAGE_SUBS
    n_pg = CMP_PAGES_PER_STEP
    view = lambda c: c.reshape(n_pool, PAGE_SUBS, SUB_WIDTH)
    page = lambda j: pl.BlockSpec((1, PAGE_SUBS, SUB_WIDTH), lambda bi, si, pt: (pt[bi, si * n_pg + j], 0, 0))
    full = lambda a: pl.BlockSpec(a.shape, lambda bi, si, pt: (0,) * a.ndim)
    out = pl.BlockSpec((1, n_sub, KV_WIDTH), lambda bi, si, pt: (bi, 0, 0))
    return pl.pallas_call(
        _compress_pages_kernel,
        out_shape=[jax.ShapeDtypeStruct((b, n_sub, KV_WIDTH), jnp.bfloat16)] * 2,
        grid_spec=pltpu.PrefetchScalarGridSpec(
            num_scalar_prefetch=1,
            grid=(b, n_pages // n_pg),
            in_specs=[page(j) for j in range(n_pg)] * 2 + [full(a) for a in (*wk, *wv)],
            out_specs=[out, out],
            scratch_shapes=[pltpu.VMEM((2, CMP_RATIO, n_sub, SUB_WIDTH), jnp.bfloat16)]),
        compiler_params=pltpu.CompilerParams(dimension_semantics=("arbitrary", "arbitrary"),
                                             vmem_limit_bytes=V7X_VMEM_LIMIT_BYTES),
        name="compress_pages",
    )(page_table, *([view(cache_k)] * n_pg), *([view(cache_v)] * n_pg), *wk, *wv)


N_SEL_DEC = PAST_LEN // L_SEL + 1
N_SEL_DEC_PAD = 256
CUR_BLOCK = PAST_LEN // L_SEL


def _nsa_decode_cmp_kernel(q_ref, kc_ref, vc_ref, bias_ref, oc_ref, imp_ref):
    f32, bf16 = jnp.float32, jnp.bfloat16
    n_c = kc_ref.shape[1]
    s = _dot_nt(q_ref[0], kc_ref[0]) + bias_ref[...]
    mx = jnp.max(s, axis=-1, keepdims=True)
    mx = jnp.where(mx < 0.1 * NEG_BIG, 0.0, mx)
    p = jnp.exp(s - mx)
    p = p / jnp.maximum(jnp.sum(p, axis=-1, keepdims=True), 1e-30)
    pb = p.astype(bf16)
    oc_ref[0] = jnp.dot(pb, vc_ref[0], preferred_element_type=f32)
    ov_c = lax.broadcasted_iota(jnp.int32, (n_c, N_SEL_DEC_PAD), 0)
    ov_s = lax.broadcasted_iota(jnp.int32, (n_c, N_SEL_DEC_PAD), 1)
    overlap = jnp.where((ov_c >= 4 * ov_s - 1) & (ov_c <= 4 * ov_s + 3), 1.0, 0.0).astype(bf16)
    imp_h = jnp.dot(pb, overlap, preferred_element_type=f32)
    head = lax.broadcasted_iota(jnp.int32, imp_h.shape, 0)
    imp_ref[0] = jnp.concatenate(
        [jnp.sum(jnp.where((head >= g * GROUP) & (head < (g + 1) * GROUP), imp_h, 0.0), axis=0, keepdims=True)
         for g in range(N_KV)], axis=1)


def _nsa_decode_topk_kernel(imp_ref, idx_ref, val_ref):
    f32 = jnp.float32
    nb = imp_ref.shape[0]
    s_io = lax.broadcasted_iota(jnp.int32, (N_SEL_DEC_PAD, nb), 0)
    visible = s_io < N_SEL_DEC
    forced = (s_io == 0) | (s_io == CUR_BLOCK) | (s_io == CUR_BLOCK - 1)
    tri_r = lax.broadcasted_iota(jnp.int32, (N_SEL_DEC_PAD, N_SEL_DEC_PAD), 0)
    tri_c = lax.broadcasted_iota(jnp.int32, (N_SEL_DEC_PAD, N_SEL_DEC_PAD), 1)
    before = jnp.where(tri_c < tri_r, 1.0, 0.0).astype(jnp.bfloat16)
    for g in range(N_KV):
        x = imp_ref[:, g * N_SEL_DEC_PAD:(g + 1) * N_SEL_DEC_PAD]
        xt = jnp.concatenate([x[:, i * LANES:(i + 1) * LANES].T for i in range(N_SEL_DEC_PAD // LANES)], axis=0)
        val = jnp.where(visible, xt + jnp.where(forced, FORCE_SCORE, 0.0), -jnp.inf)
        val_ref[...] = val

        def rank_body(s, rank):
            other = val_ref[pl.ds(s, 1), :]
            beats = (other > val) | ((other == val) & (s_io > s))
            return rank + jnp.where(beats, 1, 0)

        rank = lax.fori_loop(0, N_SEL_DEC, rank_body, jnp.zeros(val.shape, jnp.int32))
        sel = (rank < K_SEL) & visible
        n_before = jnp.dot(before, jnp.where(sel, 1.0, 0.0).astype(jnp.bfloat16), preferred_element_type=f32)
        for k in range(K_SEL):
            hit = sel & (n_before == float(k))
            idx_ref[g, pl.ds(k, 1), :] = jnp.sum(jnp.where(hit, s_io.astype(f32), 0.0), axis=0,
                                                 keepdims=True).astype(jnp.int32)


def _nsa_decode_sel_kernel(pt_ref, ix_ref, *refs):
    f32, bf16 = jnp.float32, jnp.bfloat16
    n_blk = N_KV * K_SEL
    k_blocks, v_blocks = refs[:n_blk], refs[n_blk:2 * n_blk]
    (q_ref, oc_ref, gl_ref, ksn_ref, vsn_ref, kwn_ref, vwn_ref, kwin_ref, vwin_ref, near_ref, b0_ref, bw_ref,
     o_ref, kwo_ref, vwo_ref) = refs[2 * n_blk:]
    b = pl.program_id(0)
    q = q_ref[0]
    qf = q.astype(f32)
    round_bf = lambda a: a.astype(bf16).astype(f32)
    group1 = lax.broadcasted_iota(jnp.int32, (N_HEADS, LANES), 0) >= GROUP
    left = lax.broadcasted_iota(jnp.int32, (N_HEADS, LANES), 1) < L_SEL
    neg_tile = jnp.full((N_HEADS, LANES), NEG_BIG, f32)

    def block_bias(s):
        return jnp.where(s == CUR_BLOCK, neg_tile,
                         jnp.where(s == CUR_BLOCK - 1, near_ref[1],
                                   jnp.where(s == CUR_BLOCK - 2, near_ref[0], near_ref[2])))

    scores, values, has_new = [], [], []
    for g in range(N_KV):
        ids = [ix_ref[b, g * K_SEL + k] for k in range(K_SEL)]
        kcat = jnp.concatenate([k_blocks[g * K_SEL + k][0] for k in range(K_SEL)], axis=0).astype(bf16)
        values.append(jnp.concatenate([v_blocks[g * K_SEL + k][0] for k in range(K_SEL)], axis=0).astype(bf16))
        bias = jnp.concatenate([jnp.where(left, block_bias(ids[k]), block_bias(ids[k + 1]))
                                for k in range(0, K_SEL, 2)], axis=1)
        scores.append(_dot_nt(q, kcat) + bias)
        flag = ids[0] == CUR_BLOCK
        for k in range(1, K_SEL):
            flag = flag | (ids[k] == CUR_BLOCK)
        has_new.append(jnp.where(flag, 0.0, NEG_BIG))
    s = jnp.where(group1[:, :1], scores[1], scores[0])
    s_new = jnp.sum(qf * round_bf(ksn_ref[0]), axis=-1, keepdims=True) + b0_ref[...]
    s_new = s_new + jnp.where(group1, has_new[1], has_new[0])
    m = jnp.maximum(jnp.max(s, axis=-1, keepdims=True), s_new)
    p = jnp.exp(s - m[:, :1])
    p_new = jnp.exp(s_new - m)
    l = jnp.sum(p, axis=-1, keepdims=True) + p_new
    pb = p.astype(bf16)
    o_sel = jnp.where(group1, jnp.dot(pb, values[1], preferred_element_type=f32),
                      jnp.dot(pb, values[0], preferred_element_type=f32))
    o_sel = (o_sel + round_bf(p_new) * round_bf(vsn_ref[0])) / l

    kwin, vwin = kwin_ref[0], vwin_ref[0]
    s = _dot_nt(q, kwin.astype(bf16)) + bw_ref[...]
    s_new = jnp.sum(qf * round_bf(kwn_ref[0]), axis=-1, keepdims=True) + b0_ref[...]
    m = jnp.maximum(jnp.max(s, axis=-1, keepdims=True), s_new)
    p = jnp.exp(s - m[:, :1])
    p_new = jnp.exp(s_new - m)
    l = jnp.sum(p, axis=-1, keepdims=True) + p_new
    o_win = jnp.dot(p.astype(bf16), vwin.astype(bf16), preferred_element_type=f32)
    o_win = (o_win + round_bf(p_new) * round_bf(vwn_ref[0])) / l

    gates = jax.nn.sigmoid(gl_ref[0])
    o_ref[0] = (oc_ref[0] * gates[0] + o_sel * gates[1] + o_win * gates[2]).astype(bf16)

    wb = kwin.shape[0]
    last = lax.broadcasted_iota(jnp.int32, kwin.shape, 0) == wb - 1
    kwo_ref[0] = jnp.where(last, kwn_ref[0], pltpu.roll(kwin, wb - 1, 0))
    vwo_ref[0] = jnp.where(last, vwn_ref[0], pltpu.roll(vwin, wb - 1, 0))


def _nsa_decode(qpad, small, kc, vc, ks_new, vs_new, kw_new, vw_new, cache_k_sel, cache_v_sel,
                cache_k_win, cache_v_win, page_table, rel_bias):
    f32 = jnp.float32
    b = qpad.shape[0]
    n_c = kc.shape[1]
    n_pool = cache_k_sel.shape[0]
    wb = cache_k_win.shape[1]
    tab = rel_bias.astype(f32)[_bucket_table()].T
    dist_c = PAST_LEN - (L_CMP - 1) - D_CMP * np.arange(n_c)
    bias_c = jnp.where(jnp.asarray(dist_c >= 0)[None, :], tab[:, np.clip(dist_c, 0, MAX_DIST)], NEG_BIG)
    j = np.arange(LANES) % L_SEL
    near = jnp.stack([tab[:, 2 * L_SEL - j], tab[:, L_SEL - j],
                      jnp.broadcast_to(tab[:, MAX_DIST:], (N_HEADS, LANES))])
    bias0 = jnp.broadcast_to(tab[:, :1], (N_HEADS, LANES))
    bias_w = tab[:, np.minimum(wb - np.arange(wb), MAX_DIST)]
    q3 = qpad.reshape(b, N_HEADS, LANES)

    per_seq = lambda *shape: pl.BlockSpec((1,) + shape, lambda bi, *_: (bi,) + (0,) * len(shape))
    full = lambda a: pl.BlockSpec(a.shape, lambda bi, *_: (0,) * a.ndim)
    oc, imp = pl.pallas_call(
        _nsa_decode_cmp_kernel,
        out_shape=[jax.ShapeDtypeStruct((b, N_HEADS, LANES), f32),
                   jax.ShapeDtypeStruct((b, 1, N_KV * N_SEL_DEC_PAD), f32)],
        grid=(b,),
        in_specs=[per_seq(N_HEADS, LANES), per_seq(n_c, KV_WIDTH), per_seq(n_c, KV_WIDTH), full(bias_c)],
        out_specs=[per_seq(N_HEADS, LANES), per_seq(1, N_KV * N_SEL_DEC_PAD)],
        compiler_params=pltpu.CompilerParams(dimension_semantics=("arbitrary",)),
        name="nsa_decode_cmp",
    )(q3, kc, vc, bias_c)

    idx = pl.pallas_call(
        _nsa_decode_topk_kernel,
        out_shape=jax.ShapeDtypeStruct((N_KV, K_SEL, b), jnp.int32),
        scratch_shapes=[pltpu.VMEM((N_SEL_DEC_PAD, b), f32)],
        name="nsa_decode_topk",
    )(imp.reshape(b, N_KV * N_SEL_DEC_PAD))
    idx = idx.reshape(N_KV * K_SEL, b).T

    halves = PAGE_SIZE // L_SEL

    def sel_block(i):
        def index_map(bi, pt, ix):
            s = jnp.minimum(ix[bi, i], CUR_BLOCK - 1)
            return (pt[bi, s // halves] * halves + s % halves, 0, 0)
        return pl.BlockSpec((1, L_SEL, KV_WIDTH), index_map)

    half_pages = lambda c: c.reshape(n_pool * halves, L_SEL, KV_WIDTH)
    glog = jnp.broadcast_to(small[:, :3 * N_HEADS].reshape(b, N_HEADS, 3).transpose(0, 2, 1)[..., None],
                            (b, 3, N_HEADS, LANES))
    new_row = lambda a: a.reshape(b, 1, KV_WIDTH)
    n_blk = N_KV * K_SEL
    o_pad, k_win, v_win = pl.pallas_call(
        _nsa_decode_sel_kernel,
        out_shape=[jax.ShapeDtypeStruct((b, N_HEADS, LANES), jnp.bfloat16),
                   jax.ShapeDtypeStruct((b, wb, KV_WIDTH), f32), jax.ShapeDtypeStruct((b, wb, KV_WIDTH), f32)],
        grid_spec=pltpu.PrefetchScalarGridSpec(
            num_scalar_prefetch=2,
            grid=(b,),
            in_specs=[sel_block(i) for i in range(n_blk)] * 2
                     + [per_seq(N_HEADS, LANES), per_seq(N_HEADS, LANES), per_seq(3, N_HEADS, LANES)]
                     + [per_seq(1, KV_WIDTH)] * 4 + [per_seq(wb, KV_WIDTH)] * 2
                     + [full(near), full(bias0), full(bias_w)],
            out_specs=[per_seq(N_HEADS, LANES), per_seq(wb, KV_WIDTH), per_seq(wb, KV_WIDTH)]),
        compiler_params=pltpu.CompilerParams(dimension_semantics=("arbitrary",)),
        name="nsa_decode_sel",
    )(page_table, idx, *([half_pages(cache_k_sel)] * n_blk), *([half_pages(cache_v_sel)] * n_blk),
      q3, oc, glog, new_row(ks_new), new_row(vs_new), new_row(kw_new), new_row(vw_new),
      cache_k_win.reshape(b, wb, KV_WIDTH), cache_v_win.reshape(b, wb, KV_WIDTH), near, bias0, bias_w)
    return o_pad.reshape(b, QPAD_WIDTH), k_win, v_win


MLSTM_CHUNK = 128
MLSTM_DEC_TILE = 8
CONV_TAIL = 8


def _mlstm_norm_gate(h, zo, norm_g):
    return h * lax.rsqrt(jnp.mean(h * h, axis=-1, keepdims=True) + EPS) * norm_g * jax.nn.sigmoid(zo)


def _mlstm_prompt_kernel(zqk_ref, zv_ref, zo_ref, sm_ref, cw_ref, cb_ref, wq_ref, wk_ref, gb_ref, ng_ref,
                         o_ref, co_ref, no_ref, mo_ref, xbuf_ref, c_ref, n_ref, m_ref):
    f32, bf16 = jnp.float32, jnp.bfloat16
    L = MLSTM_CHUNK
    c = pl.program_id(1)

    @pl.when(c == 0)
    def _():
        xbuf_ref[:CONV_TAIL] = jnp.zeros((CONV_TAIL, M_WIDTH), f32)
        c_ref[...] = jnp.zeros(c_ref.shape, f32)
        n_ref[...] = jnp.zeros(n_ref.shape, f32)
        m_ref[...] = jnp.zeros(m_ref.shape, f32)

    x = zqk_ref[...]
    xbuf_ref[CONV_TAIL:] = x
    conv = cb_ref[...]
    for j in range(CONV_W):
        conv = conv + xbuf_ref[pl.ds(CONV_TAIL - (CONV_W - 1) + j, L), :] * cw_ref[j:j + 1, :]
    xbuf_ref[:CONV_TAIL] = x[L - CONV_TAIL:]
    a = jax.nn.silu(conv).astype(bf16)

    t_io = lax.broadcasted_iota(jnp.int32, (L, L), 0)
    s_io = lax.broadcasted_iota(jnp.int32, (L, L), 1)
    causal = t_io >= s_io
    sm = sm_ref[...]
    for h in range(M_HEADS):
        hs = slice(h * M_DH, (h + 1) * M_DH)
        q = jnp.dot(a[:, hs], wq_ref[h], preferred_element_type=f32)
        k = jnp.dot(a[:, hs], wk_ref[h], preferred_element_type=f32) * (M_DH ** -0.5)
        v = zv_ref[:, hs]
        qb, kb, vb = q.astype(bf16), k.astype(bf16), v.astype(bf16)
        col = 3 * N_HEADS + h
        ig = jnp.broadcast_to(sm[:, col:col + 1], (L, L)) + gb_ref[0:1, hs]
        lf = jax.nn.log_sigmoid(jnp.broadcast_to(sm[:, col + M_HEADS:col + M_HEADS + 1], (L, L)) + gb_ref[1:2, hs])
        bcum = lf
        sh = 1
        while sh < L:
            bcum = bcum + jnp.where(t_io >= sh, pltpu.roll(bcum, sh, 0), 0.0)
            sh *= 2
        m_old = m_ref[h:h + 1, :]
        c_old = c_ref[h]
        n_old = n_ref[h:h + 1, :]
        dmat = jnp.where(causal, bcum - bcum.T + ig.T, -jnp.inf)
        inter = bcum + m_old
        m_t = jnp.maximum(jnp.max(dmat, axis=-1, keepdims=True), inter)
        sc = _dot_nt(qb, kb) * jnp.exp(dmat - m_t)
        decay = jnp.exp(inter - m_t)
        num = decay * _dot_nt(qb, c_old.astype(bf16)) + jnp.dot(sc.astype(bf16), vb, preferred_element_type=f32)
        den = decay * jnp.sum(q * n_old, axis=-1, keepdims=True) + jnp.sum(sc, axis=-1, keepdims=True)
        hh = num / jnp.maximum(jnp.abs(den), jnp.exp(-m_t))
        o_ref[:, hs] = _mlstm_norm_gate(hh, zo_ref[:, hs], ng_ref[:, hs])

        b_last = bcum[L - 1:L, :]
        w_log = b_last - bcum + ig
        m_new = jnp.maximum(b_last + m_old, jnp.max(w_log, axis=0, keepdims=True))
        w = jnp.exp(w_log - m_new)
        carry = jnp.exp(b_last + m_old - m_new)
        c_ref[h] = carry * c_old + jnp.dot((w * v).T.astype(bf16), kb, preferred_element_type=f32)
        n_ref[h:h + 1, :] = carry * n_old + jnp.sum(w * k, axis=0, keepdims=True)
        m_ref[h:h + 1, :] = m_new

    @pl.when(c == pl.num_programs(1) - 1)
    def _():
        co_ref[0] = c_ref[...]
        no_ref[0] = n_ref[...]
        mo_ref[0] = m_ref[...]


def _mlstm_weights(conv_w, conv_b, wq, wk, gate_bias, norm_g):
    gb = jnp.repeat(gate_bias.astype(jnp.float32), M_DH, axis=1)
    return (conv_w, conv_b.reshape(1, M_WIDTH), wq.astype(jnp.bfloat16), wk.astype(jnp.bfloat16), gb,
            norm_g.reshape(1, M_WIDTH))


def _mlstm_prompt(zmqk, zmv, zmo, small, weights, b, t):
    L = MLSTM_CHUNK
    n_ch = t // L
    rows = lambda width: pl.BlockSpec((L, width), lambda bi, ci: (bi * n_ch + ci, 0))
    full = lambda a: pl.BlockSpec(a.shape, lambda bi, ci: (0,) * a.ndim)
    state = lambda *shape: pl.BlockSpec((1,) + shape, lambda bi, ci: (bi,) + (0,) * len(shape))
    f32 = jnp.float32
    return pl.pallas_call(
        _mlstm_prompt_kernel,
        out_shape=[jax.ShapeDtypeStruct((b * t, M_WIDTH), f32), jax.ShapeDtypeStruct((b, M_HEADS, M_DH, M_DH), f32),
                   jax.ShapeDtypeStruct((b, 8, M_DH), f32), jax.ShapeDtypeStruct((b, 8, M_DH), f32)],
        grid=(b, n_ch),
        in_specs=[rows(M_WIDTH), rows(M_WIDTH), rows(M_WIDTH), rows(LANES)] + [full(a) for a in weights],
        out_specs=[rows(M_WIDTH), state(M_HEADS, M_DH, M_DH), state(8, M_DH), state(8, M_DH)],
        scratch_shapes=[pltpu.VMEM((CONV_TAIL + L, M_WIDTH), f32), pltpu.VMEM((M_HEADS, M_DH, M_DH), f32),
                        pltpu.VMEM((8, M_DH), f32), pltpu.VMEM((8, M_DH), f32)],
        compiler_params=pltpu.CompilerParams(dimension_semantics=("arbitrary", "arbitrary")),
        name="mlstm_prompt",
    )(zmqk, zmv, zmo, small, *weights)


def _mlstm_decode_kernel(zqk_ref, conv_ref, zv_ref, zo_ref, sm_ref, c_ref, n_ref, m_ref,
                         cw_ref, cb_ref, wq_ref, wk_ref, gb_ref, ng_ref, o_ref, co_ref, no_ref, mo_ref):
    f32, bf16 = jnp.float32, jnp.bfloat16
    nt = MLSTM_DEC_TILE
    conv = cb_ref[...] + zqk_ref[...] * cw_ref[CONV_W - 1:CONV_W, :]
    for j in range(CONV_W - 1):
        conv = conv + conv_ref[j] * cw_ref[j:j + 1, :]
    a = jax.nn.silu(conv).astype(bf16)
    sm = sm_ref[...]
    lane = lax.broadcasted_iota(jnp.int32, (M_DH, M_DH), 1)
    pad_rows = jnp.zeros((M_DH - nt, M_DH), f32)
    for h in range(M_HEADS):
        hs = slice(h * M_DH, (h + 1) * M_DH)
        q = jnp.dot(a[:, hs], wq_ref[h], preferred_element_type=f32)
        k = jnp.dot(a[:, hs], wk_ref[h], preferred_element_type=f32) * (M_DH ** -0.5)
        v = zv_ref[:, hs]
        col = 3 * N_HEADS + h
        ig = jnp.broadcast_to(sm[:, col:col + 1], (nt, M_DH)) + gb_ref[0:1, hs]
        lf = jax.nn.log_sigmoid(jnp.broadcast_to(sm[:, col + M_HEADS:col + M_HEADS + 1], (nt, M_DH)) + gb_ref[1:2, hs])
        m_old = m_ref[:, hs]
        n_old = n_ref[:, hs]
        m_new = jnp.maximum(lf + m_old, ig)
        decay = jnp.exp(lf + m_old - m_new)
        w = jnp.exp(ig - m_new)
        sc = jnp.sum(q * k, axis=-1, keepdims=True) * w
        den = decay * jnp.sum(n_old * q, axis=-1, keepdims=True) + sc
        wv_t = jnp.concatenate([w * v, pad_rows], axis=0).T
        cq_t = jnp.zeros((M_DH, M_DH), f32)
        for i in range(nt):
            c_old = c_ref[i, h]
            cq = jnp.sum(c_old * q[i:i + 1, :], axis=-1, keepdims=True)
            cq_t = jnp.where(lane == i, cq, cq_t)
            co_ref[i, h] = decay[i:i + 1, :] * c_old + wv_t[:, i:i + 1] * k[i:i + 1, :]
        num = decay * cq_t.T[:nt] + sc * v
        hh = num / jnp.maximum(jnp.abs(den), jnp.exp(-m_new))
        o_ref[:, hs] = _mlstm_norm_gate(hh, zo_ref[:, hs], ng_ref[:, hs])
        no_ref[:, hs] = decay * n_old + w * k
        mo_ref[:, hs] = m_new


def _mlstm_decode(zmqk, zmv, zmo, small, state_c, state_n, state_m, state_conv, weights):
    b = zmqk.shape[0]
    nt = MLSTM_DEC_TILE
    f32 = jnp.float32
    rows = lambda width: pl.BlockSpec((nt, width), lambda i: (i, 0))
    full = lambda a: pl.BlockSpec(a.shape, lambda i: (0,) * a.ndim)
    cspec = pl.BlockSpec((nt, M_HEADS, M_DH, M_DH), lambda i: (i, 0, 0, 0))
    conv_t = state_conv.transpose(1, 0, 2)
    m_rep = jnp.repeat(state_m, M_DH, axis=1)
    return pl.pallas_call(
        _mlstm_decode_kernel,
        out_shape=[jax.ShapeDtypeStruct((b, M_WIDTH), f32), jax.ShapeDtypeStruct(state_c.shape, f32),
                   jax.ShapeDtypeStruct((b, M_WIDTH), f32), jax.ShapeDtypeStruct((b, M_WIDTH), f32)],
        grid=(b // nt,),
        in_specs=[rows(M_WIDTH), pl.BlockSpec((CONV_W - 1, nt, M_WIDTH), lambda i: (0, i, 0)), rows(M_WIDTH),
                  rows(M_WIDTH), rows(LANES), cspec, rows(M_WIDTH), rows(M_WIDTH)] + [full(a) for a in weights],
        out_specs=[rows(M_WIDTH), cspec, rows(M_WIDTH), rows(M_WIDTH)],
        compiler_params=pltpu.CompilerParams(dimension_semantics=("arbitrary",)),
        name="mlstm_decode",
    )(zmqk, conv_t, zmv, zmo, small, state_c, state_n.reshape(b, M_WIDTH), m_rep, *weights)


def _merge_kernel(x_ref, on_ref, om_ref, zm_ref, wn_ref, wm_ref, wo_ref, o_ref):
    zm = zm_ref[...]
    g_a = jax.nn.sigmoid(zm[:, :D_MODEL])
    g_b = jax.nn.sigmoid(zm[:, D_MODEL:])
    ya = jnp.dot(on_ref[...], wn_ref[...], preferred_element_type=jnp.float32)
    yb = jnp.dot(om_ref[...].astype(jnp.bfloat16), wm_ref[...], preferred_element_type=jnp.float32)
    y = (g_a * ya + g_b * yb).astype(jnp.bfloat16)
    o_ref[...] = x_ref[...] + jnp.dot(y, wo_ref[...], preferred_element_type=jnp.float32)


def _merge(x, o_nsa, o_mlstm, zmerge, w_proj_nsa, w_proj_mlstm, w_out):
    n = x.shape[0]
    tm = min(PROJ_TOKEN_TILE, n)
    row = lambda width: pl.BlockSpec((tm, width), lambda i: (i, 0))
    full = lambda a: pl.BlockSpec(a.shape, lambda i: (0, 0))
    return pl.pallas_call(
        _merge_kernel,
        out_shape=jax.ShapeDtypeStruct((n, D_MODEL), jnp.float32),
        grid=(n // tm,),
        in_specs=[row(D_MODEL), row(o_nsa.shape[1]), row(M_WIDTH), row(2 * D_MODEL),
                  full(w_proj_nsa), full(w_proj_mlstm), full(w_out)],
        out_specs=row(D_MODEL),
        compiler_params=pltpu.CompilerParams(dimension_semantics=("arbitrary",),
                                             vmem_limit_bytes=V7X_VMEM_LIMIT_BYTES),
        name="merge_out_proj",
    )(x, o_nsa, o_mlstm, zmerge, w_proj_nsa, w_proj_mlstm, w_out)


def _kv_rows(a, b, t):
    return a.reshape(b, t, N_KV, HEAD_DIM)


def _mix_prompt(proj, b, t, cmp_k, cmp_v, mlstm_w, rel_bias):
    qpad, kvb, kc_rows, vc_rows, ks_rows, vs_rows, kw_rows, vw_rows, zmqk, zmv, zmo, zmerge, small = proj
    kc, vc = _compress_rows(kc_rows, vc_rows, b, t, cmp_k, cmp_v)
    o_nsa = _nsa_prompt(qpad, small, kvb, kc, vc.transpose(0, 2, 1), rel_bias, b, t)
    o_mlstm, c_f, n_f, m_f = _mlstm_prompt(zmqk, zmv, zmo, small, mlstm_w, b, t)
    n_keep = min(WINDOW, t)
    states = (_kv_rows(kc_rows, b, t), _kv_rows(vc_rows, b, t), _kv_rows(ks_rows, b, t), _kv_rows(vs_rows, b, t),
              _kv_rows(kw_rows, b, t)[:, t - n_keep:], _kv_rows(vw_rows, b, t)[:, t - n_keep:],
              c_f, n_f[:, :M_HEADS], m_f[:, :M_HEADS, 0], zmqk.reshape(b, t, M_WIDTH)[:, t - (CONV_W - 1):])
    return o_nsa, o_mlstm, zmerge, states


def _mix_decode(proj, b, caches, mlstm_state, page_table, cmp_k, cmp_v, mlstm_w, rel_bias):
    cache_k_cmp, cache_v_cmp, cache_k_sel, cache_v_sel, cache_k_win, cache_v_win = caches
    state_c, state_n, state_m, state_conv = mlstm_state
    qpad, _, kc_new, vc_new, ks_new, vs_new, kw_new, vw_new, zmqk, zmv, zmo, zmerge, small = proj
    kc, vc = _compress_pages(cache_k_cmp, cache_v_cmp, page_table, cmp_k, cmp_v)
    o_nsa, k_win, v_win = _nsa_decode(qpad, small, kc, vc, ks_new, vs_new, kw_new, vw_new, cache_k_sel, cache_v_sel,
                                      cache_k_win, cache_v_win, page_table, rel_bias)
    o_mlstm, c_n, n_n, m_rep = _mlstm_decode(zmqk, zmv, zmo, small, state_c, state_n, state_m, state_conv, mlstm_w)
    wb = cache_k_win.shape[1]
    conv_new = jnp.concatenate([state_conv[:, 1:], zmqk[:, None, :]], axis=1)
    states = (_kv_rows(kc_new, b, 1), _kv_rows(vc_new, b, 1), _kv_rows(ks_new, b, 1), _kv_rows(vs_new, b, 1),
              _kv_rows(k_win, b, wb), _kv_rows(v_win, b, wb),
              c_n, n_n.reshape(b, M_HEADS, M_DH), m_rep[:, ::M_DH], conv_new)
    return o_nsa, o_mlstm, zmerge, states


def kernel(x_prompt, x_sample, cache_k_cmp, cache_v_cmp, cache_k_sel, cache_v_sel, cache_k_win, cache_v_win, state_mlstm_C, state_mlstm_n, state_mlstm_m, state_mlstm_conv, page_table, norm_ffn1, ffn1_w_in, ffn1_w_out, norm_mix, w_mix_in, cmp_pos_k, cmp_pos_v, cmp_phi_k1, cmp_phi_k2, cmp_phi_v1, cmp_phi_v2, rel_bias, mlstm_conv_w, mlstm_conv_b, mlstm_wq, mlstm_wk, mlstm_gate_bias, mlstm_norm, w_proj_nsa, w_proj_mlstm, w_out, norm_ffn2, ffn2_w_in, ffn2_w_out, norm_final):
    assert x_sample.shape[1] == DEC_SEQ == 1
    bf = lambda w: w.astype(jnp.bfloat16)
    w1i, w1o, w2i, w2o = bf(ffn1_w_in), bf(ffn1_w_out), bf(ffn2_w_in), bf(ffn2_w_out)
    w_mix = _regroup_mix_weight(w_mix_in)
    w_nsa_out, wm, wo = bf(_pad_nsa_out_weight(w_proj_nsa)), bf(w_proj_mlstm), bf(w_out)
    cmp_k = _compress_weights(cmp_pos_k, cmp_phi_k1, cmp_phi_k2)
    cmp_v = _compress_weights(cmp_pos_v, cmp_phi_v1, cmp_phi_v2)
    mlstm_w = _mlstm_weights(mlstm_conv_w, mlstm_conv_b, mlstm_wq, mlstm_wk, mlstm_gate_bias, mlstm_norm)

    def layer(x3, mix_fn):
        b, t, _ = x3.shape
        x = x3.reshape(b * t, D_MODEL)
        x1 = _ffn(x, norm_ffn1, w1i, w1o, norm_final, final_norm=False)
        proj = _project(x1, norm_mix, w_mix)
        o_nsa, o_mlstm, zmerge, states = mix_fn(proj, b, t)
        x2 = _merge(x1, o_nsa, o_mlstm, zmerge, w_nsa_out, wm, wo)
        y = _ffn(x2, norm_ffn2, w2i, w2o, norm_final, final_norm=True)
        return y.reshape(b, t, D_MODEL), states

    y_prompt, st_p = layer(x_prompt, lambda proj, b, t: _mix_prompt(proj, b, t, cmp_k, cmp_v, mlstm_w, rel_bias))
    y_sample, st_s = layer(x_sample, lambda proj, b, t: _mix_decode(
        proj, b, (cache_k_cmp, cache_v_cmp, cache_k_sel, cache_v_sel, cache_k_win, cache_v_win),
        (state_mlstm_C, state_mlstm_n, state_mlstm_m, state_mlstm_conv), page_table, cmp_k, cmp_v, mlstm_w, rel_bias))
    k_cmp_p, v_cmp_p, k_sel_p, v_sel_p, k_win_p, v_win_p, C_p, n_p, m_p, conv_p = st_p
    k_cmp_s, v_cmp_s, k_sel_s, v_sel_s, k_win_s, v_win_s, C_s, n_s, m_s, conv_s = st_s
    return (y_prompt, y_sample, k_cmp_p, k_cmp_s, v_cmp_p, v_cmp_s, k_sel_p, k_sel_s, v_sel_p, v_sel_s,
            k_win_p, k_win_s, v_win_p, v_win_s, C_p, C_s, n_p, n_s, m_p, m_s, conv_p, conv_s)
```

```python
import math
from functools import partial

import numpy as np
import jax
import jax.numpy as jnp
from jax import lax
from jax.experimental import pallas as pl
from jax.experimental.pallas import tpu as pltpu

D_MODEL = 1024
SEQ = 8192
DEC_SEQ = 1
PAST_LEN = 8192
PAGE_SIZE = 128
N_HEADS = 8
N_KV = 2
GROUP = N_HEADS // N_KV
HEAD_DIM = 64
NSA_WIDTH = N_HEADS * HEAD_DIM
KV_WIDTH = N_KV * HEAD_DIM
L_CMP = 32
D_CMP = 16
CMP_RATIO = L_CMP // D_CMP
L_SEL = 64
K_SEL = 16
WINDOW = 512
Q_BLOCK = 128
FORCE_SCORE = 1000.0
N_BUCKETS = 32
MAX_DIST = 128
M_HEADS = 4
M_DH = 128
M_WIDTH = M_HEADS * M_DH
CONV_W = 4
M_CHUNK = 64
D_FF = 2816
EPS = 1e-6

V7X_VMEM_LIMIT_BYTES = 56 * 1024 * 1024
LANES = 128

QPAD_WIDTH = N_HEADS * LANES
N_SEL_PAD = 128
FAR_TILES = 2
NEG_BIG = -1e30
SEL_OFF = -32768.0

FFN_TOKEN_TILE = 512
FFN_FF_TILE = 1408
PROJ_TOKEN_TILE = 512

_MIX_SIZES = (NSA_WIDTH, KV_WIDTH, KV_WIDTH, KV_WIDTH, KV_WIDTH, KV_WIDTH, KV_WIDTH, 3 * N_HEADS,
              M_WIDTH, M_WIDTH, 2 * M_HEADS, M_WIDTH, 2 * D_MODEL)
_MIX_OFFS = np.concatenate([[0], np.cumsum(_MIX_SIZES)]).tolist()
_PROJ_SEGS = (("kv", 6 * KV_WIDTH), ("mqk", M_WIDTH), ("mv", M_WIDTH),
              ("mo", M_WIDTH), ("merge", 2 * D_MODEL), ("small", LANES))


def _rms(x, g):
    return x * lax.rsqrt(jnp.mean(x * x, axis=-1, keepdims=True) + EPS) * g


def _ffn_kernel(x_ref, g_ref, wg_ref, wu_ref, wo_ref, gf_ref, o_ref, h_ref, acc_ref, *, final_norm):
    j = pl.program_id(1)

    @pl.when(j == 0)
    def _():
        h_ref[...] = _rms(x_ref[...], g_ref[...]).astype(jnp.bfloat16)
        acc_ref[...] = jnp.zeros_like(acc_ref)

    h = h_ref[...]
    gate = jnp.dot(h, wg_ref[...], preferred_element_type=jnp.float32)
    up = jnp.dot(h, wu_ref[...], preferred_element_type=jnp.float32)
    a = (jax.nn.silu(gate) * up).astype(jnp.bfloat16)
    acc_ref[...] += jnp.dot(a, wo_ref[...], preferred_element_type=jnp.float32)

    @pl.when(j == pl.num_programs(1) - 1)
    def _():
        y = x_ref[...] + 0.5 * acc_ref[...]
        if final_norm:
            y = _rms(y, gf_ref[...])
        o_ref[...] = y


def _ffn(x, g, w_in, w_out, g_final, *, final_norm):
    n = x.shape[0]
    tm = min(FFN_TOKEN_TILE, n)
    nj = D_FF // FFN_FF_TILE
    return pl.pallas_call(
        partial(_ffn_kernel, final_norm=final_norm),
        out_shape=jax.ShapeDtypeStruct((n, D_MODEL), jnp.float32),
        grid=(n // tm, nj),
        in_specs=[
            pl.BlockSpec((tm, D_MODEL), lambda i, j: (i, 0)),
            pl.BlockSpec((1, D_MODEL), lambda i, j: (0, 0)),
            pl.BlockSpec((D_MODEL, FFN_FF_TILE), lambda i, j: (0, j)),
            pl.BlockSpec((D_MODEL, FFN_FF_TILE), lambda i, j: (0, j + D_FF // FFN_FF_TILE)),
            pl.BlockSpec((FFN_FF_TILE, D_MODEL), lambda i, j: (j, 0)),
            pl.BlockSpec((1, D_MODEL), lambda i, j: (0, 0)),
        ],
        out_specs=pl.BlockSpec((tm, D_MODEL), lambda i, j: (i, 0)),
        scratch_shapes=[pltpu.VMEM((tm, D_MODEL), jnp.bfloat16), pltpu.VMEM((tm, D_MODEL), jnp.float32)],
        compiler_params=pltpu.CompilerParams(dimension_semantics=("arbitrary", "arbitrary"),
                                             vmem_limit_bytes=V7X_VMEM_LIMIT_BYTES),
        name="ffn",
    )(x, g.reshape(1, D_MODEL), w_in, w_in, w_out, g_final.reshape(1, D_MODEL))


def _proj_kernel(x_ref, g_ref, w_ref, qpad_ref, kvb_ref, *o_refs):
    h = _rms(x_ref[...], g_ref[...]).astype(jnp.bfloat16)
    zq = jnp.dot(h, w_ref[:, :QPAD_WIDTH], preferred_element_type=jnp.float32)
    qpad_ref[...] = (zq * (HEAD_DIM ** -0.5)).astype(jnp.bfloat16)
    off = QPAD_WIDTH
    o_refs = list(o_refs)
    for name, width in _PROJ_SEGS:
        z = jnp.dot(h, w_ref[:, off:off + width], preferred_element_type=jnp.float32)
        if name == "kv":
            kvb_ref[...] = z.astype(jnp.bfloat16)
            for i in range(6):
                o_refs.pop(0)[...] = z[:, i * KV_WIDTH:(i + 1) * KV_WIDTH]
        else:
            o_refs.pop(0)[...] = z
        off += width


def _regroup_mix_weight(w_mix_in):
    seg = lambda i: w_mix_in[:, _MIX_OFFS[i]:_MIX_OFFS[i + 1]]
    wq = seg(0)
    zeros = jnp.zeros((D_MODEL, HEAD_DIM), w_mix_in.dtype)
    qpad = []
    for h in range(N_HEADS):
        wh = wq[:, h * HEAD_DIM:(h + 1) * HEAD_DIM]
        qpad += [wh, zeros] if h // GROUP == 0 else [zeros, wh]
    small = jnp.concatenate([seg(7), seg(10), jnp.zeros((D_MODEL, LANES - 3 * N_HEADS - 2 * M_HEADS), w_mix_in.dtype)], axis=1)
    cols = qpad + [seg(i) for i in range(1, 7)] + [seg(8), seg(9), seg(11), seg(12), small]
    return jnp.concatenate(cols, axis=1).astype(jnp.bfloat16)


def _project(x, g, w_regrouped):
    n = x.shape[0]
    tm = min(PROJ_TOKEN_TILE, n)
    wcols = w_regrouped.shape[1]
    row = lambda width: pl.BlockSpec((tm, width), lambda i: (i, 0))
    widths = []
    for name, width in _PROJ_SEGS:
        widths += [KV_WIDTH] * 6 if name == "kv" else [width]
    return pl.pallas_call(
        _proj_kernel,
        out_shape=[jax.ShapeDtypeStruct((n, QPAD_WIDTH), jnp.bfloat16),
                   jax.ShapeDtypeStruct((n, 6 * KV_WIDTH), jnp.bfloat16)]
                  + [jax.ShapeDtypeStruct((n, width), jnp.float32) for width in widths],
        grid=(n // tm,),
        in_specs=[
            row(D_MODEL),
            pl.BlockSpec((1, D_MODEL), lambda i: (0, 0)),
            pl.BlockSpec((D_MODEL, wcols), lambda i: (0, 0)),
        ],
        out_specs=[row(QPAD_WIDTH), row(6 * KV_WIDTH)] + [row(width) for width in widths],
        compiler_params=pltpu.CompilerParams(dimension_semantics=("arbitrary",),
                                             vmem_limit_bytes=V7X_VMEM_LIMIT_BYTES),
        name="mix_in_proj",
    )(x, g.reshape(1, D_MODEL), w_regrouped)


def _dot_nt(a, b):
    return lax.dot_general(a, b, (((1,), (1,)), ((), ())), preferred_element_type=jnp.float32)


def _bucket_table():
    d = np.arange(MAX_DIST + 1)
    max_exact = N_BUCKETS // 2
    nf = np.maximum(d, 1).astype(np.float64)
    large = max_exact + (np.log(nf / max_exact) / math.log(MAX_DIST / max_exact) * (N_BUCKETS - max_exact)).astype(np.int64)
    return np.where(d < max_exact, d, np.minimum(large, N_BUCKETS - 1)).astype(np.int32)


def _bias_tables(rel_bias, n_c):
    tab = rel_bias.astype(jnp.float32)[_bucket_table()].T
    i = np.arange(Q_BLOCK)[:, None]
    j = np.arange(Q_BLOCK)[None, :]
    far = tab[:, MAX_DIST][:, None, None]
    t0 = tab[:, np.clip(i - j, 0, MAX_DIST)] - far
    t1 = tab[:, np.minimum(Q_BLOCK + i - j, MAX_DIST)] - far
    tiles = jnp.stack([t0, t1]).reshape(2, N_HEADS * Q_BLOCK, Q_BLOCK)
    lo, hi = -2 * Q_BLOCK // D_CMP, Q_BLOCK // D_CMP
    dist = np.arange(Q_BLOCK)[None, :] - (L_CMP - 1) - D_CMP * np.arange(lo, hi)[:, None]
    band = jnp.where(jnp.asarray(dist >= 0), tab[:, np.clip(dist, 0, MAX_DIST)], NEG_BIG)
    bc = jnp.concatenate([jnp.broadcast_to(tab[:, MAX_DIST][:, None, None], (N_HEADS, n_c + lo, Q_BLOCK)), band,
                          jnp.full((N_HEADS, n_c - hi, Q_BLOCK), NEG_BIG, jnp.float32)], axis=1)
    bct = bc.transpose(1, 0, 2).reshape(2 * n_c, N_HEADS * Q_BLOCK)
    return tiles, bct


def _nsa_prompt_kernel(q_ref, gate_ref, kc_ref, vct_ref, ks_ref, vs_ref, kw_ref, vw_ref, tiles_ref, bct_ref,
                       o_ref, m_ref, l_ref, acc_ref, imp_ref, *, n_c):
    f32, bf16 = jnp.float32, jnp.bfloat16
    qb = pl.program_id(1)
    rows_all = N_HEADS * Q_BLOCK
    qi = lax.broadcasted_iota(jnp.int32, (rows_all, Q_BLOCK), 0) & (Q_BLOCK - 1)
    kj = lax.broadcasted_iota(jnp.int32, (rows_all, Q_BLOCK), 1)
    causal = qi >= kj
    window_edge = kj >= qi
    sig = jax.nn.sigmoid(gate_ref[...])

    s_io = lax.broadcasted_iota(jnp.int32, (N_SEL_PAD, N_KV * Q_BLOCK), 0)
    i_io = lax.broadcasted_iota(jnp.int32, (N_SEL_PAD, N_KV * Q_BLOCK), 1) & (Q_BLOCK - 1)
    qpos = qb * Q_BLOCK + i_io
    cur = 2 * qb + (i_io >= L_SEL).astype(jnp.int32)
    blk_valid = s_io * L_SEL <= qpos
    blk_forced = (s_io == 0) | (s_io == cur) | (s_io == cur - 1)
    ov_s = lax.broadcasted_iota(jnp.int32, (N_SEL_PAD, n_c), 0)
    ov_c = lax.broadcasted_iota(jnp.int32, (N_SEL_PAD, n_c), 1)
    overlap_t = jnp.where((ov_c >= 4 * ov_s - 1) & (ov_c <= 4 * ov_s + 3), 1.0, 0.0).astype(bf16)

    def flash_init():
        m_ref[...] = jnp.full(m_ref.shape, NEG_BIG, f32)
        l_ref[...] = jnp.zeros(l_ref.shape, f32)
        acc_ref[...] = jnp.zeros(acc_ref.shape, f32)

    def flash_step(s, v_tile):
        m_old = m_ref[...]
        m_new = jnp.maximum(m_old, jnp.max(s, axis=-1, keepdims=True))
        alpha = jnp.exp(m_old - m_new)
        p = jnp.exp(s - jnp.concatenate([m_new] * (s.shape[1] // LANES), axis=1))
        l_ref[...] = alpha * l_ref[...] + jnp.sum(p, axis=-1, keepdims=True)
        acc_ref[...] = alpha * acc_ref[...] + jnp.dot(p.astype(bf16), v_tile, preferred_element_type=f32)
        m_ref[...] = m_new

    def key_rows(kt, n_tiles=1):
        return pl.ds(pl.multiple_of(kt * Q_BLOCK, Q_BLOCK), n_tiles * Q_BLOCK)

    q_all = jnp.concatenate([q_ref[:, h * LANES:(h + 1) * LANES] for h in range(N_HEADS)], axis=0)

    st = _dot_nt(kc_ref[0], q_all)
    st = st + bct_ref[pl.ds(pl.multiple_of(n_c - 8 * qb, 8), n_c), :]
    mx = jnp.max(st, axis=0, keepdims=True)
    mx = jnp.where(mx < 0.1 * NEG_BIG, 0.0, mx)
    p = jnp.exp(st - mx)
    p = p / jnp.maximum(jnp.sum(p, axis=0, keepdims=True), 1e-30)
    pb = p.astype(bf16)
    oc_t = jnp.dot(vct_ref[0], pb, preferred_element_type=f32)

    imps = []
    for g in range(N_KV):
        imp_g = None
        for r in range(GROUP):
            h = GROUP * g + r
            part = jnp.dot(overlap_t, pb[:, h * Q_BLOCK:(h + 1) * Q_BLOCK], preferred_element_type=f32)
            imp_g = part if imp_g is None else imp_g + part
        imps.append(imp_g)
    imp = jnp.concatenate(imps, axis=1)
    imp = jnp.where(blk_valid, imp + jnp.where(blk_forced, FORCE_SCORE, 0.0), -jnp.inf)
    imp_ref[...] = imp

    def rank_body(s, rank):
        other = imp_ref[pl.ds(s, 1), :]
        beats = (other > imp) | ((other == imp) & (s_io > s))
        return rank + jnp.where(beats, 1, 0)

    rank = lax.fori_loop(0, 2 * qb + 2, rank_body, jnp.zeros(imp.shape, jnp.int32))
    sel_off = jnp.where(rank < K_SEL, 0.0, SEL_OFF)
    sel_rows = []
    for g in range(N_KV):
        sel_rows += [sel_off[:, g * Q_BLOCK:(g + 1) * Q_BLOCK].T.astype(bf16)] * GROUP
    lhs_sel = jnp.concatenate([jnp.concatenate(sel_rows, axis=0), q_all], axis=1)

    def sel_scores(kt, n_tiles=1):
        key_blk = jnp.right_shift(lax.broadcasted_iota(jnp.int32, (n_tiles * Q_BLOCK, N_SEL_PAD), 0),
                                  L_SEL.bit_length() - 1)
        blk_lane = lax.broadcasted_iota(jnp.int32, (n_tiles * Q_BLOCK, N_SEL_PAD), 1)
        onehot = jnp.where(blk_lane == 2 * kt + key_blk, 1.0, 0.0).astype(bf16)
        rhs = jnp.concatenate([onehot, ks_ref[0, key_rows(kt, n_tiles), :]], axis=1)
        return _dot_nt(lhs_sel, rhs)

    flash_init()

    n_far = jnp.maximum(qb - 1, 0)

    def far_body(i, carry):
        flash_step(sel_scores(FAR_TILES * i, FAR_TILES), vs_ref[0, key_rows(FAR_TILES * i, FAR_TILES), :])
        return carry

    lax.fori_loop(0, n_far // FAR_TILES, far_body, 0)

    @pl.when(n_far % FAR_TILES == 1)
    def _():
        flash_step(sel_scores(n_far - 1), vs_ref[0, key_rows(n_far - 1), :])

    @pl.when(qb >= 1)
    def _():
        flash_step(sel_scores(qb - 1) + tiles_ref[1], vs_ref[0, key_rows(qb - 1), :])

    flash_step(jnp.where(causal, sel_scores(qb) + tiles_ref[0], NEG_BIG), vs_ref[0, key_rows(qb), :])
    o_sel = acc_ref[...] / l_ref[...]

    flash_init()
    for dt in range(WINDOW // Q_BLOCK + 1):
        @pl.when(qb >= dt)
        def _():
            s = _dot_nt(q_all, kw_ref[0, key_rows(qb - dt), :])
            if dt < 2:
                s = s + tiles_ref[dt]
            if dt == 0:
                s = jnp.where(causal, s, NEG_BIG)
            if dt == WINDOW // Q_BLOCK:
                s = jnp.where(window_edge, s, NEG_BIG)
            flash_step(s, vw_ref[0, key_rows(qb - dt), :])
    o_win = acc_ref[...] / l_ref[...]

    for h in range(N_HEADS):
        rs = slice(h * Q_BLOCK, (h + 1) * Q_BLOCK)
        o = (oc_t[:, rs].T * sig[:, 3 * h:3 * h + 1] + o_sel[rs] * sig[:, 3 * h + 1:3 * h + 2]
             + o_win[rs] * sig[:, 3 * h + 2:3 * h + 3])
        o_ref[:, h * LANES:(h + 1) * LANES] = o.astype(bf16)


def _nsa_prompt(qpad, small, kvb, kcb, vct, rel_bias, b, t):
    n_c = t // D_CMP
    n_qb = t // Q_BLOCK
    tiles, bct = _bias_tables(rel_bias, n_c)
    kv3 = kvb.reshape(b, t, 6 * KV_WIDTH)
    rows = lambda width: pl.BlockSpec((Q_BLOCK, width), lambda bi, qi: (bi * n_qb + qi, 0))
    seq = lambda lane_block: pl.BlockSpec((1, t, KV_WIDTH), lambda bi, qi: (bi, 0, lane_block))
    full = lambda a: pl.BlockSpec(a.shape, lambda bi, qi: (0,) * a.ndim)
    rows_all = N_HEADS * Q_BLOCK
    return pl.pallas_call(
        partial(_nsa_prompt_kernel, n_c=n_c),
        out_shape=jax.ShapeDtypeStruct((b * t, QPAD_WIDTH), jnp.bfloat16),
        grid=(b, n_qb),
        in_specs=[rows(QPAD_WIDTH), rows(LANES),
                  pl.BlockSpec((1, n_c, KV_WIDTH), lambda bi, qi: (bi, 0, 0)),
                  pl.BlockSpec((1, KV_WIDTH, n_c), lambda bi, qi: (bi, 0, 0)),
                  seq(2), seq(3), seq(4), seq(5), full(tiles), full(bct)],
        out_specs=rows(QPAD_WIDTH),
        scratch_shapes=[pltpu.VMEM((rows_all, LANES), jnp.float32), pltpu.VMEM((rows_all, LANES), jnp.float32),
                        pltpu.VMEM((rows_all, KV_WIDTH), jnp.float32),
                        pltpu.VMEM((N_SEL_PAD, N_KV * Q_BLOCK), jnp.float32)],
        compiler_params=pltpu.CompilerParams(dimension_semantics=("arbitrary", "arbitrary"),
                                             vmem_limit_bytes=V7X_VMEM_LIMIT_BYTES),
        name="nsa_prompt",
    )(qpad, small, kcb, vct, kv3, kv3, kv3, kv3, tiles, bct)


def _pad_nsa_out_weight(w_proj_nsa):
    zeros = jnp.zeros((HEAD_DIM, D_MODEL), w_proj_nsa.dtype)
    rows = []
    for h in range(N_HEADS):
        wh = w_proj_nsa[h * HEAD_DIM:(h + 1) * HEAD_DIM]
        rows += [wh, zeros] if h // GROUP == 0 else [zeros, wh]
    return jnp.concatenate(rows, axis=0)


SUB_WIDTH = D_CMP * KV_WIDTH
PHI_HIDDEN = 2 * HEAD_DIM
PAGE_SUBS = PAGE_SIZE // D_CMP
CMP_PAGES_PER_STEP = 16


def _compress_weights(pos_emb, w1, w2):
    eye = jnp.eye(N_KV, dtype=w1.dtype)
    halves = []
    for r in range(CMP_RATIO):
        w1r = w1[r * D_CMP:(r + 1) * D_CMP]
        halves.append(jnp.einsum('ldh,gk->lgdkh', w1r, eye).reshape(SUB_WIDTH, N_KV * PHI_HIDDEN))
    w1big = jnp.concatenate(halves, axis=1).astype(jnp.bfloat16)
    pos = jnp.broadcast_to(pos_emb.reshape(CMP_RATIO, D_CMP, 1, HEAD_DIM), (CMP_RATIO, D_CMP, N_KV, HEAD_DIM))
    w2big = jnp.einsum('hd,gk->ghkd', w2, eye).reshape(N_KV * PHI_HIDDEN, KV_WIDTH).astype(jnp.bfloat16)
    return w1big, pos.reshape(CMP_RATIO, 1, SUB_WIDTH), w2big


def _compress_tokens(lhs0, lhs1, w1big, w2big):
    hw = N_KV * PHI_HIDDEN
    p0 = jnp.dot(lhs0, w1big[:, :hw], preferred_element_type=jnp.float32)
    p1 = jnp.dot(lhs1, w1big[:, hw:], preferred_element_type=jnp.float32)
    n_sub = p1.shape[0]
    hidden = p0 + pltpu.roll(p1, n_sub - 1, 0)
    return jnp.dot(jax.nn.gelu(hidden).astype(jnp.bfloat16), w2big, preferred_element_type=jnp.float32)


def _compress_rows_kernel(rk_ref, rv_ref, w1k_ref, pk_ref, w2k_ref, w1v_ref, pv_ref, w2v_ref, kc_ref, vc_ref):
    for r_ref, w1_ref, p_ref, w2_ref, o_ref in ((rk_ref, w1k_ref, pk_ref, w2k_ref, kc_ref),
                                                 (rv_ref, w1v_ref, pv_ref, w2v_ref, vc_ref)):
        x = r_ref[0]
        lhs = [(x + p_ref[r]).astype(jnp.bfloat16) for r in range(CMP_RATIO)]
        o_ref[0] = _compress_tokens(lhs[0], lhs[1], w1_ref[...], w2_ref[...]).astype(o_ref.dtype)


def _compress_rows(k_rows, v_rows, b, t, wk, wv):
    n_sub = t // D_CMP
    view = lambda a: a.reshape(b, n_sub, SUB_WIDTH)
    seq = pl.BlockSpec((1, n_sub, SUB_WIDTH), lambda bi: (bi, 0, 0))
    full = lambda a: pl.BlockSpec(a.shape, lambda bi: (0,) * a.ndim)
    out = pl.BlockSpec((1, n_sub, KV_WIDTH), lambda bi: (bi, 0, 0))
    return pl.pallas_call(
        _compress_rows_kernel,
        out_shape=[jax.ShapeDtypeStruct((b, n_sub, KV_WIDTH), jnp.bfloat16)] * 2,
        grid=(b,),
        in_specs=[seq, seq] + [full(a) for a in (*wk, *wv)],
        out_specs=[out, out],
        compiler_params=pltpu.CompilerParams(dimension_semantics=("arbitrary",),
                                             vmem_limit_bytes=V7X_VMEM_LIMIT_BYTES),
        name="compress_rows",
    )(view(k_rows), view(v_rows), *wk, *wv)


def _compress_pages_kernel(pt_ref, *refs):
    n_pg = CMP_PAGES_PER_STEP
    k_pages, v_pages = refs[:n_pg], refs[n_pg:2 * n_pg]
    w1k_ref, pk_ref, w2k_ref, w1v_ref, pv_ref, w2v_ref, kc_ref, vc_ref, lhs_ref = refs[2 * n_pg:]
    step = pl.program_id(1)
    for c, (pages, p_ref) in enumerate(((k_pages, pk_ref), (v_pages, pv_ref))):
        for j in range(0, n_pg, 2):
            x = jnp.concatenate([pages[j][0], pages[j + 1][0]], axis=0)
            rows = pl.ds(pl.multiple_of(step * n_pg * PAGE_SUBS + j * PAGE_SUBS, 2 * PAGE_SUBS), 2 * PAGE_SUBS)
            for r in range(CMP_RATIO):
                lhs_ref[c, r, rows, :] = (x + p_ref[r]).astype(jnp.bfloat16)

    @pl.when(step == pl.num_programs(1) - 1)
    def _():
        for c, (w1_ref, w2_ref, o_ref) in enumerate(((w1k_ref, w2k_ref, kc_ref), (w1v_ref, w2v_ref, vc_ref))):
            o_ref[0] = _compress_tokens(lhs_ref[c, 0], lhs_ref[c, 1], w1_ref[...], w2_ref[...]).astype(o_ref.dtype)


def _compress_pages(cache_k, cache_v, page_table, wk, wv):
    b, n_pages = page_table.shape
    n_pool = cache_k.shape[0]
    n_sub = n_pages * PAGE_SUBS
    n_pg = CMP_PAGES_PER_STEP
    view = lambda c: c.reshape(n_pool, PAGE_SUBS, SUB_WIDTH)
    page = lambda j: pl.BlockSpec(
        (1, PAGE_SUBS, SUB_WIDTH),
        lambda bi, si, pt: (jnp.clip(pt[jnp.minimum(bi, b - 1), jnp.minimum(si, n_pages // n_pg - 1) * n_pg + j],
                                     0, n_pool - 1), 0, 0))
    full = lambda a: pl.BlockSpec(a.shape, lambda bi, si, pt: (0,) * a.ndim)
    out = pl.BlockSpec((1, n_sub, KV_WIDTH), lambda bi, si, pt: (bi, 0, 0))
    return pl.pallas_call(
        _compress_pages_kernel,
        out_shape=[jax.ShapeDtypeStruct((b, n_sub, KV_WIDTH), jnp.bfloat16)] * 2,
        grid_spec=pltpu.PrefetchScalarGridSpec(
            num_scalar_prefetch=1,
            grid=(b, n_pages // n_pg),
            in_specs=[page(j) for j in range(n_pg)] * 2 + [full(a) for a in (*wk, *wv)],
            out_specs=[out, out],
            scratch_shapes=[pltpu.VMEM((2, CMP_RATIO, n_sub, SUB_WIDTH), jnp.bfloat16)]),
        compiler_params=pltpu.CompilerParams(dimension_semantics=("arbitrary", "arbitrary"),
                                             vmem_limit_bytes=V7X_VMEM_LIMIT_BYTES),
        name="compress_pages",
    )(page_table, *([view(cache_k)] * n_pg), *([view(cache_v)] * n_pg), *wk, *wv)


N_SEL_DEC = PAST_LEN // L_SEL + 1
N_SEL_DEC_PAD = 256
CUR_BLOCK = PAST_LEN // L_SEL


def _nsa_decode_cmp_kernel(q_ref, kc_ref, vc_ref, bias_ref, oc_ref, imp_ref):
    f32, bf16 = jnp.float32, jnp.bfloat16
    n_c = kc_ref.shape[1]
    s = _dot_nt(q_ref[0], kc_ref[0]) + bias_ref[...]
    mx = jnp.max(s, axis=-1, keepdims=True)
    mx = jnp.where(mx < 0.1 * NEG_BIG, 0.0, mx)
    p = jnp.exp(s - mx)
    p = p / jnp.maximum(jnp.sum(p, axis=-1, keepdims=True), 1e-30)
    pb = p.astype(bf16)
    oc_ref[0] = jnp.dot(pb, vc_ref[0], preferred_element_type=f32)
    ov_c = lax.broadcasted_iota(jnp.int32, (n_c, N_SEL_DEC_PAD), 0)
    ov_s = lax.broadcasted_iota(jnp.int32, (n_c, N_SEL_DEC_PAD), 1)
    overlap = jnp.where((ov_c >= 4 * ov_s - 1) & (ov_c <= 4 * ov_s + 3), 1.0, 0.0).astype(bf16)
    imp_h = jnp.dot(pb, overlap, preferred_element_type=f32)
    head = lax.broadcasted_iota(jnp.int32, imp_h.shape, 0)
    imp_ref[0] = jnp.concatenate(
        [jnp.sum(jnp.where((head >= g * GROUP) & (head < (g + 1) * GROUP), imp_h, 0.0), axis=0, keepdims=True)
         for g in range(N_KV)], axis=1)


def _nsa_decode_topk_kernel(imp_ref, idx_ref, val_ref):
    f32 = jnp.float32
    nb = imp_ref.shape[0]
    s_io = lax.broadcasted_iota(jnp.int32, (N_SEL_DEC_PAD, nb), 0)
    visible = s_io < N_SEL_DEC
    forced = (s_io == 0) | (s_io == CUR_BLOCK) | (s_io == CUR_BLOCK - 1)
    tri_r = lax.broadcasted_iota(jnp.int32, (N_SEL_DEC_PAD, N_SEL_DEC_PAD), 0)
    tri_c = lax.broadcasted_iota(jnp.int32, (N_SEL_DEC_PAD, N_SEL_DEC_PAD), 1)
    before = jnp.where(tri_c < tri_r, 1.0, 0.0).astype(jnp.bfloat16)
    for g in range(N_KV):
        x = imp_ref[:, g * N_SEL_DEC_PAD:(g + 1) * N_SEL_DEC_PAD]
        xt = jnp.concatenate([x[:, i * LANES:(i + 1) * LANES].T for i in range(N_SEL_DEC_PAD // LANES)], axis=0)
        val = jnp.where(visible, xt + jnp.where(forced, FORCE_SCORE, 0.0), -jnp.inf)
        val_ref[...] = val

        def rank_body(s, rank):
            other = val_ref[pl.ds(s, 1), :]
            beats = (other > val) | ((other == val) & (s_io > s))
            return rank + jnp.where(beats, 1, 0)

        rank = lax.fori_loop(0, N_SEL_DEC, rank_body, jnp.zeros(val.shape, jnp.int32))
        sel = (rank < K_SEL) & visible
        n_before = jnp.dot(before, jnp.where(sel, 1.0, 0.0).astype(jnp.bfloat16), preferred_element_type=f32)
        for k in range(K_SEL):
            hit = sel & (n_before == float(k))
            idx_ref[g, pl.ds(k, 1), :] = jnp.sum(jnp.where(hit, s_io.astype(f32), 0.0), axis=0,
                                                 keepdims=True).astype(jnp.int32)


def _nsa_decode_sel_kernel(pt_ref, ix_ref, *refs):
    f32, bf16 = jnp.float32, jnp.bfloat16
    n_blk = N_KV * K_SEL
    k_blocks, v_blocks = refs[:n_blk], refs[n_blk:2 * n_blk]
    (q_ref, oc_ref, gl_ref, ksn_ref, vsn_ref, kwn_ref, vwn_ref, kwin_ref, vwin_ref, near_ref, b0_ref, bw_ref,
     o_ref, kwo_ref, vwo_ref) = refs[2 * n_blk:]
    b = pl.program_id(0)
    q = q_ref[0]
    qf = q.astype(f32)
    round_bf = lambda a: a.astype(bf16).astype(f32)
    group1 = lax.broadcasted_iota(jnp.int32, (N_HEADS, LANES), 0) >= GROUP
    second_half = lax.broadcasted_iota(jnp.int32, (N_HEADS, LANES), 1) >= L_SEL
    neg_tile = jnp.full((N_HEADS, LANES), NEG_BIG, f32)
    no_rows = jnp.zeros((HEAD_DIM, PAGE_SIZE), bf16)

    def block_bias(s):
        tile = jnp.where(s == CUR_BLOCK, neg_tile,
                         jnp.where(s == CUR_BLOCK - 1, near_ref[1],
                                   jnp.where(s == CUR_BLOCK - 2, near_ref[0], near_ref[2])))
        return jnp.where(second_half == ((s & 1) == 1), tile, neg_tile)

    def group_rows(x, g):
        x = x.astype(bf16)
        return jnp.concatenate([x, no_rows] if g == 0 else [no_rows, x], axis=0)

    scores, values, has_new = [], [], []
    for g in range(N_KV):
        ids = [ix_ref[b, g * K_SEL + k] for k in range(K_SEL)]
        kt = jnp.concatenate([group_rows(k_blocks[g * K_SEL + k][0], g) for k in range(K_SEL)], axis=1)
        values.append(jnp.concatenate([group_rows(v_blocks[g * K_SEL + k][0], g) for k in range(K_SEL)], axis=1))
        bias = jnp.concatenate([block_bias(ids[k]) for k in range(K_SEL)], axis=1)
        scores.append(jnp.dot(q, kt, preferred_element_type=f32) + bias)
        flag = ids[0] == CUR_BLOCK
        for k in range(1, K_SEL):
            flag = flag | (ids[k] == CUR_BLOCK)
        has_new.append(jnp.where(flag, 0.0, NEG_BIG))
    s = jnp.where(group1[:, :1], scores[1], scores[0])
    s_new = jnp.sum(qf * round_bf(ksn_ref[0]), axis=-1, keepdims=True) + b0_ref[...]
    s_new = s_new + jnp.where(group1, has_new[1], has_new[0])
    m = jnp.maximum(jnp.max(s, axis=-1, keepdims=True), s_new)
    p = jnp.exp(s - m[:, :1])
    p_new = jnp.exp(s_new - m)
    l = jnp.sum(p, axis=-1, keepdims=True) + p_new
    pb = p.astype(bf16)
    o_sel = jnp.where(group1, _dot_nt(pb, values[1]), _dot_nt(pb, values[0]))
    o_sel = (o_sel + round_bf(p_new) * round_bf(vsn_ref[0])) / l

    kwin, vwin = kwin_ref[0], vwin_ref[0]
    s = jnp.dot(q, kwin.astype(bf16), preferred_element_type=f32) + bw_ref[...]
    s_new = jnp.sum(qf * round_bf(kwn_ref[0]), axis=-1, keepdims=True) + b0_ref[...]
    m = jnp.maximum(jnp.max(s, axis=-1, keepdims=True), s_new)
    p = jnp.exp(s - m[:, :1])
    p_new = jnp.exp(s_new - m)
    l = jnp.sum(p, axis=-1, keepdims=True) + p_new
    o_win = _dot_nt(p.astype(bf16), vwin.astype(bf16))
    o_win = (o_win + round_bf(p_new) * round_bf(vwn_ref[0])) / l

    gates = jax.nn.sigmoid(gl_ref[0])
    o_ref[0] = (oc_ref[0] * gates[0] + o_sel * gates[1] + o_win * gates[2]).astype(bf16)

    wb = kwin.shape[1]
    last = lax.broadcasted_iota(jnp.int32, kwin.shape, 1) == wb - 1
    as_column = lambda row: jnp.broadcast_to(row, (KV_WIDTH, KV_WIDTH)).T[:, :1]
    kwo_ref[0] = jnp.where(last, as_column(kwn_ref[0]), pltpu.roll(kwin, wb - 1, 1))
    vwo_ref[0] = jnp.where(last, as_column(vwn_ref[0]), pltpu.roll(vwin, wb - 1, 1))


def _nsa_decode(qpad, small, kc, vc, ks_new, vs_new, kw_new, vw_new, cache_k_sel, cache_v_sel,
                cache_k_win, cache_v_win, page_table, rel_bias):
    f32 = jnp.float32
    b = qpad.shape[0]
    n_c = kc.shape[1]
    n_pool = cache_k_sel.shape[0]
    wb = cache_k_win.shape[1]
    tab = rel_bias.astype(f32)[_bucket_table()].T
    dist_c = PAST_LEN - (L_CMP - 1) - D_CMP * np.arange(n_c)
    bias_c = jnp.where(jnp.asarray(dist_c >= 0)[None, :], tab[:, np.clip(dist_c, 0, MAX_DIST)], NEG_BIG)
    j = np.arange(LANES) % L_SEL
    near = jnp.stack([tab[:, 2 * L_SEL - j], tab[:, L_SEL - j],
                      jnp.broadcast_to(tab[:, MAX_DIST:], (N_HEADS, LANES))])
    bias0 = jnp.broadcast_to(tab[:, :1], (N_HEADS, LANES))
    bias_w = tab[:, np.minimum(wb - np.arange(wb), MAX_DIST)]
    q3 = qpad.reshape(b, N_HEADS, LANES)

    per_seq = lambda *shape: pl.BlockSpec((1,) + shape, lambda bi, *_: (bi,) + (0,) * len(shape))
    full = lambda a: pl.BlockSpec(a.shape, lambda bi, *_: (0,) * a.ndim)
    oc, imp = pl.pallas_call(
        _nsa_decode_cmp_kernel,
        out_shape=[jax.ShapeDtypeStruct((b, N_HEADS, LANES), f32),
                   jax.ShapeDtypeStruct((b, 1, N_KV * N_SEL_DEC_PAD), f32)],
        grid=(b,),
        in_specs=[per_seq(N_HEADS, LANES), per_seq(n_c, KV_WIDTH), per_seq(n_c, KV_WIDTH), full(bias_c)],
        out_specs=[per_seq(N_HEADS, LANES), per_seq(1, N_KV * N_SEL_DEC_PAD)],
        compiler_params=pltpu.CompilerParams(dimension_semantics=("arbitrary",)),
        name="nsa_decode_cmp",
    )(q3, kc, vc, bias_c)

    idx = pl.pallas_call(
        _nsa_decode_topk_kernel,
        out_shape=jax.ShapeDtypeStruct((N_KV, K_SEL, b), jnp.int32),
        scratch_shapes=[pltpu.VMEM((N_SEL_DEC_PAD, b), f32)],
        name="nsa_decode_topk",
    )(imp.reshape(b, N_KV * N_SEL_DEC_PAD))
    idx = idx.reshape(N_KV * K_SEL, b).T

    halves = PAGE_SIZE // L_SEL

    def sel_block(i):
        def index_map(bi, pt, ix):
            bs = jnp.minimum(bi, b - 1)
            s = jnp.clip(ix[bs, i], 0, CUR_BLOCK - 1)
            return (jnp.clip(pt[bs, s // halves], 0, n_pool - 1) * N_KV + i // K_SEL, 0, 0)
        return pl.BlockSpec((1, HEAD_DIM, PAGE_SIZE), index_map)

    keys_minor = lambda c: c.transpose(0, 2, 3, 1)
    page_groups = lambda c: keys_minor(c).reshape(n_pool * N_KV, HEAD_DIM, PAGE_SIZE)
    win_t = lambda c: keys_minor(c).reshape(b, KV_WIDTH, wb)
    glog = jnp.broadcast_to(small[:, :3 * N_HEADS].reshape(b, N_HEADS, 3).transpose(0, 2, 1)[..., None],
                            (b, 3, N_HEADS, LANES))
    new_row = lambda a: a.reshape(b, 1, KV_WIDTH)
    n_blk = N_KV * K_SEL
    o_pad, k_win, v_win = pl.pallas_call(
        _nsa_decode_sel_kernel,
        out_shape=[jax.ShapeDtypeStruct((b, N_HEADS, LANES), jnp.bfloat16),
                   jax.ShapeDtypeStruct((b, KV_WIDTH, wb), f32), jax.ShapeDtypeStruct((b, KV_WIDTH, wb), f32)],
        grid_spec=pltpu.PrefetchScalarGridSpec(
            num_scalar_prefetch=2,
            grid=(b,),
            in_specs=[sel_block(i) for i in range(n_blk)] * 2
                     + [per_seq(N_HEADS, LANES), per_seq(N_HEADS, LANES), per_seq(3, N_HEADS, LANES)]
                     + [per_seq(1, KV_WIDTH)] * 4 + [per_seq(KV_WIDTH, wb)] * 2
                     + [full(near), full(bias0), full(bias_w)],
            out_specs=[per_seq(N_HEADS, LANES), per_seq(KV_WIDTH, wb), per_seq(KV_WIDTH, wb)]),
        compiler_params=pltpu.CompilerParams(dimension_semantics=("arbitrary",)),
        name="nsa_decode_sel",
    )(page_table, idx, *([page_groups(cache_k_sel)] * n_blk), *([page_groups(cache_v_sel)] * n_blk),
      q3, oc, glog, new_row(ks_new), new_row(vs_new), new_row(kw_new), new_row(vw_new),
      win_t(cache_k_win), win_t(cache_v_win), near, bias0, bias_w)
    rows_major = lambda a: a.reshape(b, N_KV, HEAD_DIM, wb).transpose(0, 3, 1, 2)
    return o_pad.reshape(b, QPAD_WIDTH), rows_major(k_win), rows_major(v_win)


MLSTM_CHUNK = 128
MLSTM_DEC_TILE = 8
CONV_TAIL = 8


def _mlstm_norm_gate(h, zo, norm_g):
    return h * lax.rsqrt(jnp.mean(h * h, axis=-1, keepdims=True) + EPS) * norm_g * jax.nn.sigmoid(zo)


def _mlstm_prompt_kernel(zqk_ref, zv_ref, zo_ref, sm_ref, cw_ref, cb_ref, wq_ref, wk_ref, gb_ref, ng_ref,
                         o_ref, co_ref, no_ref, mo_ref, xbuf_ref, c_ref, n_ref, m_ref):
    f32, bf16 = jnp.float32, jnp.bfloat16
    L = MLSTM_CHUNK
    c = pl.program_id(1)

    @pl.when(c == 0)
    def _():
        xbuf_ref[:CONV_TAIL] = jnp.zeros((CONV_TAIL, M_WIDTH), f32)
        c_ref[...] = jnp.zeros(c_ref.shape, f32)
        n_ref[...] = jnp.zeros(n_ref.shape, f32)
        m_ref[...] = jnp.zeros(m_ref.shape, f32)

    x = zqk_ref[...]
    xbuf_ref[CONV_TAIL:] = x
    conv = cb_ref[...]
    for j in range(CONV_W):
        conv = conv + xbuf_ref[pl.ds(CONV_TAIL - (CONV_W - 1) + j, L), :] * cw_ref[j:j + 1, :]
    xbuf_ref[:CONV_TAIL] = x[L - CONV_TAIL:]
    a = jax.nn.silu(conv).astype(bf16)

    t_io = lax.broadcasted_iota(jnp.int32, (L, L), 0)
    s_io = lax.broadcasted_iota(jnp.int32, (L, L), 1)
    causal = t_io >= s_io
    sm = sm_ref[...]
    for h in range(M_HEADS):
        hs = slice(h * M_DH, (h + 1) * M_DH)
        q = jnp.dot(a[:, hs], wq_ref[h], preferred_element_type=f32)
        k = jnp.dot(a[:, hs], wk_ref[h], preferred_element_type=f32) * (M_DH ** -0.5)
        v = zv_ref[:, hs]
        qb, kb, vb = q.astype(bf16), k.astype(bf16), v.astype(bf16)
        col = 3 * N_HEADS + h
        ig = jnp.broadcast_to(sm[:, col:col + 1], (L, L)) + gb_ref[0:1, hs]
        lf = jax.nn.log_sigmoid(jnp.broadcast_to(sm[:, col + M_HEADS:col + M_HEADS + 1], (L, L)) + gb_ref[1:2, hs])
        bcum = lf
        sh = 1
        while sh < L:
            bcum = bcum + jnp.where(t_io >= sh, pltpu.roll(bcum, sh, 0), 0.0)
            sh *= 2
        m_old = m_ref[h:h + 1, :]
        c_old = c_ref[h]
        n_old = n_ref[h:h + 1, :]
        dmat = jnp.where(causal, bcum - bcum.T + ig.T, -jnp.inf)
        inter = bcum + m_old
        m_t = jnp.maximum(jnp.max(dmat, axis=-1, keepdims=True), inter)
        sc = _dot_nt(qb, kb) * jnp.exp(dmat - m_t)
        decay = jnp.exp(inter - m_t)
        num = decay * _dot_nt(qb, c_old.astype(bf16)) + jnp.dot(sc.astype(bf16), vb, preferred_element_type=f32)
        den = decay * jnp.sum(q * n_old, axis=-1, keepdims=True) + jnp.sum(sc, axis=-1, keepdims=True)
        hh = num / jnp.maximum(jnp.abs(den), jnp.exp(-m_t))
        o_ref[:, hs] = _mlstm_norm_gate(hh, zo_ref[:, hs], ng_ref[:, hs])

        b_last = bcum[L - 1:L, :]
        w_log = b_last - bcum + ig
        m_new = jnp.maximum(b_last + m_old, jnp.max(w_log, axis=0, keepdims=True))
        w = jnp.exp(w_log - m_new)
        carry = jnp.exp(b_last + m_old - m_new)
        c_ref[h] = carry * c_old + jnp.dot((w * v).T.astype(bf16), kb, preferred_element_type=f32)
        n_ref[h:h + 1, :] = carry * n_old + jnp.sum(w * k, axis=0, keepdims=True)
        m_ref[h:h + 1, :] = m_new

    @pl.when(c == pl.num_programs(1) - 1)
    def _():
        co_ref[0] = c_ref[...]
        no_ref[0] = n_ref[...]
        mo_ref[0] = m_ref[...]


def _mlstm_weights(conv_w, conv_b, wq, wk, gate_bias, norm_g):
    gb = jnp.repeat(gate_bias.astype(jnp.float32), M_DH, axis=1)
    return (conv_w, conv_b.reshape(1, M_WIDTH), wq.astype(jnp.bfloat16), wk.astype(jnp.bfloat16), gb,
            norm_g.reshape(1, M_WIDTH))


def _mlstm_prompt(zmqk, zmv, zmo, small, weights, b, t):
    L = MLSTM_CHUNK
    n_ch = t // L
    rows = lambda width: pl.BlockSpec((L, width), lambda bi, ci: (bi * n_ch + ci, 0))
    full = lambda a: pl.BlockSpec(a.shape, lambda bi, ci: (0,) * a.ndim)
    state = lambda *shape: pl.BlockSpec((1,) + shape, lambda bi, ci: (bi,) + (0,) * len(shape))
    f32 = jnp.float32
    return pl.pallas_call(
        _mlstm_prompt_kernel,
        out_shape=[jax.ShapeDtypeStruct((b * t, M_WIDTH), f32), jax.ShapeDtypeStruct((b, M_HEADS, M_DH, M_DH), f32),
                   jax.ShapeDtypeStruct((b, 8, M_DH), f32), jax.ShapeDtypeStruct((b, 8, M_DH), f32)],
        grid=(b, n_ch),
        in_specs=[rows(M_WIDTH), rows(M_WIDTH), rows(M_WIDTH), rows(LANES)] + [full(a) for a in weights],
        out_specs=[rows(M_WIDTH), state(M_HEADS, M_DH, M_DH), state(8, M_DH), state(8, M_DH)],
        scratch_shapes=[pltpu.VMEM((CONV_TAIL + L, M_WIDTH), f32), pltpu.VMEM((M_HEADS, M_DH, M_DH), f32),
                        pltpu.VMEM((8, M_DH), f32), pltpu.VMEM((8, M_DH), f32)],
        compiler_params=pltpu.CompilerParams(dimension_semantics=("arbitrary", "arbitrary")),
        name="mlstm_prompt",
    )(zmqk, zmv, zmo, small, *weights)


def _mlstm_decode_kernel(zqk_ref, conv_ref, zv_ref, zo_ref, sm_ref, c_ref, n_ref, m_ref,
                         cw_ref, cb_ref, wq_ref, wk_ref, gb_ref, ng_ref, o_ref, co_ref, no_ref, mo_ref):
    f32, bf16 = jnp.float32, jnp.bfloat16
    nt = MLSTM_DEC_TILE
    conv = cb_ref[...] + zqk_ref[...] * cw_ref[CONV_W - 1:CONV_W, :]
    for j in range(CONV_W - 1):
        conv = conv + conv_ref[j] * cw_ref[j:j + 1, :]
    a = jax.nn.silu(conv).astype(bf16)
    sm = sm_ref[...]
    lane = lax.broadcasted_iota(jnp.int32, (M_DH, M_DH), 1)
    pad_rows = jnp.zeros((M_DH - nt, M_DH), f32)
    for h in range(M_HEADS):
        hs = slice(h * M_DH, (h + 1) * M_DH)
        q = jnp.dot(a[:, hs], wq_ref[h], preferred_element_type=f32)
        k = jnp.dot(a[:, hs], wk_ref[h], preferred_element_type=f32) * (M_DH ** -0.5)
        v = zv_ref[:, hs]
        col = 3 * N_HEADS + h
        ig = jnp.broadcast_to(sm[:, col:col + 1], (nt, M_DH)) + gb_ref[0:1, hs]
        lf = jax.nn.log_sigmoid(jnp.broadcast_to(sm[:, col + M_HEADS:col + M_HEADS + 1], (nt, M_DH)) + gb_ref[1:2, hs])
        m_old = m_ref[:, hs]
        n_old = n_ref[:, hs]
        m_new = jnp.maximum(lf + m_old, ig)
        decay = jnp.exp(lf + m_old - m_new)
        w = jnp.exp(ig - m_new)
        sc = jnp.sum(q * k, axis=-1, keepdims=True) * w
        den = decay * jnp.sum(n_old * q, axis=-1, keepdims=True) + sc
        wv_t = jnp.concatenate([w * v, pad_rows], axis=0).T
        cq_t = jnp.zeros((M_DH, M_DH), f32)
        for i in range(nt):
            c_old = c_ref[i, h]
            cq = jnp.sum(c_old * q[i:i + 1, :], axis=-1, keepdims=True)
            cq_t = jnp.where(lane == i, cq, cq_t)
            co_ref[i, h] = decay[i:i + 1, :] * c_old + wv_t[:, i:i + 1] * k[i:i + 1, :]
        num = decay * cq_t.T[:nt] + sc * v
        hh = num / jnp.maximum(jnp.abs(den), jnp.exp(-m_new))
        o_ref[:, hs] = _mlstm_norm_gate(hh, zo_ref[:, hs], ng_ref[:, hs])
        no_ref[:, hs] = decay * n_old + w * k
        mo_ref[:, hs] = m_new


def _mlstm_decode(zmqk, zmv, zmo, small, state_c, state_n, state_m, state_conv, weights):
    b = zmqk.shape[0]
    nt = MLSTM_DEC_TILE
    f32 = jnp.float32
    rows = lambda width: pl.BlockSpec((nt, width), lambda i: (i, 0))
    full = lambda a: pl.BlockSpec(a.shape, lambda i: (0,) * a.ndim)
    cspec = pl.BlockSpec((nt, M_HEADS, M_DH, M_DH), lambda i: (i, 0, 0, 0))
    conv_t = state_conv.transpose(1, 0, 2)
    m_rep = jnp.repeat(state_m, M_DH, axis=1)
    return pl.pallas_call(
        _mlstm_decode_kernel,
        out_shape=[jax.ShapeDtypeStruct((b, M_WIDTH), f32), jax.ShapeDtypeStruct(state_c.shape, f32),
                   jax.ShapeDtypeStruct((b, M_WIDTH), f32), jax.ShapeDtypeStruct((b, M_WIDTH), f32)],
        grid=(b // nt,),
        in_specs=[rows(M_WIDTH), pl.BlockSpec((CONV_W - 1, nt, M_WIDTH), lambda i: (0, i, 0)), rows(M_WIDTH),
                  rows(M_WIDTH), rows(LANES), cspec, rows(M_WIDTH), rows(M_WIDTH)] + [full(a) for a in weights],
        out_specs=[rows(M_WIDTH), cspec, rows(M_WIDTH), rows(M_WIDTH)],
        compiler_params=pltpu.CompilerParams(dimension_semantics=("arbitrary",)),
        name="mlstm_decode",
    )(zmqk, conv_t, zmv, zmo, small, state_c, state_n.reshape(b, M_WIDTH), m_rep, *weights)


def _merge_kernel(x_ref, on_ref, om_ref, zm_ref, wn_ref, wm_ref, wo_ref, o_ref):
    zm = zm_ref[...]
    g_a = jax.nn.sigmoid(zm[:, :D_MODEL])
    g_b = jax.nn.sigmoid(zm[:, D_MODEL:])
    ya = jnp.dot(on_ref[...], wn_ref[...], preferred_element_type=jnp.float32)
    yb = jnp.dot(om_ref[...].astype(jnp.bfloat16), wm_ref[...], preferred_element_type=jnp.float32)
    y = (g_a * ya + g_b * yb).astype(jnp.bfloat16)
    o_ref[...] = x_ref[...] + jnp.dot(y, wo_ref[...], preferred_element_type=jnp.float32)


def _merge(x, o_nsa, o_mlstm, zmerge, w_proj_nsa, w_proj_mlstm, w_out):
    n = x.shape[0]
    tm = min(PROJ_TOKEN_TILE, n)
    row = lambda width: pl.BlockSpec((tm, width), lambda i: (i, 0))
    full = lambda a: pl.BlockSpec(a.shape, lambda i: (0, 0))
    return pl.pallas_call(
        _merge_kernel,
        out_shape=jax.ShapeDtypeStruct((n, D_MODEL), jnp.float32),
        grid=(n // tm,),
        in_specs=[row(D_MODEL), row(o_nsa.shape[1]), row(M_WIDTH), row(2 * D_MODEL),
                  full(w_proj_nsa), full(w_proj_mlstm), full(w_out)],
        out_specs=row(D_MODEL),
        compiler_params=pltpu.CompilerParams(dimension_semantics=("arbitrary",),
                                             vmem_limit_bytes=V7X_VMEM_LIMIT_BYTES),
        name="merge_out_proj",
    )(x, o_nsa, o_mlstm, zmerge, w_proj_nsa, w_proj_mlstm, w_out)


def _kv_rows(a, b, t):
    return a.reshape(b, t, N_KV, HEAD_DIM)


def _mix_prompt(proj, b, t, cmp_k, cmp_v, mlstm_w, rel_bias):
    qpad, kvb, kc_rows, vc_rows, ks_rows, vs_rows, kw_rows, vw_rows, zmqk, zmv, zmo, zmerge, small = proj
    kc, vc = _compress_rows(kc_rows, vc_rows, b, t, cmp_k, cmp_v)
    o_nsa = _nsa_prompt(qpad, small, kvb, kc, vc.transpose(0, 2, 1), rel_bias, b, t)
    o_mlstm, c_f, n_f, m_f = _mlstm_prompt(zmqk, zmv, zmo, small, mlstm_w, b, t)
    n_keep = min(WINDOW, t)
    states = (_kv_rows(kc_rows, b, t), _kv_rows(vc_rows, b, t), _kv_rows(ks_rows, b, t), _kv_rows(vs_rows, b, t),
              _kv_rows(kw_rows, b, t)[:, t - n_keep:], _kv_rows(vw_rows, b, t)[:, t - n_keep:],
              c_f, n_f[:, :M_HEADS], m_f[:, :M_HEADS, 0], zmqk.reshape(b, t, M_WIDTH)[:, t - (CONV_W - 1):])
    return o_nsa, o_mlstm, zmerge, states


def _mix_decode(proj, b, caches, mlstm_state, page_table, cmp_k, cmp_v, mlstm_w, rel_bias):
    cache_k_cmp, cache_v_cmp, cache_k_sel, cache_v_sel, cache_k_win, cache_v_win = caches
    state_c, state_n, state_m, state_conv = mlstm_state
    qpad, _, kc_new, vc_new, ks_new, vs_new, kw_new, vw_new, zmqk, zmv, zmo, zmerge, small = proj
    kc, vc = _compress_pages(cache_k_cmp, cache_v_cmp, page_table, cmp_k, cmp_v)
    o_nsa, k_win, v_win = _nsa_decode(qpad, small, kc, vc, ks_new, vs_new, kw_new, vw_new, cache_k_sel, cache_v_sel,
                                      cache_k_win, cache_v_win, page_table, rel_bias)
    o_mlstm, c_n, n_n, m_rep = _mlstm_decode(zmqk, zmv, zmo, small, state_c, state_n, state_m, state_conv, mlstm_w)
    wb = cache_k_win.shape[1]
    conv_new = jnp.concatenate([state_conv[:, 1:], zmqk[:, None, :]], axis=1)
    states = (_kv_rows(kc_new, b, 1), _kv_rows(vc_new, b, 1), _kv_rows(ks_new, b, 1), _kv_rows(vs_new, b, 1),
              _kv_rows(k_win, b, wb), _kv_rows(v_win, b, wb),
              c_n, n_n.reshape(b, M_HEADS, M_DH), m_rep[:, ::M_DH], conv_new)
    return o_nsa, o_mlstm, zmerge, states


def kernel(x_prompt, x_sample, cache_k_cmp, cache_v_cmp, cache_k_sel, cache_v_sel, cache_k_win, cache_v_win, state_mlstm_C, state_mlstm_n, state_mlstm_m, state_mlstm_conv, page_table, norm_ffn1, ffn1_w_in, ffn1_w_out, norm_mix, w_mix_in, cmp_pos_k, cmp_pos_v, cmp_phi_k1, cmp_phi_k2, cmp_phi_v1, cmp_phi_v2, rel_bias, mlstm_conv_w, mlstm_conv_b, mlstm_wq, mlstm_wk, mlstm_gate_bias, mlstm_norm, w_proj_nsa, w_proj_mlstm, w_out, norm_ffn2, ffn2_w_in, ffn2_w_out, norm_final):
    assert x_sample.shape[1] == DEC_SEQ == 1
    bf = lambda w: w.astype(jnp.bfloat16)
    w1i, w1o, w2i, w2o = bf(ffn1_w_in), bf(ffn1_w_out), bf(ffn2_w_in), bf(ffn2_w_out)
    w_mix = _regroup_mix_weight(w_mix_in)
    w_nsa_out, wm, wo = bf(_pad_nsa_out_weight(w_proj_nsa)), bf(w_proj_mlstm), bf(w_out)
    cmp_k = _compress_weights(cmp_pos_k, cmp_phi_k1, cmp_phi_k2)
    cmp_v = _compress_weights(cmp_pos_v, cmp_phi_v1, cmp_phi_v2)
    mlstm_w = _mlstm_weights(mlstm_conv_w, mlstm_conv_b, mlstm_wq, mlstm_wk, mlstm_gate_bias, mlstm_norm)

    def layer(x3, mix_fn):
        b, t, _ = x3.shape
        x = x3.reshape(b * t, D_MODEL)
        x1 = _ffn(x, norm_ffn1, w1i, w1o, norm_final, final_norm=False)
        proj = _project(x1, norm_mix, w_mix)
        o_nsa, o_mlstm, zmerge, states = mix_fn(proj, b, t)
        x2 = _merge(x1, o_nsa, o_mlstm, zmerge, w_nsa_out, wm, wo)
        y = _ffn(x2, norm_ffn2, w2i, w2o, norm_final, final_norm=True)
        return y.reshape(b, t, D_MODEL), states

    y_prompt, st_p = layer(x_prompt, lambda proj, b, t: _mix_prompt(proj, b, t, cmp_k, cmp_v, mlstm_w, rel_bias))
    y_sample, st_s = layer(x_sample, lambda proj, b, t: _mix_decode(
        proj, b, (cache_k_cmp, cache_v_cmp, cache_k_sel, cache_v_sel, cache_k_win, cache_v_win),
        (state_mlstm_C, state_mlstm_n, state_mlstm_m, state_mlstm_conv), page_table, cmp_k, cmp_v, mlstm_w, rel_bias))
    k_cmp_p, v_cmp_p, k_sel_p, v_sel_p, k_win_p, v_win_p, C_p, n_p, m_p, conv_p = st_p
    k_cmp_s, v_cmp_s, k_sel_s, v_sel_s, k_win_s, v_win_s, C_s, n_s, m_s, conv_s = st_s
    return (y_prompt, y_sample, k_cmp_p, k_cmp_s, v_cmp_p, v_cmp_s, k_sel_p, k_sel_s, v_sel_p, v_sel_s,
            k_win_p, k_win_s, v_win_p, v_win_s, C_p, C_s, n_p, n_s, m_p, m_s, conv_p, conv_s)
```

```python
import math
from functools import partial

import numpy as np
import jax
import jax.numpy as jnp
from jax import lax
from jax.experimental import pallas as pl
from jax.experimental.pallas import tpu as pltpu

D_MODEL = 1024
SEQ = 8192
DEC_SEQ = 1
PAST_LEN = 8192
PAGE_SIZE = 128
N_HEADS = 8
N_KV = 2
GROUP = N_HEADS // N_KV
HEAD_DIM = 64
NSA_WIDTH = N_HEADS * HEAD_DIM
KV_WIDTH = N_KV * HEAD_DIM
L_CMP = 32
D_CMP = 16
CMP_RATIO = L_CMP // D_CMP
L_SEL = 64
K_SEL = 16
WINDOW = 512
Q_BLOCK = 128
FORCE_SCORE = 1000.0
N_BUCKETS = 32
MAX_DIST = 128
M_HEADS = 4
M_DH = 128
M_WIDTH = M_HEADS * M_DH
CONV_W = 4
M_CHUNK = 64
D_FF = 2816
EPS = 1e-6

V7X_VMEM_LIMIT_BYTES = 56 * 1024 * 1024
LANES = 128

QPAD_WIDTH = N_HEADS * LANES
N_SEL_PAD = 128
FAR_TILES = 4
NEG_BIG = -1e30
SEL_OFF = -32768.0

FFN_TOKEN_TILE = 512
FFN_FF_TILE = 1408
PROJ_TOKEN_TILE = 512

_MIX_SIZES = (NSA_WIDTH, KV_WIDTH, KV_WIDTH, KV_WIDTH, KV_WIDTH, KV_WIDTH, KV_WIDTH, 3 * N_HEADS,
              M_WIDTH, M_WIDTH, 2 * M_HEADS, M_WIDTH, 2 * D_MODEL)
_MIX_OFFS = np.concatenate([[0], np.cumsum(_MIX_SIZES)]).tolist()
_PROJ_SEGS = (("kv", 6 * KV_WIDTH), ("mqk", M_WIDTH), ("mv", M_WIDTH),
              ("mo", M_WIDTH), ("merge", 2 * D_MODEL), ("small", LANES))


def _rms(x, g):
    return x * lax.rsqrt(jnp.mean(x * x, axis=-1, keepdims=True) + EPS) * g


def _ffn_kernel(x_ref, g_ref, wg_ref, wu_ref, wo_ref, gf_ref, o_ref, h_ref, acc_ref, *, final_norm):
    j = pl.program_id(1)

    @pl.when(j == 0)
    def _():
        h_ref[...] = _rms(x_ref[...], g_ref[...]).astype(jnp.bfloat16)
        acc_ref[...] = jnp.zeros_like(acc_ref)

    h = h_ref[...]
    gate = jnp.dot(h, wg_ref[...], preferred_element_type=jnp.float32)
    up = jnp.dot(h, wu_ref[...], preferred_element_type=jnp.float32)
    a = (jax.nn.silu(gate) * up).astype(jnp.bfloat16)
    acc_ref[...] += jnp.dot(a, wo_ref[...], preferred_element_type=jnp.float32)

    @pl.when(j == pl.num_programs(1) - 1)
    def _():
        y = x_ref[...] + 0.5 * acc_ref[...]
        if final_norm:
            y = _rms(y, gf_ref[...])
        o_ref[...] = y


def _ffn(x, g, w_in, w_out, g_final, *, final_norm):
    n = x.shape[0]
    tm = min(FFN_TOKEN_TILE, n)
    nj = D_FF // FFN_FF_TILE
    return pl.pallas_call(
        partial(_ffn_kernel, final_norm=final_norm),
        out_shape=jax.ShapeDtypeStruct((n, D_MODEL), jnp.float32),
        grid=(n // tm, nj),
        in_specs=[
            pl.BlockSpec((tm, D_MODEL), lambda i, j: (i, 0)),
            pl.BlockSpec((1, D_MODEL), lambda i, j: (0, 0)),
            pl.BlockSpec((D_MODEL, FFN_FF_TILE), lambda i, j: (0, j)),
            pl.BlockSpec((D_MODEL, FFN_FF_TILE), lambda i, j: (0, j + D_FF // FFN_FF_TILE)),
            pl.BlockSpec((FFN_FF_TILE, D_MODEL), lambda i, j: (j, 0)),
            pl.BlockSpec((1, D_MODEL), lambda i, j: (0, 0)),
        ],
        out_specs=pl.BlockSpec((tm, D_MODEL), lambda i, j: (i, 0)),
        scratch_shapes=[pltpu.VMEM((tm, D_MODEL), jnp.bfloat16), pltpu.VMEM((tm, D_MODEL), jnp.float32)],
        compiler_params=pltpu.CompilerParams(dimension_semantics=("arbitrary", "arbitrary"),
                                             vmem_limit_bytes=V7X_VMEM_LIMIT_BYTES),
        name="ffn",
    )(x, g.reshape(1, D_MODEL), w_in, w_in, w_out, g_final.reshape(1, D_MODEL))


def _proj_kernel(x_ref, g_ref, w_ref, qpad_ref, kvb_ref, *o_refs):
    h = _rms(x_ref[...], g_ref[...]).astype(jnp.bfloat16)
    zq = jnp.dot(h, w_ref[:, :QPAD_WIDTH], preferred_element_type=jnp.float32)
    qpad_ref[...] = (zq * (HEAD_DIM ** -0.5)).astype(jnp.bfloat16)
    off = QPAD_WIDTH
    o_refs = list(o_refs)
    for name, width in _PROJ_SEGS:
        z = jnp.dot(h, w_ref[:, off:off + width], preferred_element_type=jnp.float32)
        if name == "kv":
            kvb_ref[...] = z.astype(jnp.bfloat16)
            for i in range(6):
                o_refs.pop(0)[...] = z[:, i * KV_WIDTH:(i + 1) * KV_WIDTH]
        else:
            o_refs.pop(0)[...] = z
        off += width


def _regroup_mix_weight(w_mix_in):
    seg = lambda i: w_mix_in[:, _MIX_OFFS[i]:_MIX_OFFS[i + 1]]
    wq = seg(0)
    zeros = jnp.zeros((D_MODEL, HEAD_DIM), w_mix_in.dtype)
    qpad = []
    for h in range(N_HEADS):
        wh = wq[:, h * HEAD_DIM:(h + 1) * HEAD_DIM]
        qpad += [wh, zeros] if h // GROUP == 0 else [zeros, wh]
    small = jnp.concatenate([seg(7), seg(10), jnp.zeros((D_MODEL, LANES - 3 * N_HEADS - 2 * M_HEADS), w_mix_in.dtype)], axis=1)
    cols = qpad + [seg(i) for i in range(1, 7)] + [seg(8), seg(9), seg(11), seg(12), small]
    return jnp.concatenate(cols, axis=1).astype(jnp.bfloat16)


def _project(x, g, w_regrouped):
    n = x.shape[0]
    tm = min(PROJ_TOKEN_TILE, n)
    wcols = w_regrouped.shape[1]
    row = lambda width: pl.BlockSpec((tm, width), lambda i: (i, 0))
    widths = []
    for name, width in _PROJ_SEGS:
        widths += [KV_WIDTH] * 6 if name == "kv" else [width]
    return pl.pallas_call(
        _proj_kernel,
        out_shape=[jax.ShapeDtypeStruct((n, QPAD_WIDTH), jnp.bfloat16),
                   jax.ShapeDtypeStruct((n, 6 * KV_WIDTH), jnp.bfloat16)]
                  + [jax.ShapeDtypeStruct((n, width), jnp.float32) for width in widths],
        grid=(n // tm,),
        in_specs=[
            row(D_MODEL),
            pl.BlockSpec((1, D_MODEL), lambda i: (0, 0)),
            pl.BlockSpec((D_MODEL, wcols), lambda i: (0, 0)),
        ],
        out_specs=[row(QPAD_WIDTH), row(6 * KV_WIDTH)] + [row(width) for width in widths],
        compiler_params=pltpu.CompilerParams(dimension_semantics=("arbitrary",),
                                             vmem_limit_bytes=V7X_VMEM_LIMIT_BYTES),
        name="mix_in_proj",
    )(x, g.reshape(1, D_MODEL), w_regrouped)


def _dot_nt(a, b):
    return lax.dot_general(a, b, (((1,), (1,)), ((), ())), preferred_element_type=jnp.float32)


def _bucket_table():
    d = np.arange(MAX_DIST + 1)
    max_exact = N_BUCKETS // 2
    nf = np.maximum(d, 1).astype(np.float64)
    large = max_exact + (np.log(nf / max_exact) / math.log(MAX_DIST / max_exact) * (N_BUCKETS - max_exact)).astype(np.int64)
    return np.where(d < max_exact, d, np.minimum(large, N_BUCKETS - 1)).astype(np.int32)


def _bias_tables(rel_bias, n_c):
    tab = rel_bias.astype(jnp.float32)[_bucket_table()].T
    i = np.arange(Q_BLOCK)[:, None]
    j = np.arange(Q_BLOCK)[None, :]
    far = tab[:, MAX_DIST][:, None, None]
    t0 = tab[:, np.clip(i - j, 0, MAX_DIST)] - far
    t1 = tab[:, np.minimum(Q_BLOCK + i - j, MAX_DIST)] - far
    tiles = jnp.stack([t0, t1]).reshape(2, N_HEADS * Q_BLOCK, Q_BLOCK)
    lo, hi = -2 * Q_BLOCK // D_CMP, Q_BLOCK // D_CMP
    dist = np.arange(Q_BLOCK)[None, :] - (L_CMP - 1) - D_CMP * np.arange(lo, hi)[:, None]
    band = jnp.where(jnp.asarray(dist >= 0), tab[:, np.clip(dist, 0, MAX_DIST)], NEG_BIG)
    bc = jnp.concatenate([jnp.broadcast_to(tab[:, MAX_DIST][:, None, None], (N_HEADS, n_c + lo, Q_BLOCK)), band,
                          jnp.full((N_HEADS, n_c - hi, Q_BLOCK), NEG_BIG, jnp.float32)], axis=1)
    bct = bc.transpose(1, 0, 2).reshape(2 * n_c, N_HEADS * Q_BLOCK)
    return tiles, bct


def _nsa_prompt_kernel(q_ref, gate_ref, kc_ref, vct_ref, ks_ref, vs_ref, kw_ref, vw_ref, tiles_ref, bct_ref,
                       o_ref, m_ref, l_ref, acc_ref, imp_ref, *, n_c):
    f32, bf16 = jnp.float32, jnp.bfloat16
    qb = pl.program_id(1)
    rows_all = N_HEADS * Q_BLOCK
    qi = lax.broadcasted_iota(jnp.int32, (rows_all, Q_BLOCK), 0) & (Q_BLOCK - 1)
    kj = lax.broadcasted_iota(jnp.int32, (rows_all, Q_BLOCK), 1)
    causal = qi >= kj
    window_edge = kj >= qi
    sig = jax.nn.sigmoid(gate_ref[...])

    s_io = lax.broadcasted_iota(jnp.int32, (N_SEL_PAD, N_KV * Q_BLOCK), 0)
    i_io = lax.broadcasted_iota(jnp.int32, (N_SEL_PAD, N_KV * Q_BLOCK), 1) & (Q_BLOCK - 1)
    qpos = qb * Q_BLOCK + i_io
    cur = 2 * qb + (i_io >= L_SEL).astype(jnp.int32)
    blk_valid = s_io * L_SEL <= qpos
    blk_forced = (s_io == 0) | (s_io == cur) | (s_io == cur - 1)
    ov_s = lax.broadcasted_iota(jnp.int32, (N_SEL_PAD, n_c), 0)
    ov_c = lax.broadcasted_iota(jnp.int32, (N_SEL_PAD, n_c), 1)
    overlap_t = jnp.where((ov_c >= 4 * ov_s - 1) & (ov_c <= 4 * ov_s + 3), 1.0, 0.0).astype(bf16)

    def flash_init():
        m_ref[...] = jnp.full(m_ref.shape, NEG_BIG, f32)
        l_ref[...] = jnp.zeros(l_ref.shape, f32)
        acc_ref[...] = jnp.zeros(acc_ref.shape, f32)

    def flash_step(s, v_tile):
        m_old = m_ref[...]
        m_new = jnp.maximum(m_old, jnp.max(s, axis=-1, keepdims=True))
        alpha = jnp.exp(m_old - m_new)
        p = jnp.exp(s - jnp.concatenate([m_new] * (s.shape[1] // LANES), axis=1))
        l_ref[...] = alpha * l_ref[...] + jnp.sum(p, axis=-1, keepdims=True)
        acc_ref[...] = alpha * acc_ref[...] + jnp.dot(p.astype(bf16), v_tile, preferred_element_type=f32)
        m_ref[...] = m_new

    def key_rows(kt, n_tiles=1):
        return pl.ds(pl.multiple_of(kt * Q_BLOCK, Q_BLOCK), n_tiles * Q_BLOCK)

    q_all = jnp.concatenate([q_ref[:, h * LANES:(h + 1) * LANES] for h in range(N_HEADS)], axis=0)

    st = _dot_nt(kc_ref[0], q_all)
    st = st + bct_ref[pl.ds(pl.multiple_of(n_c - 8 * qb, 8), n_c), :]
    mx = jnp.max(st, axis=0, keepdims=True)
    mx = jnp.where(mx < 0.1 * NEG_BIG, 0.0, mx)
    p = jnp.exp(st - mx)
    p = p / jnp.maximum(jnp.sum(p, axis=0, keepdims=True), 1e-30)
    pb = p.astype(bf16)
    oc_t = jnp.dot(vct_ref[0], pb, preferred_element_type=f32)

    imps = []
    for g in range(N_KV):
        imp_g = None
        for r in range(GROUP):
            h = GROUP * g + r
            part = jnp.dot(overlap_t, pb[:, h * Q_BLOCK:(h + 1) * Q_BLOCK], preferred_element_type=f32)
            imp_g = part if imp_g is None else imp_g + part
        imps.append(imp_g)
    imp = jnp.concatenate(imps, axis=1)
    imp = jnp.where(blk_valid, imp + jnp.where(blk_forced, FORCE_SCORE, 0.0), -jnp.inf)
    imp_ref[...] = imp

    def rank_body(s, rank):
        other = imp_ref[pl.ds(s, 1), :]
        beats = (other > imp) | ((other == imp) & (s_io > s))
        return rank + jnp.where(beats, 1, 0)

    rank = lax.fori_loop(0, 2 * qb + 2, rank_body, jnp.zeros(imp.shape, jnp.int32))
    sel_off = jnp.where(rank < K_SEL, 0.0, SEL_OFF)
    sel_rows = []
    for g in range(N_KV):
        sel_rows += [sel_off[:, g * Q_BLOCK:(g + 1) * Q_BLOCK].T.astype(bf16)] * GROUP
    lhs_sel = jnp.concatenate([jnp.concatenate(sel_rows, axis=0), q_all], axis=1)

    def sel_scores(kt, n_tiles=1):
        key_blk = jnp.right_shift(lax.broadcasted_iota(jnp.int32, (n_tiles * Q_BLOCK, N_SEL_PAD), 0),
                                  L_SEL.bit_length() - 1)
        blk_lane = lax.broadcasted_iota(jnp.int32, (n_tiles * Q_BLOCK, N_SEL_PAD), 1)
        onehot = jnp.where(blk_lane == 2 * kt + key_blk, 1.0, 0.0).astype(bf16)
        rhs = jnp.concatenate([onehot, ks_ref[0, key_rows(kt, n_tiles), :]], axis=1)
        return _dot_nt(lhs_sel, rhs)

    flash_init()

    n_far = jnp.maximum(qb - 1, 0)

    def far_step(kt, n_tiles):
        flash_step(sel_scores(kt, n_tiles), vs_ref[0, key_rows(kt, n_tiles), :])

    def far_body(i, carry):
        far_step(FAR_TILES * i, FAR_TILES)
        return carry

    lax.fori_loop(0, n_far // FAR_TILES, far_body, 0)
    width = FAR_TILES // 2
    while width >= 1:
        @pl.when((n_far & width) != 0)
        def _(width=width):
            far_step((n_far // (2 * width)) * (2 * width), width)
        width //= 2

    @pl.when(qb >= 1)
    def _():
        flash_step(sel_scores(qb - 1) + tiles_ref[1], vs_ref[0, key_rows(qb - 1), :])

    flash_step(jnp.where(causal, sel_scores(qb) + tiles_ref[0], NEG_BIG), vs_ref[0, key_rows(qb), :])
    o_sel = acc_ref[...] / l_ref[...]

    flash_init()
    for dt in range(WINDOW // Q_BLOCK + 1):
        @pl.when(qb >= dt)
        def _():
            s = _dot_nt(q_all, kw_ref[0, key_rows(qb - dt), :])
            if dt < 2:
                s = s + tiles_ref[dt]
            if dt == 0:
                s = jnp.where(causal, s, NEG_BIG)
            if dt == WINDOW // Q_BLOCK:
                s = jnp.where(window_edge, s, NEG_BIG)
            flash_step(s, vw_ref[0, key_rows(qb - dt), :])
    o_win = acc_ref[...] / l_ref[...]

    for h in range(N_HEADS):
        rs = slice(h * Q_BLOCK, (h + 1) * Q_BLOCK)
        o = (oc_t[:, rs].T * sig[:, 3 * h:3 * h + 1] + o_sel[rs] * sig[:, 3 * h + 1:3 * h + 2]
             + o_win[rs] * sig[:, 3 * h + 2:3 * h + 3])
        o_ref[:, h * LANES:(h + 1) * LANES] = o.astype(bf16)


def _nsa_prompt(qpad, small, kvb, kcb, vct, rel_bias, b, t):
    n_c = t // D_CMP
    n_qb = t // Q_BLOCK
    tiles, bct = _bias_tables(rel_bias, n_c)
    kv3 = kvb.reshape(b, t, 6 * KV_WIDTH)
    rows = lambda width: pl.BlockSpec((Q_BLOCK, width), lambda bi, qi: (bi * n_qb + qi, 0))
    seq = lambda lane_block: pl.BlockSpec((1, t, KV_WIDTH), lambda bi, qi: (bi, 0, lane_block))
    full = lambda a: pl.BlockSpec(a.shape, lambda bi, qi: (0,) * a.ndim)
    rows_all = N_HEADS * Q_BLOCK
    return pl.pallas_call(
        partial(_nsa_prompt_kernel, n_c=n_c),
        out_shape=jax.ShapeDtypeStruct((b * t, QPAD_WIDTH), jnp.bfloat16),
        grid=(b, n_qb),
        in_specs=[rows(QPAD_WIDTH), rows(LANES),
                  pl.BlockSpec((1, n_c, KV_WIDTH), lambda bi, qi: (bi, 0, 0)),
                  pl.BlockSpec((1, KV_WIDTH, n_c), lambda bi, qi: (bi, 0, 0)),
                  seq(2), seq(3), seq(4), seq(5), full(tiles), full(bct)],
        out_specs=rows(QPAD_WIDTH),
        scratch_shapes=[pltpu.VMEM((rows_all, LANES), jnp.float32), pltpu.VMEM((rows_all, LANES), jnp.float32),
                        pltpu.VMEM((rows_all, KV_WIDTH), jnp.float32),
                        pltpu.VMEM((N_SEL_PAD, N_KV * Q_BLOCK), jnp.float32)],
        compiler_params=pltpu.CompilerParams(dimension_semantics=("arbitrary", "arbitrary"),
                                             vmem_limit_bytes=V7X_VMEM_LIMIT_BYTES),
        name="nsa_prompt",
    )(qpad, small, kcb, vct, kv3, kv3, kv3, kv3, tiles, bct)


def _pad_nsa_out_weight(w_proj_nsa):
    zeros = jnp.zeros((HEAD_DIM, D_MODEL), w_proj_nsa.dtype)
    rows = []
    for h in range(N_HEADS):
        wh = w_proj_nsa[h * HEAD_DIM:(h + 1) * HEAD_DIM]
        rows += [wh, zeros] if h // GROUP == 0 else [zeros, wh]
    return jnp.concatenate(rows, axis=0)


SUB_WIDTH = D_CMP * KV_WIDTH
PHI_HIDDEN = 2 * HEAD_DIM
PAGE_SUBS = PAGE_SIZE // D_CMP
CMP_PAGES_PER_STEP = 16


def _compress_weights(pos_emb, w1, w2):
    eye = jnp.eye(N_KV, dtype=w1.dtype)
    halves = []
    for r in range(CMP_RATIO):
        w1r = w1[r * D_CMP:(r + 1) * D_CMP]
        halves.append(jnp.einsum('ldh,gk->lgdkh', w1r, eye).reshape(SUB_WIDTH, N_KV * PHI_HIDDEN))
    w1big = jnp.concatenate(halves, axis=1).astype(jnp.bfloat16)
    pos = jnp.broadcast_to(pos_emb.reshape(CMP_RATIO, D_CMP, 1, HEAD_DIM), (CMP_RATIO, D_CMP, N_KV, HEAD_DIM))
    w2big = jnp.einsum('hd,gk->ghkd', w2, eye).reshape(N_KV * PHI_HIDDEN, KV_WIDTH).astype(jnp.bfloat16)
    return w1big, pos.reshape(CMP_RATIO, 1, SUB_WIDTH), w2big


def _compress_tokens(lhs0, lhs1, w1big, w2big):
    hw = N_KV * PHI_HIDDEN
    p0 = jnp.dot(lhs0, w1big[:, :hw], preferred_element_type=jnp.float32)
    p1 = jnp.dot(lhs1, w1big[:, hw:], preferred_element_type=jnp.float32)
    n_sub = p1.shape[0]
    hidden = p0 + pltpu.roll(p1, n_sub - 1, 0)
    return jnp.dot(jax.nn.gelu(hidden).astype(jnp.bfloat16), w2big, preferred_element_type=jnp.float32)


def _compress_rows_kernel(rk_ref, rv_ref, w1k_ref, pk_ref, w2k_ref, w1v_ref, pv_ref, w2v_ref, kc_ref, vc_ref):
    for r_ref, w1_ref, p_ref, w2_ref, o_ref in ((rk_ref, w1k_ref, pk_ref, w2k_ref, kc_ref),
                                                 (rv_ref, w1v_ref, pv_ref, w2v_ref, vc_ref)):
        x = r_ref[0]
        lhs = [(x + p_ref[r]).astype(jnp.bfloat16) for r in range(CMP_RATIO)]
        o_ref[0] = _compress_tokens(lhs[0], lhs[1], w1_ref[...], w2_ref[...]).astype(o_ref.dtype)


def _compress_rows(k_rows, v_rows, b, t, wk, wv):
    n_sub = t // D_CMP
    view = lambda a: a.reshape(b, n_sub, SUB_WIDTH)
    seq = pl.BlockSpec((1, n_sub, SUB_WIDTH), lambda bi: (bi, 0, 0))
    full = lambda a: pl.BlockSpec(a.shape, lambda bi: (0,) * a.ndim)
    out = pl.BlockSpec((1, n_sub, KV_WIDTH), lambda bi: (bi, 0, 0))
    return pl.pallas_call(
        _compress_rows_kernel,
        out_shape=[jax.ShapeDtypeStruct((b, n_sub, KV_WIDTH), jnp.bfloat16)] * 2,
        grid=(b,),
        in_specs=[seq, seq] + [full(a) for a in (*wk, *wv)],
        out_specs=[out, out],
        compiler_params=pltpu.CompilerParams(dimension_semantics=("arbitrary",),
                                             vmem_limit_bytes=V7X_VMEM_LIMIT_BYTES),
        name="compress_rows",
    )(view(k_rows), view(v_rows), *wk, *wv)


def _compress_pages_kernel(pt_ref, *refs):
    n_pg = CMP_PAGES_PER_STEP
    k_pages, v_pages = refs[:n_pg], refs[n_pg:2 * n_pg]
    w1k_ref, pk_ref, w2k_ref, w1v_ref, pv_ref, w2v_ref, kc_ref, vc_ref, lhs_ref = refs[2 * n_pg:]
    step = pl.program_id(1)
    for c, (pages, p_ref) in enumerate(((k_pages, pk_ref), (v_pages, pv_ref))):
        for j in range(0, n_pg, 2):
            rows = pl.ds(pl.multiple_of(step * n_pg * PAGE_SUBS + j * PAGE_SUBS, 2 * PAGE_SUBS), 2 * PAGE_SUBS)
            for l in range(D_CMP):
                x = jnp.concatenate([pages[j][0, pl.ds(l, PAGE_SUBS, stride=D_CMP), :],
                                     pages[j + 1][0, pl.ds(l, PAGE_SUBS, stride=D_CMP), :]], axis=0)
                lanes = slice(l * KV_WIDTH, (l + 1) * KV_WIDTH)
                for r in range(CMP_RATIO):
                    lhs_ref[c, r, rows, lanes] = (x + p_ref[r, :, lanes]).astype(jnp.bfloat16)

    @pl.when(step == pl.num_programs(1) - 1)
    def _():
        for c, (w1_ref, w2_ref, o_ref) in enumerate(((w1k_ref, w2k_ref, kc_ref), (w1v_ref, w2v_ref, vc_ref))):
            o_ref[0] = _compress_tokens(lhs_ref[c, 0], lhs_ref[c, 1], w1_ref[...], w2_ref[...]).astype(o_ref.dtype)


def _compress_pages(cache_k, cache_v, page_table, wk, wv):
    b, n_pages = page_table.shape
    n_pool = cache_k.shape[0]
    n_sub = n_pages * PAGE_SUBS
    n_pg = CMP_PAGES_PER_STEP
    view = lambda c: c.reshape(n_pool, PAGE_SIZE, KV_WIDTH)
    page = lambda j: pl.BlockSpec(
        (1, PAGE_SIZE, KV_WIDTH),
        lambda bi, si, pt: (jnp.clip(pt[jnp.minimum(bi, b - 1), jnp.minimum(si, n_pages // n_pg - 1) * n_pg + j],
                                     0, n_pool - 1), 0, 0))
    full = lambda a: pl.BlockSpec(a.shape, lambda bi, si, pt: (0,) * a.ndim)
    out = pl.BlockSpec((1, n_sub, KV_WIDTH), lambda bi, si, pt: (bi, 0, 0))
    return pl.pallas_call(
        _compress_pages_kernel,
        out_shape=[jax.ShapeDtypeStruct((b, n_sub, KV_WIDTH), jnp.bfloat16)] * 2,
        grid_spec=pltpu.PrefetchScalarGridSpec(
            num_scalar_prefetch=1,
            grid=(b, n_pages // n_pg),
            in_specs=[page(j) for j in range(n_pg)] * 2 + [full(a) for a in (*wk, *wv)],
            out_specs=[out, out],
            scratch_shapes=[pltpu.VMEM((2, CMP_RATIO, n_sub, SUB_WIDTH), jnp.bfloat16)]),
        compiler_params=pltpu.CompilerParams(dimension_semantics=("arbitrary", "arbitrary"),
                                             vmem_limit_bytes=V7X_VMEM_LIMIT_BYTES),
        name="compress_pages",
    )(page_table, *([view(cache_k)] * n_pg), *([view(cache_v)] * n_pg), *wk, *wv)


N_SEL_DEC = PAST_LEN // L_SEL + 1
N_SEL_DEC_PAD = 256
CUR_BLOCK = PAST_LEN // L_SEL


def _nsa_decode_cmp_kernel(q_ref, kc_ref, vc_ref, bias_ref, oc_ref, imp_ref):
    f32, bf16 = jnp.float32, jnp.bfloat16
    n_c = kc_ref.shape[1]
    s = _dot_nt(q_ref[0], kc_ref[0]) + bias_ref[...]
    mx = jnp.max(s, axis=-1, keepdims=True)
    mx = jnp.where(mx < 0.1 * NEG_BIG, 0.0, mx)
    p = jnp.exp(s - mx)
    p = p / jnp.maximum(jnp.sum(p, axis=-1, keepdims=True), 1e-30)
    pb = p.astype(bf16)
    oc_ref[0] = jnp.dot(pb, vc_ref[0], preferred_element_type=f32)
    ov_c = lax.broadcasted_iota(jnp.int32, (n_c, N_SEL_DEC_PAD), 0)
    ov_s = lax.broadcasted_iota(jnp.int32, (n_c, N_SEL_DEC_PAD), 1)
    overlap = jnp.where((ov_c >= 4 * ov_s - 1) & (ov_c <= 4 * ov_s + 3), 1.0, 0.0).astype(bf16)
    imp_h = jnp.dot(pb, overlap, preferred_element_type=f32)
    head = lax.broadcasted_iota(jnp.int32, imp_h.shape, 0)
    imp_ref[0] = jnp.concatenate(
        [jnp.sum(jnp.where((head >= g * GROUP) & (head < (g + 1) * GROUP), imp_h, 0.0), axis=0, keepdims=True)
         for g in range(N_KV)], axis=1)


def _nsa_decode_topk_kernel(imp_ref, idx_ref, val_ref):
    f32 = jnp.float32
    nb = imp_ref.shape[0]
    s_io = lax.broadcasted_iota(jnp.int32, (N_SEL_DEC_PAD, nb), 0)
    visible = s_io < N_SEL_DEC
    forced = (s_io == 0) | (s_io == CUR_BLOCK) | (s_io == CUR_BLOCK - 1)
    tri_r = lax.broadcasted_iota(jnp.int32, (N_SEL_DEC_PAD, N_SEL_DEC_PAD), 0)
    tri_c = lax.broadcasted_iota(jnp.int32, (N_SEL_DEC_PAD, N_SEL_DEC_PAD), 1)
    before = jnp.where(tri_c < tri_r, 1.0, 0.0).astype(jnp.bfloat16)
    for g in range(N_KV):
        x = imp_ref[:, g * N_SEL_DEC_PAD:(g + 1) * N_SEL_DEC_PAD]
        xt = jnp.concatenate([x[:, i * LANES:(i + 1) * LANES].T for i in range(N_SEL_DEC_PAD // LANES)], axis=0)
        val = jnp.where(visible, xt + jnp.where(forced, FORCE_SCORE, 0.0), -jnp.inf)
        val_ref[...] = val

        def rank_body(s, rank):
            other = val_ref[pl.ds(s, 1), :]
            beats = (other > val) | ((other == val) & (s_io > s))
            return rank + jnp.where(beats, 1, 0)

        rank = lax.fori_loop(0, N_SEL_DEC, rank_body, jnp.zeros(val.shape, jnp.int32))
        sel = (rank < K_SEL) & visible
        n_before = jnp.dot(before, jnp.where(sel, 1.0, 0.0).astype(jnp.bfloat16), preferred_element_type=f32)
        for k in range(K_SEL):
            hit = sel & (n_before == float(k))
            idx_ref[g, pl.ds(k, 1), :] = jnp.sum(jnp.where(hit, s_io.astype(f32), 0.0), axis=0,
                                                 keepdims=True).astype(jnp.int32)


def _nsa_decode_sel_kernel(pt_ref, ix_ref, *refs):
    f32, bf16 = jnp.float32, jnp.bfloat16
    n_blk = N_KV * K_SEL
    k_blocks, v_blocks = refs[:n_blk], refs[n_blk:2 * n_blk]
    (q_ref, oc_ref, gl_ref, ksn_ref, vsn_ref, kwn_ref, vwn_ref, kwin_ref, vwin_ref, near_ref, b0_ref, bw_ref,
     o_ref, kwo_ref, vwo_ref) = refs[2 * n_blk:]
    b = pl.program_id(0)
    q = q_ref[0]
    qf = q.astype(f32)
    round_bf = lambda a: a.astype(bf16).astype(f32)
    group1 = lax.broadcasted_iota(jnp.int32, (N_HEADS, LANES), 0) >= GROUP
    second_half = lax.broadcasted_iota(jnp.int32, (N_HEADS, LANES), 1) >= L_SEL
    neg_tile = jnp.full((N_HEADS, LANES), NEG_BIG, f32)
    no_rows = jnp.zeros((HEAD_DIM, PAGE_SIZE), bf16)

    def block_bias(s):
        tile = jnp.where(s == CUR_BLOCK, neg_tile,
                         jnp.where(s == CUR_BLOCK - 1, near_ref[1],
                                   jnp.where(s == CUR_BLOCK - 2, near_ref[0], near_ref[2])))
        return jnp.where(second_half == ((s & 1) == 1), tile, neg_tile)

    def group_rows(x, g):
        x = x.astype(bf16)
        return jnp.concatenate([x, no_rows] if g == 0 else [no_rows, x], axis=0)

    scores, values, has_new = [], [], []
    for g in range(N_KV):
        ids = [ix_ref[b, g * K_SEL + k] for k in range(K_SEL)]
        kt = jnp.concatenate([group_rows(k_blocks[g * K_SEL + k][0], g) for k in range(K_SEL)], axis=1)
        values.append(jnp.concatenate([group_rows(v_blocks[g * K_SEL + k][0], g) for k in range(K_SEL)], axis=1))
        bias = jnp.concatenate([block_bias(ids[k]) for k in range(K_SEL)], axis=1)
        scores.append(jnp.dot(q, kt, preferred_element_type=f32) + bias)
        flag = ids[0] == CUR_BLOCK
        for k in range(1, K_SEL):
            flag = flag | (ids[k] == CUR_BLOCK)
        has_new.append(jnp.where(flag, 0.0, NEG_BIG))
    s = jnp.where(group1[:, :1], scores[1], scores[0])
    s_new = jnp.sum(qf * round_bf(ksn_ref[0]), axis=-1, keepdims=True) + b0_ref[...]
    s_new = s_new + jnp.where(group1, has_new[1], has_new[0])
    m = jnp.maximum(jnp.max(s, axis=-1, keepdims=True), s_new)
    p = jnp.exp(s - m[:, :1])
    p_new = jnp.exp(s_new - m)
    l = jnp.sum(p, axis=-1, keepdims=True) + p_new
    pb = p.astype(bf16)
    o_sel = jnp.where(group1, _dot_nt(pb, values[1]), _dot_nt(pb, values[0]))
    o_sel = (o_sel + round_bf(p_new) * round_bf(vsn_ref[0])) / l

    kwin, vwin = kwin_ref[0], vwin_ref[0]
    s = jnp.dot(q, kwin.astype(bf16), preferred_element_type=f32) + bw_ref[...]
    s_new = jnp.sum(qf * round_bf(kwn_ref[0]), axis=-1, keepdims=True) + b0_ref[...]
    m = jnp.maximum(jnp.max(s, axis=-1, keepdims=True), s_new)
    p = jnp.exp(s - m[:, :1])
    p_new = jnp.exp(s_new - m)
    l = jnp.sum(p, axis=-1, keepdims=True) + p_new
    o_win = _dot_nt(p.astype(bf16), vwin.astype(bf16))
    o_win = (o_win + round_bf(p_new) * round_bf(vwn_ref[0])) / l

    gates = jax.nn.sigmoid(gl_ref[0])
    o_ref[0] = (oc_ref[0] * gates[0] + o_sel * gates[1] + o_win * gates[2]).astype(bf16)

    wb = kwin.shape[1]
    last = lax.broadcasted_iota(jnp.int32, kwin.shape, 1) == wb - 1
    as_column = lambda row: jnp.broadcast_to(row, (KV_WIDTH, KV_WIDTH)).T[:, :1]
    kwo_ref[0] = jnp.where(last, as_column(kwn_ref[0]), pltpu.roll(kwin, wb - 1, 1))
    vwo_ref[0] = jnp.where(last, as_column(vwn_ref[0]), pltpu.roll(vwin, wb - 1, 1))


def _nsa_decode(qpad, small, kc, vc, ks_new, vs_new, kw_new, vw_new, cache_k_sel, cache_v_sel,
                cache_k_win, cache_v_win, page_table, rel_bias):
    f32 = jnp.float32
    b = qpad.shape[0]
    n_c = kc.shape[1]
    n_pool = cache_k_sel.shape[0]
    wb = cache_k_win.shape[1]
    tab = rel_bias.astype(f32)[_bucket_table()].T
    dist_c = PAST_LEN - (L_CMP - 1) - D_CMP * np.arange(n_c)
    bias_c = jnp.where(jnp.asarray(dist_c >= 0)[None, :], tab[:, np.clip(dist_c, 0, MAX_DIST)], NEG_BIG)
    j = np.arange(LANES) % L_SEL
    near = jnp.stack([tab[:, 2 * L_SEL - j], tab[:, L_SEL - j],
                      jnp.broadcast_to(tab[:, MAX_DIST:], (N_HEADS, LANES))])
    bias0 = jnp.broadcast_to(tab[:, :1], (N_HEADS, LANES))
    bias_w = tab[:, np.minimum(wb - np.arange(wb), MAX_DIST)]
    q3 = qpad.reshape(b, N_HEADS, LANES)

    per_seq = lambda *shape: pl.BlockSpec((1,) + shape, lambda bi, *_: (bi,) + (0,) * len(shape))
    full = lambda a: pl.BlockSpec(a.shape, lambda bi, *_: (0,) * a.ndim)
    oc, imp = pl.pallas_call(
        _nsa_decode_cmp_kernel,
        out_shape=[jax.ShapeDtypeStruct((b, N_HEADS, LANES), f32),
                   jax.ShapeDtypeStruct((b, 1, N_KV * N_SEL_DEC_PAD), f32)],
        grid=(b,),
        in_specs=[per_seq(N_HEADS, LANES), per_seq(n_c, KV_WIDTH), per_seq(n_c, KV_WIDTH), full(bias_c)],
        out_specs=[per_seq(N_HEADS, LANES), per_seq(1, N_KV * N_SEL_DEC_PAD)],
        compiler_params=pltpu.CompilerParams(dimension_semantics=("arbitrary",)),
        name="nsa_decode_cmp",
    )(q3, kc, vc, bias_c)

    idx = pl.pallas_call(
        _nsa_decode_topk_kernel,
        out_shape=jax.ShapeDtypeStruct((N_KV, K_SEL, b), jnp.int32),
        scratch_shapes=[pltpu.VMEM((N_SEL_DEC_PAD, b), f32)],
        name="nsa_decode_topk",
    )(imp.reshape(b, N_KV * N_SEL_DEC_PAD))
    idx = idx.reshape(N_KV * K_SEL, b).T

    halves = PAGE_SIZE // L_SEL

    def sel_block(i):
        def index_map(bi, pt, ix):
            bs = jnp.minimum(bi, b - 1)
            s = jnp.clip(ix[bs, i], 0, CUR_BLOCK - 1)
            return (jnp.clip(pt[bs, s // halves], 0, n_pool - 1) * N_KV + i // K_SEL, 0, 0)
        return pl.BlockSpec((1, HEAD_DIM, PAGE_SIZE), index_map)

    keys_minor = lambda c: c.transpose(0, 2, 3, 1)
    page_groups = lambda c: keys_minor(c).reshape(n_pool * N_KV, HEAD_DIM, PAGE_SIZE)
    win_t = lambda c: keys_minor(c).reshape(b, KV_WIDTH, wb)
    glog = jnp.broadcast_to(small[:, :3 * N_HEADS].reshape(b, N_HEADS, 3).transpose(0, 2, 1)[..., None],
                            (b, 3, N_HEADS, LANES))
    new_row = lambda a: a.reshape(b, 1, KV_WIDTH)
    n_blk = N_KV * K_SEL
    o_pad, k_win, v_win = pl.pallas_call(
        _nsa_decode_sel_kernel,
        out_shape=[jax.ShapeDtypeStruct((b, N_HEADS, LANES), jnp.bfloat16),
                   jax.ShapeDtypeStruct((b, KV_WIDTH, wb), f32), jax.ShapeDtypeStruct((b, KV_WIDTH, wb), f32)],
        grid_spec=pltpu.PrefetchScalarGridSpec(
            num_scalar_prefetch=2,
            grid=(b,),
            in_specs=[sel_block(i) for i in range(n_blk)] * 2
                     + [per_seq(N_HEADS, LANES), per_seq(N_HEADS, LANES), per_seq(3, N_HEADS, LANES)]
                     + [per_seq(1, KV_WIDTH)] * 4 + [per_seq(KV_WIDTH, wb)] * 2
                     + [full(near), full(bias0), full(bias_w)],
            out_specs=[per_seq(N_HEADS, LANES), per_seq(KV_WIDTH, wb), per_seq(KV_WIDTH, wb)]),
        compiler_params=pltpu.CompilerParams(dimension_semantics=("arbitrary",)),
        name="nsa_decode_sel",
    )(page_table, idx, *([page_groups(cache_k_sel)] * n_blk), *([page_groups(cache_v_sel)] * n_blk),
      q3, oc, glog, new_row(ks_new), new_row(vs_new), new_row(kw_new), new_row(vw_new),
      win_t(cache_k_win), win_t(cache_v_win), near, bias0, bias_w)
    rows_major = lambda a: a.reshape(b, N_KV, HEAD_DIM, wb).transpose(0, 3, 1, 2)
    return o_pad.reshape(b, QPAD_WIDTH), rows_major(k_win), rows_major(v_win)


MLSTM_CHUNK = 128
MLSTM_DEC_TILE = 8
CONV_TAIL = 8


def _mlstm_norm_gate(h, zo, norm_g):
    return h * lax.rsqrt(jnp.mean(h * h, axis=-1, keepdims=True) + EPS) * norm_g * jax.nn.sigmoid(zo)


def _mlstm_prompt_kernel(zqk_ref, zv_ref, zo_ref, sm_ref, cw_ref, cb_ref, wq_ref, wk_ref, gb_ref, ng_ref,
                         o_ref, co_ref, no_ref, mo_ref, xbuf_ref, c_ref, n_ref, m_ref):
    f32, bf16 = jnp.float32, jnp.bfloat16
    L = MLSTM_CHUNK
    c = pl.program_id(1)

    @pl.when(c == 0)
    def _():
        xbuf_ref[:CONV_TAIL] = jnp.zeros((CONV_TAIL, M_WIDTH), f32)
        c_ref[...] = jnp.zeros(c_ref.shape, f32)
        n_ref[...] = jnp.zeros(n_ref.shape, f32)
        m_ref[...] = jnp.zeros(m_ref.shape, f32)

    x = zqk_ref[...]
    xbuf_ref[CONV_TAIL:] = x
    conv = cb_ref[...]
    for j in range(CONV_W):
        conv = conv + xbuf_ref[pl.ds(CONV_TAIL - (CONV_W - 1) + j, L), :] * cw_ref[j:j + 1, :]
    xbuf_ref[:CONV_TAIL] = x[L - CONV_TAIL:]
    a = jax.nn.silu(conv).astype(bf16)

    t_io = lax.broadcasted_iota(jnp.int32, (L, L), 0)
    s_io = lax.broadcasted_iota(jnp.int32, (L, L), 1)
    causal = t_io >= s_io
    sm = sm_ref[...]
    for h in range(M_HEADS):
        hs = slice(h * M_DH, (h + 1) * M_DH)
        q = jnp.dot(a[:, hs], wq_ref[h], preferred_element_type=f32)
        k = jnp.dot(a[:, hs], wk_ref[h], preferred_element_type=f32) * (M_DH ** -0.5)
        v = zv_ref[:, hs]
        qb, kb, vb = q.astype(bf16), k.astype(bf16), v.astype(bf16)
        col = 3 * N_HEADS + h
        ig = jnp.broadcast_to(sm[:, col:col + 1], (L, L)) + gb_ref[0:1, hs]
        lf = jax.nn.log_sigmoid(jnp.broadcast_to(sm[:, col + M_HEADS:col + M_HEADS + 1], (L, L)) + gb_ref[1:2, hs])
        bcum = lf
        sh = 1
        while sh < L:
            bcum = bcum + jnp.where(t_io >= sh, pltpu.roll(bcum, sh, 0), 0.0)
            sh *= 2
        m_old = m_ref[h:h + 1, :]
        c_old = c_ref[h]
        n_old = n_ref[h:h + 1, :]
        dmat = jnp.where(causal, bcum - bcum.T + ig.T, -jnp.inf)
        inter = bcum + m_old
        m_t = jnp.maximum(jnp.max(dmat, axis=-1, keepdims=True), inter)
        sc = _dot_nt(qb, kb) * jnp.exp(dmat - m_t)
        decay = jnp.exp(inter - m_t)
        num = decay * _dot_nt(qb, c_old.astype(bf16)) + jnp.dot(sc.astype(bf16), vb, preferred_element_type=f32)
        den = decay * jnp.sum(q * n_old, axis=-1, keepdims=True) + jnp.sum(sc, axis=-1, keepdims=True)
        hh = num / jnp.maximum(jnp.abs(den), jnp.exp(-m_t))
        o_ref[:, hs] = _mlstm_norm_gate(hh, zo_ref[:, hs], ng_ref[:, hs])

        b_last = bcum[L - 1:L, :]
        w_log = b_last - bcum + ig
        m_new = jnp.maximum(b_last + m_old, jnp.max(w_log, axis=0, keepdims=True))
        w = jnp.exp(w_log - m_new)
        carry = jnp.exp(b_last + m_old - m_new)
        c_ref[h] = carry * c_old + jnp.dot((w * v).T.astype(bf16), kb, preferred_element_type=f32)
        n_ref[h:h + 1, :] = carry * n_old + jnp.sum(w * k, axis=0, keepdims=True)
        m_ref[h:h + 1, :] = m_new

    @pl.when(c == pl.num_programs(1) - 1)
    def _():
        co_ref[0] = c_ref[...]
        no_ref[0] = n_ref[...]
        mo_ref[0] = m_ref[...]


def _mlstm_weights(conv_w, conv_b, wq, wk, gate_bias, norm_g):
    gb = jnp.repeat(gate_bias.astype(jnp.float32), M_DH, axis=1)
    return (conv_w, conv_b.reshape(1, M_WIDTH), wq.astype(jnp.bfloat16), wk.astype(jnp.bfloat16), gb,
            norm_g.reshape(1, M_WIDTH))


def _mlstm_prompt(zmqk, zmv, zmo, small, weights, b, t):
    L = MLSTM_CHUNK
    n_ch = t // L
    rows = lambda width: pl.BlockSpec((L, width), lambda bi, ci: (bi * n_ch + ci, 0))
    full = lambda a: pl.BlockSpec(a.shape, lambda bi, ci: (0,) * a.ndim)
    state = lambda *shape: pl.BlockSpec((1,) + shape, lambda bi, ci: (bi,) + (0,) * len(shape))
    f32 = jnp.float32
    return pl.pallas_call(
        _mlstm_prompt_kernel,
        out_shape=[jax.ShapeDtypeStruct((b * t, M_WIDTH), f32), jax.ShapeDtypeStruct((b, M_HEADS, M_DH, M_DH), f32),
                   jax.ShapeDtypeStruct((b, 8, M_DH), f32), jax.ShapeDtypeStruct((b, 8, M_DH), f32)],
        grid=(b, n_ch),
        in_specs=[rows(M_WIDTH), rows(M_WIDTH), rows(M_WIDTH), rows(LANES)] + [full(a) for a in weights],
        out_specs=[rows(M_WIDTH), state(M_HEADS, M_DH, M_DH), state(8, M_DH), state(8, M_DH)],
        scratch_shapes=[pltpu.VMEM((CONV_TAIL + L, M_WIDTH), f32), pltpu.VMEM((M_HEADS, M_DH, M_DH), f32),
                        pltpu.VMEM((8, M_DH), f32), pltpu.VMEM((8, M_DH), f32)],
        compiler_params=pltpu.CompilerParams(dimension_semantics=("arbitrary", "arbitrary")),
        name="mlstm_prompt",
    )(zmqk, zmv, zmo, small, *weights)


def _mlstm_decode_kernel(zqk_ref, conv_ref, zv_ref, zo_ref, sm_ref, c_ref, n_ref, m_ref,
                         cw_ref, cb_ref, wq_ref, wk_ref, gb_ref, ng_ref, o_ref, co_ref, no_ref, mo_ref):
    f32, bf16 = jnp.float32, jnp.bfloat16
    nt = MLSTM_DEC_TILE
    conv = cb_ref[...] + zqk_ref[...] * cw_ref[CONV_W - 1:CONV_W, :]
    for j in range(CONV_W - 1):
        conv = conv + conv_ref[j] * cw_ref[j:j + 1, :]
    a = jax.nn.silu(conv).astype(bf16)
    sm = sm_ref[...]
    lane = lax.broadcasted_iota(jnp.int32, (M_DH, M_DH), 1)
    pad_rows = jnp.zeros((M_DH - nt, M_DH), f32)
    for h in range(M_HEADS):
        hs = slice(h * M_DH, (h + 1) * M_DH)
        q = jnp.dot(a[:, hs], wq_ref[h], preferred_element_type=f32)
        k = jnp.dot(a[:, hs], wk_ref[h], preferred_element_type=f32) * (M_DH ** -0.5)
        v = zv_ref[:, hs]
        col = 3 * N_HEADS + h
        ig = jnp.broadcast_to(sm[:, col:col + 1], (nt, M_DH)) + gb_ref[0:1, hs]
        lf = jax.nn.log_sigmoid(jnp.broadcast_to(sm[:, col + M_HEADS:col + M_HEADS + 1], (nt, M_DH)) + gb_ref[1:2, hs])
        m_old = m_ref[:, hs]
        n_old = n_ref[:, hs]
        m_new = jnp.maximum(lf + m_old, ig)
        decay = jnp.exp(lf + m_old - m_new)
        w = jnp.exp(ig - m_new)
        sc = jnp.sum(q * k, axis=-1, keepdims=True) * w
        den = decay * jnp.sum(n_old * q, axis=-1, keepdims=True) + sc
        wv_t = jnp.concatenate([w * v, pad_rows], axis=0).T
        cq_t = jnp.zeros((M_DH, M_DH), f32)
        for i in range(nt):
            c_old = c_ref[i, h]
            cq = jnp.sum(c_old * q[i:i + 1, :], axis=-1, keepdims=True)
            cq_t = jnp.where(lane == i, cq, cq_t)
            co_ref[i, h] = decay[i:i + 1, :] * c_old + wv_t[:, i:i + 1] * k[i:i + 1, :]
        num = decay * cq_t.T[:nt] + sc * v
        hh = num / jnp.maximum(jnp.abs(den), jnp.exp(-m_new))
        o_ref[:, hs] = _mlstm_norm_gate(hh, zo_ref[:, hs], ng_ref[:, hs])
        no_ref[:, hs] = decay * n_old + w * k
        mo_ref[:, hs] = m_new


def _mlstm_decode(zmqk, zmv, zmo, small, state_c, state_n, state_m, state_conv, weights):
    b = zmqk.shape[0]
    nt = MLSTM_DEC_TILE
    f32 = jnp.float32
    rows = lambda width: pl.BlockSpec((nt, width), lambda i: (i, 0))
    full = lambda a: pl.BlockSpec(a.shape, lambda i: (0,) * a.ndim)
    cspec = pl.BlockSpec((nt, M_HEADS, M_DH, M_DH), lambda i: (i, 0, 0, 0))
    conv_t = state_conv.transpose(1, 0, 2)
    m_rep = jnp.repeat(state_m, M_DH, axis=1)
    return pl.pallas_call(
        _mlstm_decode_kernel,
        out_shape=[jax.ShapeDtypeStruct((b, M_WIDTH), f32), jax.ShapeDtypeStruct(state_c.shape, f32),
                   jax.ShapeDtypeStruct((b, M_WIDTH), f32), jax.ShapeDtypeStruct((b, M_WIDTH), f32)],
        grid=(b // nt,),
        in_specs=[rows(M_WIDTH), pl.BlockSpec((CONV_W - 1, nt, M_WIDTH), lambda i: (0, i, 0)), rows(M_WIDTH),
                  rows(M_WIDTH), rows(LANES), cspec, rows(M_WIDTH), rows(M_WIDTH)] + [full(a) for a in weights],
        out_specs=[rows(M_WIDTH), cspec, rows(M_WIDTH), rows(M_WIDTH)],
        compiler_params=pltpu.CompilerParams(dimension_semantics=("arbitrary",)),
        name="mlstm_decode",
    )(zmqk, conv_t, zmv, zmo, small, state_c, state_n.reshape(b, M_WIDTH), m_rep, *weights)


def _merge_kernel(x_ref, on_ref, om_ref, zm_ref, wn_ref, wm_ref, wo_ref, o_ref):
    zm = zm_ref[...]
    g_a = jax.nn.sigmoid(zm[:, :D_MODEL])
    g_b = jax.nn.sigmoid(zm[:, D_MODEL:])
    ya = jnp.dot(on_ref[...], wn_ref[...], preferred_element_type=jnp.float32)
    yb = jnp.dot(om_ref[...].astype(jnp.bfloat16), wm_ref[...], preferred_element_type=jnp.float32)
    y = (g_a * ya + g_b * yb).astype(jnp.bfloat16)
    o_ref[...] = x_ref[...] + jnp.dot(y, wo_ref[...], preferred_element_type=jnp.float32)


def _merge(x, o_nsa, o_mlstm, zmerge, w_proj_nsa, w_proj_mlstm, w_out):
    n = x.shape[0]
    tm = min(PROJ_TOKEN_TILE, n)
    row = lambda width: pl.BlockSpec((tm, width), lambda i: (i, 0))
    full = lambda a: pl.BlockSpec(a.shape, lambda i: (0, 0))
    return pl.pallas_call(
        _merge_kernel,
        out_shape=jax.ShapeDtypeStruct((n, D_MODEL), jnp.float32),
        grid=(n // tm,),
        in_specs=[row(D_MODEL), row(o_nsa.shape[1]), row(M_WIDTH), row(2 * D_MODEL),
                  full(w_proj_nsa), full(w_proj_mlstm), full(w_out)],
        out_specs=row(D_MODEL),
        compiler_params=pltpu.CompilerParams(dimension_semantics=("arbitrary",),
                                             vmem_limit_bytes=V7X_VMEM_LIMIT_BYTES),
        name="merge_out_proj",
    )(x, o_nsa, o_mlstm, zmerge, w_proj_nsa, w_proj_mlstm, w_out)


def _kv_rows(a, b, t):
    return a.reshape(b, t, N_KV, HEAD_DIM)


def _mix_prompt(proj, b, t, cmp_k, cmp_v, mlstm_w, rel_bias):
    qpad, kvb, kc_rows, vc_rows, ks_rows, vs_rows, kw_rows, vw_rows, zmqk, zmv, zmo, zmerge, small = proj
    kc, vc = _compress_rows(kc_rows, vc_rows, b, t, cmp_k, cmp_v)
    o_nsa = _nsa_prompt(qpad, small, kvb, kc, vc.transpose(0, 2, 1), rel_bias, b, t)
    o_mlstm, c_f, n_f, m_f = _mlstm_prompt(zmqk, zmv, zmo, small, mlstm_w, b, t)
    n_keep = min(WINDOW, t)
    states = (_kv_rows(kc_rows, b, t), _kv_rows(vc_rows, b, t), _kv_rows(ks_rows, b, t), _kv_rows(vs_rows, b, t),
              _kv_rows(kw_rows, b, t)[:, t - n_keep:], _kv_rows(vw_rows, b, t)[:, t - n_keep:],
              c_f, n_f[:, :M_HEADS], m_f[:, :M_HEADS, 0], zmqk.reshape(b, t, M_WIDTH)[:, t - (CONV_W - 1):])
    return o_nsa, o_mlstm, zmerge, states


def _mix_decode(proj, b, caches, mlstm_state, page_table, cmp_k, cmp_v, mlstm_w, rel_bias):
    cache_k_cmp, cache_v_cmp, cache_k_sel, cache_v_sel, cache_k_win, cache_v_win = caches
    state_c, state_n, state_m, state_conv = mlstm_state
    qpad, _, kc_new, vc_new, ks_new, vs_new, kw_new, vw_new, zmqk, zmv, zmo, zmerge, small = proj
    kc, vc = _compress_pages(cache_k_cmp, cache_v_cmp, page_table, cmp_k, cmp_v)
    o_nsa, k_win, v_win = _nsa_decode(qpad, small, kc, vc, ks_new, vs_new, kw_new, vw_new, cache_k_sel, cache_v_sel,
                                      cache_k_win, cache_v_win, page_table, rel_bias)
    o_mlstm, c_n, n_n, m_rep = _mlstm_decode(zmqk, zmv, zmo, small, state_c, state_n, state_m, state_conv, mlstm_w)
    wb = cache_k_win.shape[1]
    conv_new = jnp.concatenate([state_conv[:, 1:], zmqk[:, None, :]], axis=1)
    states = (_kv_rows(kc_new, b, 1), _kv_rows(vc_new, b, 1), _kv_rows(ks_new, b, 1), _kv_rows(vs_new, b, 1),
              _kv_rows(k_win, b, wb), _kv_rows(v_win, b, wb),
              c_n, n_n.reshape(b, M_HEADS, M_DH), m_rep[:, ::M_DH], conv_new)
    return o_nsa, o_mlstm, zmerge, states


def kernel(x_prompt, x_sample, cache_k_cmp, cache_v_cmp, cache_k_sel, cache_v_sel, cache_k_win, cache_v_win, state_mlstm_C, state_mlstm_n, state_mlstm_m, state_mlstm_conv, page_table, norm_ffn1, ffn1_w_in, ffn1_w_out, norm_mix, w_mix_in, cmp_pos_k, cmp_pos_v, cmp_phi_k1, cmp_phi_k2, cmp_phi_v1, cmp_phi_v2, rel_bias, mlstm_conv_w, mlstm_conv_b, mlstm_wq, mlstm_wk, mlstm_gate_bias, mlstm_norm, w_proj_nsa, w_proj_mlstm, w_out, norm_ffn2, ffn2_w_in, ffn2_w_out, norm_final):
    assert x_sample.shape[1] == DEC_SEQ == 1
    bf = lambda w: w.astype(jnp.bfloat16)
    w1i, w1o, w2i, w2o = bf(ffn1_w_in), bf(ffn1_w_out), bf(ffn2_w_in), bf(ffn2_w_out)
    w_mix = _regroup_mix_weight(w_mix_in)
    w_nsa_out, wm, wo = bf(_pad_nsa_out_weight(w_proj_nsa)), bf(w_proj_mlstm), bf(w_out)
    cmp_k = _compress_weights(cmp_pos_k, cmp_phi_k1, cmp_phi_k2)
    cmp_v = _compress_weights(cmp_pos_v, cmp_phi_v1, cmp_phi_v2)
    mlstm_w = _mlstm_weights(mlstm_conv_w, mlstm_conv_b, mlstm_wq, mlstm_wk, mlstm_gate_bias, mlstm_norm)

    def layer(x3, mix_fn):
        b, t, _ = x3.shape
        x = x3.reshape(b * t, D_MODEL)
        x1 = _ffn(x, norm_ffn1, w1i, w1o, norm_final, final_norm=False)
        proj = _project(x1, norm_mix, w_mix)
        o_nsa, o_mlstm, zmerge, states = mix_fn(proj, b, t)
        x2 = _merge(x1, o_nsa, o_mlstm, zmerge, w_nsa_out, wm, wo)
        y = _ffn(x2, norm_ffn2, w2i, w2o, norm_final, final_norm=True)
        return y.reshape(b, t, D_MODEL), states

    y_prompt, st_p = layer(x_prompt, lambda proj, b, t: _mix_prompt(proj, b, t, cmp_k, cmp_v, mlstm_w, rel_bias))
    y_sample, st_s = layer(x_sample, lambda proj, b, t: _mix_decode(
        proj, b, (cache_k_cmp, cache_v_cmp, cache_k_sel, cache_v_sel, cache_k_win, cache_v_win),
        (state_mlstm_C, state_mlstm_n, state_mlstm_m, state_mlstm_conv), page_table, cmp_k, cmp_v, mlstm_w, rel_bias))
    k_cmp_p, v_cmp_p, k_sel_p, v_sel_p, k_win_p, v_win_p, C_p, n_p, m_p, conv_p = st_p
    k_cmp_s, v_cmp_s, k_sel_s, v_sel_s, k_win_s, v_win_s, C_s, n_s, m_s, conv_s = st_s
    return (y_prompt, y_sample, k_cmp_p, k_cmp_s, v_cmp_p, v_cmp_s, k_sel_p, k_sel_s, v_sel_p, v_sel_s,
            k_win_p, k_win_s, v_win_p, v_win_s, C_p, C_s, n_p, n_s, m_p, m_s, conv_p, conv_s)
```

```python
import math
from functools import partial

import numpy as np
import jax
import jax.numpy as jnp
from jax import lax
from jax.experimental import pallas as pl
from jax.experimental.pallas import tpu as pltpu

D_MODEL = 1024
SEQ = 8192
DEC_SEQ = 1
PAST_LEN = 8192
PAGE_SIZE = 128
N_HEADS = 8
N_KV = 2
GROUP = N_HEADS // N_KV
HEAD_DIM = 64
NSA_WIDTH = N_HEADS * HEAD_DIM
KV_WIDTH = N_KV * HEAD_DIM
L_CMP = 32
D_CMP = 16
CMP_RATIO = L_CMP // D_CMP
L_SEL = 64
K_SEL = 16
WINDOW = 512
Q_BLOCK = 128
FORCE_SCORE = 1000.0
N_BUCKETS = 32
MAX_DIST = 128
M_HEADS = 4
M_DH = 128
M_WIDTH = M_HEADS * M_DH
CONV_W = 4
M_CHUNK = 64
D_FF = 2816
EPS = 1e-6

V7X_VMEM_LIMIT_BYTES = 56 * 1024 * 1024
LANES = 128

QPAD_WIDTH = N_HEADS * LANES
N_SEL_PAD = 128
FAR_TILES = 4
NEG_BIG = -1e30
SEL_OFF = -32768.0

FFN_TOKEN_TILE = 512
FFN_FF_TILE = 1408
PROJ_TOKEN_TILE = 512

_MIX_SIZES = (NSA_WIDTH, KV_WIDTH, KV_WIDTH, KV_WIDTH, KV_WIDTH, KV_WIDTH, KV_WIDTH, 3 * N_HEADS,
              M_WIDTH, M_WIDTH, 2 * M_HEADS, M_WIDTH, 2 * D_MODEL)
_MIX_OFFS = np.concatenate([[0], np.cumsum(_MIX_SIZES)]).tolist()
_PROJ_SEGS = (("kv", 6 * KV_WIDTH), ("mqk", M_WIDTH), ("mv", M_WIDTH),
              ("mo", M_WIDTH), ("merge", 2 * D_MODEL), ("small", LANES))


def _rms(x, g):
    return x * lax.rsqrt(jnp.mean(x * x, axis=-1, keepdims=True) + EPS) * g


def _ffn_kernel(x_ref, g_ref, wg_ref, wu_ref, wo_ref, gf_ref, o_ref, h_ref, acc_ref, *, final_norm):
    j = pl.program_id(1)

    @pl.when(j == 0)
    def _():
        h_ref[...] = _rms(x_ref[...], g_ref[...]).astype(jnp.bfloat16)
        acc_ref[...] = jnp.zeros_like(acc_ref)

    h = h_ref[...]
    gate = jnp.dot(h, wg_ref[...], preferred_element_type=jnp.float32)
    up = jnp.dot(h, wu_ref[...], preferred_element_type=jnp.float32)
    a = (jax.nn.silu(gate) * up).astype(jnp.bfloat16)
    acc_ref[...] += jnp.dot(a, wo_ref[...], preferred_element_type=jnp.float32)

    @pl.when(j == pl.num_programs(1) - 1)
    def _():
        y = x_ref[...] + 0.5 * acc_ref[...]
        if final_norm:
            y = _rms(y, gf_ref[...])
        o_ref[...] = y


def _ffn(x, g, w_in, w_out, g_final, *, final_norm):
    n = x.shape[0]
    tm = min(FFN_TOKEN_TILE, n)
    nj = D_FF // FFN_FF_TILE
    return pl.pallas_call(
        partial(_ffn_kernel, final_norm=final_norm),
        out_shape=jax.ShapeDtypeStruct((n, D_MODEL), jnp.float32),
        grid=(n // tm, nj),
        in_specs=[
            pl.BlockSpec((tm, D_MODEL), lambda i, j: (i, 0)),
            pl.BlockSpec((1, D_MODEL), lambda i, j: (0, 0)),
            pl.BlockSpec((D_MODEL, FFN_FF_TILE), lambda i, j: (0, j)),
            pl.BlockSpec((D_MODEL, FFN_FF_TILE), lambda i, j: (0, j + D_FF // FFN_FF_TILE)),
            pl.BlockSpec((FFN_FF_TILE, D_MODEL), lambda i, j: (j, 0)),
            pl.BlockSpec((1, D_MODEL), lambda i, j: (0, 0)),
        ],
        out_specs=pl.BlockSpec((tm, D_MODEL), lambda i, j: (i, 0)),
        scratch_shapes=[pltpu.VMEM((tm, D_MODEL), jnp.bfloat16), pltpu.VMEM((tm, D_MODEL), jnp.float32)],
        compiler_params=pltpu.CompilerParams(dimension_semantics=("arbitrary", "arbitrary"),
                                             vmem_limit_bytes=V7X_VMEM_LIMIT_BYTES),
        name="ffn",
    )(x, g.reshape(1, D_MODEL), w_in, w_in, w_out, g_final.reshape(1, D_MODEL))


def _proj_kernel(x_ref, g_ref, w_ref, qpad_ref, kvb_ref, *o_refs):
    h = _rms(x_ref[...], g_ref[...]).astype(jnp.bfloat16)
    zq = jnp.dot(h, w_ref[:, :QPAD_WIDTH], preferred_element_type=jnp.float32)
    qpad_ref[...] = (zq * (HEAD_DIM ** -0.5)).astype(jnp.bfloat16)
    off = QPAD_WIDTH
    o_refs = list(o_refs)
    for name, width in _PROJ_SEGS:
        z = jnp.dot(h, w_ref[:, off:off + width], preferred_element_type=jnp.float32)
        if name == "kv":
            kvb_ref[...] = z.astype(jnp.bfloat16)
            for i in range(6):
                o_refs.pop(0)[...] = z[:, i * KV_WIDTH:(i + 1) * KV_WIDTH]
        else:
            o_refs.pop(0)[...] = z
        off += width


def _regroup_mix_weight(w_mix_in):
    seg = lambda i: w_mix_in[:, _MIX_OFFS[i]:_MIX_OFFS[i + 1]]
    wq = seg(0)
    zeros = jnp.zeros((D_MODEL, HEAD_DIM), w_mix_in.dtype)
    qpad = []
    for h in range(N_HEADS):
        wh = wq[:, h * HEAD_DIM:(h + 1) * HEAD_DIM]
        qpad += [wh, zeros] if h // GROUP == 0 else [zeros, wh]
    small = jnp.concatenate([seg(7), seg(10), jnp.zeros((D_MODEL, LANES - 3 * N_HEADS - 2 * M_HEADS), w_mix_in.dtype)], axis=1)
    cols = qpad + [seg(i) for i in range(1, 7)] + [seg(8), seg(9), seg(11), seg(12), small]
    return jnp.concatenate(cols, axis=1).astype(jnp.bfloat16)


def _project(x, g, w_regrouped):
    n = x.shape[0]
    tm = min(PROJ_TOKEN_TILE, n)
    wcols = w_regrouped.shape[1]
    row = lambda width: pl.BlockSpec((tm, width), lambda i: (i, 0))
    widths = []
    for name, width in _PROJ_SEGS:
        widths += [KV_WIDTH] * 6 if name == "kv" else [width]
    return pl.pallas_call(
        _proj_kernel,
        out_shape=[jax.ShapeDtypeStruct((n, QPAD_WIDTH), jnp.bfloat16),
                   jax.ShapeDtypeStruct((n, 6 * KV_WIDTH), jnp.bfloat16)]
                  + [jax.ShapeDtypeStruct((n, width), jnp.float32) for width in widths],
        grid=(n // tm,),
        in_specs=[
            row(D_MODEL),
            pl.BlockSpec((1, D_MODEL), lambda i: (0, 0)),
            pl.BlockSpec((D_MODEL, wcols), lambda i: (0, 0)),
        ],
        out_specs=[row(QPAD_WIDTH), row(6 * KV_WIDTH)] + [row(width) for width in widths],
        compiler_params=pltpu.CompilerParams(dimension_semantics=("arbitrary",),
                                             vmem_limit_bytes=V7X_VMEM_LIMIT_BYTES),
        name="mix_in_proj",
    )(x, g.reshape(1, D_MODEL), w_regrouped)


def _dot_nt(a, b):
    return lax.dot_general(a, b, (((1,), (1,)), ((), ())), preferred_element_type=jnp.float32)


def _bucket_table():
    d = np.arange(MAX_DIST + 1)
    max_exact = N_BUCKETS // 2
    nf = np.maximum(d, 1).astype(np.float64)
    large = max_exact + (np.log(nf / max_exact) / math.log(MAX_DIST / max_exact) * (N_BUCKETS - max_exact)).astype(np.int64)
    return np.where(d < max_exact, d, np.minimum(large, N_BUCKETS - 1)).astype(np.int32)


def _bias_tables(rel_bias, n_c):
    tab = rel_bias.astype(jnp.float32)[_bucket_table()].T
    i = np.arange(Q_BLOCK)[:, None]
    j = np.arange(Q_BLOCK)[None, :]
    far = tab[:, MAX_DIST][:, None, None]
    t0 = tab[:, np.clip(i - j, 0, MAX_DIST)] - far
    t1 = tab[:, np.minimum(Q_BLOCK + i - j, MAX_DIST)] - far
    tiles = jnp.stack([t0, t1]).reshape(2, N_HEADS * Q_BLOCK, Q_BLOCK)
    lo, hi = -2 * Q_BLOCK // D_CMP, Q_BLOCK // D_CMP
    dist = np.arange(Q_BLOCK)[None, :] - (L_CMP - 1) - D_CMP * np.arange(lo, hi)[:, None]
    band = jnp.where(jnp.asarray(dist >= 0), tab[:, np.clip(dist, 0, MAX_DIST)], NEG_BIG)
    bc = jnp.concatenate([jnp.broadcast_to(tab[:, MAX_DIST][:, None, None], (N_HEADS, n_c + lo, Q_BLOCK)), band,
                          jnp.full((N_HEADS, n_c - hi, Q_BLOCK), NEG_BIG, jnp.float32)], axis=1)
    bct = bc.transpose(1, 0, 2).reshape(2 * n_c, N_HEADS * Q_BLOCK)
    return tiles, bct


def _nsa_prompt_kernel(q_ref, gate_ref, kc_ref, vct_ref, ks_ref, vs_ref, kw_ref, vw_ref, tiles_ref, bct_ref,
                       o_ref, m_ref, l_ref, acc_ref, imp_ref, *, n_c):
    f32, bf16 = jnp.float32, jnp.bfloat16
    qb = pl.program_id(1)
    rows_all = N_HEADS * Q_BLOCK
    qi = lax.broadcasted_iota(jnp.int32, (rows_all, Q_BLOCK), 0) & (Q_BLOCK - 1)
    kj = lax.broadcasted_iota(jnp.int32, (rows_all, Q_BLOCK), 1)
    causal = qi >= kj
    window_edge = kj >= qi
    sig = jax.nn.sigmoid(gate_ref[...])

    s_io = lax.broadcasted_iota(jnp.int32, (N_SEL_PAD, N_KV * Q_BLOCK), 0)
    i_io = lax.broadcasted_iota(jnp.int32, (N_SEL_PAD, N_KV * Q_BLOCK), 1) & (Q_BLOCK - 1)
    qpos = qb * Q_BLOCK + i_io
    cur = 2 * qb + (i_io >= L_SEL).astype(jnp.int32)
    blk_valid = s_io * L_SEL <= qpos
    blk_forced = (s_io == 0) | (s_io == cur) | (s_io == cur - 1)
    ov_s = lax.broadcasted_iota(jnp.int32, (N_SEL_PAD, n_c), 0)
    ov_c = lax.broadcasted_iota(jnp.int32, (N_SEL_PAD, n_c), 1)
    overlap_t = jnp.where((ov_c >= 4 * ov_s - 1) & (ov_c <= 4 * ov_s + 3), 1.0, 0.0).astype(bf16)

    def flash_init():
        m_ref[...] = jnp.full(m_ref.shape, NEG_BIG, f32)
        l_ref[...] = jnp.zeros(l_ref.shape, f32)
        acc_ref[...] = jnp.zeros(acc_ref.shape, f32)

    def flash_step(s, v_tile):
        m_old = m_ref[...]
        m_new = jnp.maximum(m_old, jnp.max(s, axis=-1, keepdims=True))
        alpha = jnp.exp(m_old - m_new)
        p = jnp.exp(s - jnp.concatenate([m_new] * (s.shape[1] // LANES), axis=1))
        l_ref[...] = alpha * l_ref[...] + jnp.sum(p, axis=-1, keepdims=True)
        acc_ref[...] = alpha * acc_ref[...] + jnp.dot(p.astype(bf16), v_tile, preferred_element_type=f32)
        m_ref[...] = m_new

    def key_rows(kt, n_tiles=1):
        return pl.ds(pl.multiple_of(kt * Q_BLOCK, Q_BLOCK), n_tiles * Q_BLOCK)

    q_all = jnp.concatenate([q_ref[:, h * LANES:(h + 1) * LANES] for h in range(N_HEADS)], axis=0)

    st = _dot_nt(kc_ref[0], q_all)
    st = st + bct_ref[pl.ds(pl.multiple_of(n_c - 8 * qb, 8), n_c), :]
    mx = jnp.max(st, axis=0, keepdims=True)
    mx = jnp.where(mx < 0.1 * NEG_BIG, 0.0, mx)
    p = jnp.exp(st - mx)
    p = p / jnp.maximum(jnp.sum(p, axis=0, keepdims=True), 1e-30)
    pb = p.astype(bf16)
    oc_t = jnp.dot(vct_ref[0], pb, preferred_element_type=f32)

    imps = []
    for g in range(N_KV):
        imp_g = None
        for r in range(GROUP):
            h = GROUP * g + r
            part = jnp.dot(overlap_t, pb[:, h * Q_BLOCK:(h + 1) * Q_BLOCK], preferred_element_type=f32)
            imp_g = part if imp_g is None else imp_g + part
        imps.append(imp_g)
    imp = jnp.concatenate(imps, axis=1)
    imp = jnp.where(blk_valid, imp + jnp.where(blk_forced, FORCE_SCORE, 0.0), -jnp.inf)
    imp_ref[...] = imp

    def rank_body(s, rank):
        other = imp_ref[pl.ds(s, 1), :]
        beats = (other > imp) | ((other == imp) & (s_io > s))
        return rank + jnp.where(beats, 1, 0)

    rank = lax.fori_loop(0, 2 * qb + 2, rank_body, jnp.zeros(imp.shape, jnp.int32))
    sel_off = jnp.where(rank < K_SEL, 0.0, SEL_OFF)
    sel_rows = []
    for g in range(N_KV):
        sel_rows += [sel_off[:, g * Q_BLOCK:(g + 1) * Q_BLOCK].T.astype(bf16)] * GROUP
    lhs_sel = jnp.concatenate([jnp.concatenate(sel_rows, axis=0), q_all], axis=1)

    def sel_scores(kt, n_tiles=1):
        key_blk = jnp.right_shift(lax.broadcasted_iota(jnp.int32, (n_tiles * Q_BLOCK, N_SEL_PAD), 0),
                                  L_SEL.bit_length() - 1)
        blk_lane = lax.broadcasted_iota(jnp.int32, (n_tiles * Q_BLOCK, N_SEL_PAD), 1)
        onehot = jnp.where(blk_lane == 2 * kt + key_blk, 1.0, 0.0).astype(bf16)
        rhs = jnp.concatenate([onehot, ks_ref[0, key_rows(kt, n_tiles), :]], axis=1)
        return _dot_nt(lhs_sel, rhs)

    flash_init()

    n_far = jnp.maximum(qb - 1, 0)

    def far_step(kt, n_tiles):
        flash_step(sel_scores(kt, n_tiles), vs_ref[0, key_rows(kt, n_tiles), :])

    def far_body(i, carry):
        far_step(FAR_TILES * i, FAR_TILES)
        return carry

    lax.fori_loop(0, n_far // FAR_TILES, far_body, 0)
    width = FAR_TILES // 2
    while width >= 1:
        @pl.when((n_far & width) != 0)
        def _(width=width):
            far_step((n_far // (2 * width)) * (2 * width), width)
        width //= 2

    @pl.when(qb >= 1)
    def _():
        flash_step(sel_scores(qb - 1) + tiles_ref[1], vs_ref[0, key_rows(qb - 1), :])

    flash_step(jnp.where(causal, sel_scores(qb) + tiles_ref[0], NEG_BIG), vs_ref[0, key_rows(qb), :])
    o_sel = acc_ref[...] / l_ref[...]

    scores, values = [], []
    for dt in range(WINDOW // Q_BLOCK + 1):
        rows = key_rows(jnp.maximum(qb - dt, 0))
        s = _dot_nt(q_all, kw_ref[0, rows, :])
        if dt < 2:
            s = s + tiles_ref[dt]
        if dt == 0:
            s = jnp.where(causal, s, NEG_BIG)
        else:
            visible = window_edge if dt == WINDOW // Q_BLOCK else True
            s = jnp.where(visible & (qb >= dt), s, NEG_BIG)
        scores.append(s)
        values.append(vw_ref[0, rows, :])
    s = jnp.concatenate(scores, axis=1)
    p = jnp.exp(s - jnp.max(s, axis=-1, keepdims=True))
    o_win = (jnp.dot(p.astype(bf16), jnp.concatenate(values, axis=0), preferred_element_type=f32)
             / jnp.sum(p, axis=-1, keepdims=True))

    for h in range(N_HEADS):
        rs = slice(h * Q_BLOCK, (h + 1) * Q_BLOCK)
        o = (oc_t[:, rs].T * sig[:, 3 * h:3 * h + 1] + o_sel[rs] * sig[:, 3 * h + 1:3 * h + 2]
             + o_win[rs] * sig[:, 3 * h + 2:3 * h + 3])
        o_ref[:, h * LANES:(h + 1) * LANES] = o.astype(bf16)


def _nsa_prompt(qpad, small, kvb, kcb, vct, rel_bias, b, t):
    n_c = t // D_CMP
    n_qb = t // Q_BLOCK
    tiles, bct = _bias_tables(rel_bias, n_c)
    kv3 = kvb.reshape(b, t, 6 * KV_WIDTH)
    rows = lambda width: pl.BlockSpec((Q_BLOCK, width), lambda bi, qi: (bi * n_qb + qi, 0))
    seq = lambda lane_block: pl.BlockSpec((1, t, KV_WIDTH), lambda bi, qi: (bi, 0, lane_block))
    full = lambda a: pl.BlockSpec(a.shape, lambda bi, qi: (0,) * a.ndim)
    rows_all = N_HEADS * Q_BLOCK
    return pl.pallas_call(
        partial(_nsa_prompt_kernel, n_c=n_c),
        out_shape=jax.ShapeDtypeStruct((b * t, QPAD_WIDTH), jnp.bfloat16),
        grid=(b, n_qb),
        in_specs=[rows(QPAD_WIDTH), rows(LANES),
                  pl.BlockSpec((1, n_c, KV_WIDTH), lambda bi, qi: (bi, 0, 0)),
                  pl.BlockSpec((1, KV_WIDTH, n_c), lambda bi, qi: (bi, 0, 0)),
                  seq(2), seq(3), seq(4), seq(5), full(tiles), full(bct)],
        out_specs=rows(QPAD_WIDTH),
        scratch_shapes=[pltpu.VMEM((rows_all, LANES), jnp.float32), pltpu.VMEM((rows_all, LANES), jnp.float32),
                        pltpu.VMEM((rows_all, KV_WIDTH), jnp.float32),
                        pltpu.VMEM((N_SEL_PAD, N_KV * Q_BLOCK), jnp.float32)],
        compiler_params=pltpu.CompilerParams(dimension_semantics=("arbitrary", "arbitrary"),
                                             vmem_limit_bytes=V7X_VMEM_LIMIT_BYTES),
        name="nsa_prompt",
    )(qpad, small, kcb, vct, kv3, kv3, kv3, kv3, tiles, bct)


def _pad_nsa_out_weight(w_proj_nsa):
    zeros = jnp.zeros((HEAD_DIM, D_MODEL), w_proj_nsa.dtype)
    rows = []
    for h in range(N_HEADS):
        wh = w_proj_nsa[h * HEAD_DIM:(h + 1) * HEAD_DIM]
        rows += [wh, zeros] if h // GROUP == 0 else [zeros, wh]
    return jnp.concatenate(rows, axis=0)


SUB_WIDTH = D_CMP * KV_WIDTH
PHI_HIDDEN = 2 * HEAD_DIM
PAGE_SUBS = PAGE_SIZE // D_CMP
CMP_PAGES_PER_STEP = 16


def _compress_weights(pos_emb, w1, w2):
    eye = jnp.eye(N_KV, dtype=w1.dtype)
    halves = []
    for r in range(CMP_RATIO):
        w1r = w1[r * D_CMP:(r + 1) * D_CMP]
        halves.append(jnp.einsum('ldh,gk->lgdkh', w1r, eye).reshape(SUB_WIDTH, N_KV * PHI_HIDDEN))
    w1big = jnp.concatenate(halves, axis=1).astype(jnp.bfloat16)
    pos = jnp.broadcast_to(pos_emb.reshape(CMP_RATIO, D_CMP, 1, HEAD_DIM), (CMP_RATIO, D_CMP, N_KV, HEAD_DIM))
    w2big = jnp.einsum('hd,gk->ghkd', w2, eye).reshape(N_KV * PHI_HIDDEN, KV_WIDTH).astype(jnp.bfloat16)
    return w1big, pos.reshape(CMP_RATIO, 1, SUB_WIDTH), w2big


def _compress_tokens(lhs0, lhs1, w1big, w2big):
    hw = N_KV * PHI_HIDDEN
    p0 = jnp.dot(lhs0, w1big[:, :hw], preferred_element_type=jnp.float32)
    p1 = jnp.dot(lhs1, w1big[:, hw:], preferred_element_type=jnp.float32)
    n_sub = p1.shape[0]
    hidden = p0 + pltpu.roll(p1, n_sub - 1, 0)
    return jnp.dot(jax.nn.gelu(hidden).astype(jnp.bfloat16), w2big, preferred_element_type=jnp.float32)


def _compress_rows_kernel(rk_ref, rv_ref, w1k_ref, pk_ref, w2k_ref, w1v_ref, pv_ref, w2v_ref, kc_ref, vc_ref):
    for r_ref, w1_ref, p_ref, w2_ref, o_ref in ((rk_ref, w1k_ref, pk_ref, w2k_ref, kc_ref),
                                                 (rv_ref, w1v_ref, pv_ref, w2v_ref, vc_ref)):
        x = r_ref[0]
        lhs = [(x + p_ref[r]).astype(jnp.bfloat16) for r in range(CMP_RATIO)]
        o_ref[0] = _compress_tokens(lhs[0], lhs[1], w1_ref[...], w2_ref[...]).astype(o_ref.dtype)


def _compress_rows(k_rows, v_rows, b, t, wk, wv):
    n_sub = t // D_CMP
    view = lambda a: a.reshape(b, n_sub, SUB_WIDTH)
    seq = pl.BlockSpec((1, n_sub, SUB_WIDTH), lambda bi: (bi, 0, 0))
    full = lambda a: pl.BlockSpec(a.shape, lambda bi: (0,) * a.ndim)
    out = pl.BlockSpec((1, n_sub, KV_WIDTH), lambda bi: (bi, 0, 0))
    return pl.pallas_call(
        _compress_rows_kernel,
        out_shape=[jax.ShapeDtypeStruct((b, n_sub, KV_WIDTH), jnp.bfloat16)] * 2,
        grid=(b,),
        in_specs=[seq, seq] + [full(a) for a in (*wk, *wv)],
        out_specs=[out, out],
        compiler_params=pltpu.CompilerParams(dimension_semantics=("arbitrary",),
                                             vmem_limit_bytes=V7X_VMEM_LIMIT_BYTES),
        name="compress_rows",
    )(view(k_rows), view(v_rows), *wk, *wv)


def _compress_pages_kernel(pt_ref, *refs):
    n_pg = CMP_PAGES_PER_STEP
    k_pages, v_pages = refs[:n_pg], refs[n_pg:2 * n_pg]
    w1k_ref, pk_ref, w2k_ref, w1v_ref, pv_ref, w2v_ref, kc_ref, vc_ref, lhs_ref, rows_ref = refs[2 * n_pg:]
    step = pl.program_id(1)
    for c, (pages, p_ref) in enumerate(((k_pages, pk_ref), (v_pages, pv_ref))):
        for j in range(0, n_pg, 2):
            rows_ref[c, j] = pages[j][0].T
            rows_ref[c, j + 1] = pages[j + 1][0].T
            rows = pl.ds(pl.multiple_of(step * n_pg * PAGE_SUBS + j * PAGE_SUBS, 2 * PAGE_SUBS), 2 * PAGE_SUBS)
            for l in range(D_CMP):
                x = jnp.concatenate([rows_ref[c, j, pl.ds(l, PAGE_SUBS, stride=D_CMP), :],
                                     rows_ref[c, j + 1, pl.ds(l, PAGE_SUBS, stride=D_CMP), :]], axis=0)
                lanes = slice(l * KV_WIDTH, (l + 1) * KV_WIDTH)
                for r in range(CMP_RATIO):
                    lhs_ref[c, r, rows, lanes] = (x + p_ref[r, :, lanes]).astype(jnp.bfloat16)

    @pl.when(step == pl.num_programs(1) - 1)
    def _():
        for c, (w1_ref, w2_ref, o_ref) in enumerate(((w1k_ref, w2k_ref, kc_ref), (w1v_ref, w2v_ref, vc_ref))):
            o_ref[0] = _compress_tokens(lhs_ref[c, 0], lhs_ref[c, 1], w1_ref[...], w2_ref[...]).astype(o_ref.dtype)


def _compress_pages(cache_k, cache_v, page_table, wk, wv):
    b, n_pages = page_table.shape
    n_pool = cache_k.shape[0]
    n_sub = n_pages * PAGE_SUBS
    n_pg = CMP_PAGES_PER_STEP
    view = lambda c: c.transpose(0, 2, 3, 1).reshape(n_pool, KV_WIDTH, PAGE_SIZE)
    page = lambda j: pl.BlockSpec(
        (1, KV_WIDTH, PAGE_SIZE),
        lambda bi, si, pt: (jnp.clip(pt[jnp.minimum(bi, b - 1), jnp.minimum(si, n_pages // n_pg - 1) * n_pg + j],
                                     0, n_pool - 1), 0, 0))
    full = lambda a: pl.BlockSpec(a.shape, lambda bi, si, pt: (0,) * a.ndim)
    out = pl.BlockSpec((1, n_sub, KV_WIDTH), lambda bi, si, pt: (bi, 0, 0))
    return pl.pallas_call(
        _compress_pages_kernel,
        out_shape=[jax.ShapeDtypeStruct((b, n_sub, KV_WIDTH), jnp.bfloat16)] * 2,
        grid_spec=pltpu.PrefetchScalarGridSpec(
            num_scalar_prefetch=1,
            grid=(b, n_pages // n_pg),
            in_specs=[page(j) for j in range(n_pg)] * 2 + [full(a) for a in (*wk, *wv)],
            out_specs=[out, out],
            scratch_shapes=[pltpu.VMEM((2, CMP_RATIO, n_sub, SUB_WIDTH), jnp.bfloat16),
                            pltpu.VMEM((2, n_pg, PAGE_SIZE, KV_WIDTH), jnp.float32)]),
        compiler_params=pltpu.CompilerParams(dimension_semantics=("arbitrary", "arbitrary"),
                                             vmem_limit_bytes=V7X_VMEM_LIMIT_BYTES),
        name="compress_pages",
    )(page_table, *([view(cache_k)] * n_pg), *([view(cache_v)] * n_pg), *wk, *wv)


N_SEL_DEC = PAST_LEN // L_SEL + 1
N_SEL_DEC_PAD = 256
CUR_BLOCK = PAST_LEN // L_SEL


def _nsa_decode_cmp_kernel(q_ref, kc_ref, vc_ref, bias_ref, oc_ref, imp_ref):
    f32, bf16 = jnp.float32, jnp.bfloat16
    n_c = kc_ref.shape[1]
    s = _dot_nt(q_ref[0], kc_ref[0]) + bias_ref[...]
    mx = jnp.max(s, axis=-1, keepdims=True)
    mx = jnp.where(mx < 0.1 * NEG_BIG, 0.0, mx)
    p = jnp.exp(s - mx)
    p = p / jnp.maximum(jnp.sum(p, axis=-1, keepdims=True), 1e-30)
    pb = p.astype(bf16)
    oc_ref[0] = jnp.dot(pb, vc_ref[0], preferred_element_type=f32)
    ov_c = lax.broadcasted_iota(jnp.int32, (n_c, N_SEL_DEC_PAD), 0)
    ov_s = lax.broadcasted_iota(jnp.int32, (n_c, N_SEL_DEC_PAD), 1)
    overlap = jnp.where((ov_c >= 4 * ov_s - 1) & (ov_c <= 4 * ov_s + 3), 1.0, 0.0).astype(bf16)
    imp_h = jnp.dot(pb, overlap, preferred_element_type=f32)
    head = lax.broadcasted_iota(jnp.int32, imp_h.shape, 0)
    imp_ref[0] = jnp.concatenate(
        [jnp.sum(jnp.where((head >= g * GROUP) & (head < (g + 1) * GROUP), imp_h, 0.0), axis=0, keepdims=True)
         for g in range(N_KV)], axis=1)


def _nsa_decode_topk_kernel(imp_ref, idx_ref, val_ref):
    f32 = jnp.float32
    nb = imp_ref.shape[0]
    s_io = lax.broadcasted_iota(jnp.int32, (N_SEL_DEC_PAD, nb), 0)
    visible = s_io < N_SEL_DEC
    forced = (s_io == 0) | (s_io == CUR_BLOCK) | (s_io == CUR_BLOCK - 1)
    tri_r = lax.broadcasted_iota(jnp.int32, (N_SEL_DEC_PAD, N_SEL_DEC_PAD), 0)
    tri_c = lax.broadcasted_iota(jnp.int32, (N_SEL_DEC_PAD, N_SEL_DEC_PAD), 1)
    before = jnp.where(tri_c < tri_r, 1.0, 0.0).astype(jnp.bfloat16)
    for g in range(N_KV):
        x = imp_ref[:, g * N_SEL_DEC_PAD:(g + 1) * N_SEL_DEC_PAD]
        xt = jnp.concatenate([x[:, i * LANES:(i + 1) * LANES].T for i in range(N_SEL_DEC_PAD // LANES)], axis=0)
        val = jnp.where(visible, xt + jnp.where(forced, FORCE_SCORE, 0.0), -jnp.inf)
        val_ref[...] = val

        def rank_body(s, rank):
            other = val_ref[pl.ds(s, 1), :]
            beats = (other > val) | ((other == val) & (s_io > s))
            return rank + jnp.where(beats, 1, 0)

        rank = lax.fori_loop(0, N_SEL_DEC, rank_body, jnp.zeros(val.shape, jnp.int32))
        sel = (rank < K_SEL) & visible
        n_before = jnp.dot(before, jnp.where(sel, 1.0, 0.0).astype(jnp.bfloat16), preferred_element_type=f32)
        for k in range(K_SEL):
            hit = sel & (n_before == float(k))
            idx_ref[g, pl.ds(k, 1), :] = jnp.sum(jnp.where(hit, s_io.astype(f32), 0.0), axis=0,
                                                 keepdims=True).astype(jnp.int32)


def _nsa_decode_sel_kernel(pt_ref, ix_ref, *refs):
    f32, bf16 = jnp.float32, jnp.bfloat16
    n_blk = N_KV * K_SEL
    k_blocks, v_blocks = refs[:n_blk], refs[n_blk:2 * n_blk]
    (q_ref, oc_ref, gl_ref, ksn_ref, vsn_ref, kwn_ref, vwn_ref, kwin_ref, vwin_ref, near_ref, b0_ref, bw_ref,
     o_ref, kwo_ref, vwo_ref) = refs[2 * n_blk:]
    b = pl.program_id(0)
    q = q_ref[0]
    qf = q.astype(f32)
    round_bf = lambda a: a.astype(bf16).astype(f32)
    group1 = lax.broadcasted_iota(jnp.int32, (N_HEADS, LANES), 0) >= GROUP
    second_half = lax.broadcasted_iota(jnp.int32, (N_HEADS, LANES), 1) >= L_SEL
    neg_tile = jnp.full((N_HEADS, LANES), NEG_BIG, f32)
    no_rows = jnp.zeros((HEAD_DIM, PAGE_SIZE), bf16)

    def block_bias(s):
        tile = jnp.where(s == CUR_BLOCK, neg_tile,
                         jnp.where(s == CUR_BLOCK - 1, near_ref[1],
                                   jnp.where(s == CUR_BLOCK - 2, near_ref[0], near_ref[2])))
        return jnp.where(second_half == ((s & 1) == 1), tile, neg_tile)

    def group_rows(x, g):
        x = x.astype(bf16)
        return jnp.concatenate([x, no_rows] if g == 0 else [no_rows, x], axis=0)

    scores, values, has_new = [], [], []
    for g in range(N_KV):
        ids = [ix_ref[b, g * K_SEL + k] for k in range(K_SEL)]
        kt = jnp.concatenate([group_rows(k_blocks[g * K_SEL + k][0], g) for k in range(K_SEL)], axis=1)
        values.append(jnp.concatenate([group_rows(v_blocks[g * K_SEL + k][0], g) for k in range(K_SEL)], axis=1))
        bias = jnp.concatenate([block_bias(ids[k]) for k in range(K_SEL)], axis=1)
        scores.append(jnp.dot(q, kt, preferred_element_type=f32) + bias)
        flag = ids[0] == CUR_BLOCK
        for k in range(1, K_SEL):
            flag = flag | (ids[k] == CUR_BLOCK)
        has_new.append(jnp.where(flag, 0.0, NEG_BIG))
    s = jnp.where(group1[:, :1], scores[1], scores[0])
    s_new = jnp.sum(qf * round_bf(ksn_ref[0]), axis=-1, keepdims=True) + b0_ref[...]
    s_new = s_new + jnp.where(group1, has_new[1], has_new[0])
    m = jnp.maximum(jnp.max(s, axis=-1, keepdims=True), s_new)
    p = jnp.exp(s - m[:, :1])
    p_new = jnp.exp(s_new - m)
    l = jnp.sum(p, axis=-1, keepdims=True) + p_new
    pb = p.astype(bf16)
    o_sel = jnp.where(group1, _dot_nt(pb, values[1]), _dot_nt(pb, values[0]))
    o_sel = (o_sel + round_bf(p_new) * round_bf(vsn_ref[0])) / l

    kwin, vwin = kwin_ref[0], vwin_ref[0]
    s = jnp.dot(q, kwin.astype(bf16), preferred_element_type=f32) + bw_ref[...]
    s_new = jnp.sum(qf * round_bf(kwn_ref[0]), axis=-1, keepdims=True) + b0_ref[...]
    m = jnp.maximum(jnp.max(s, axis=-1, keepdims=True), s_new)
    p = jnp.exp(s - m[:, :1])
    p_new = jnp.exp(s_new - m)
    l = jnp.sum(p, axis=-1, keepdims=True) + p_new
    o_win = _dot_nt(p.astype(bf16), vwin.astype(bf16))
    o_win = (o_win + round_bf(p_new) * round_bf(vwn_ref[0])) / l

    gates = jax.nn.sigmoid(gl_ref[0])
    o_ref[0] = (oc_ref[0] * gates[0] + o_sel * gates[1] + o_win * gates[2]).astype(bf16)

    wb = kwin.shape[1]
    last = lax.broadcasted_iota(jnp.int32, kwin.shape, 1) == wb - 1
    as_column = lambda row: jnp.broadcast_to(row, (KV_WIDTH, KV_WIDTH)).T[:, :1]
    kwo_ref[0] = jnp.where(last, as_column(kwn_ref[0]), pltpu.roll(kwin, wb - 1, 1))
    vwo_ref[0] = jnp.where(last, as_column(vwn_ref[0]), pltpu.roll(vwin, wb - 1, 1))


def _nsa_decode(qpad, small, kc, vc, ks_new, vs_new, kw_new, vw_new, cache_k_sel, cache_v_sel,
                cache_k_win, cache_v_win, page_table, rel_bias):
    f32 = jnp.float32
    b = qpad.shape[0]
    n_c = kc.shape[1]
    n_pool = cache_k_sel.shape[0]
    wb = cache_k_win.shape[1]
    tab = rel_bias.astype(f32)[_bucket_table()].T
    dist_c = PAST_LEN - (L_CMP - 1) - D_CMP * np.arange(n_c)
    bias_c = jnp.where(jnp.asarray(dist_c >= 0)[None, :], tab[:, np.clip(dist_c, 0, MAX_DIST)], NEG_BIG)
    j = np.arange(LANES) % L_SEL
    near = jnp.stack([tab[:, 2 * L_SEL - j], tab[:, L_SEL - j],
                      jnp.broadcast_to(tab[:, MAX_DIST:], (N_HEADS, LANES))])
    bias0 = jnp.broadcast_to(tab[:, :1], (N_HEADS, LANES))
    bias_w = tab[:, np.minimum(wb - np.arange(wb), MAX_DIST)]
    q3 = qpad.reshape(b, N_HEADS, LANES)

    per_seq = lambda *shape: pl.BlockSpec((1,) + shape, lambda bi, *_: (bi,) + (0,) * len(shape))
    full = lambda a: pl.BlockSpec(a.shape, lambda bi, *_: (0,) * a.ndim)
    oc, imp = pl.pallas_call(
        _nsa_decode_cmp_kernel,
        out_shape=[jax.ShapeDtypeStruct((b, N_HEADS, LANES), f32),
                   jax.ShapeDtypeStruct((b, 1, N_KV * N_SEL_DEC_PAD), f32)],
        grid=(b,),
        in_specs=[per_seq(N_HEADS, LANES), per_seq(n_c, KV_WIDTH), per_seq(n_c, KV_WIDTH), full(bias_c)],
        out_specs=[per_seq(N_HEADS, LANES), per_seq(1, N_KV * N_SEL_DEC_PAD)],
        compiler_params=pltpu.CompilerParams(dimension_semantics=("arbitrary",)),
        name="nsa_decode_cmp",
    )(q3, kc, vc, bias_c)

    idx = pl.pallas_call(
        _nsa_decode_topk_kernel,
        out_shape=jax.ShapeDtypeStruct((N_KV, K_SEL, b), jnp.int32),
        scratch_shapes=[pltpu.VMEM((N_SEL_DEC_PAD, b), f32)],
        name="nsa_decode_topk",
    )(imp.reshape(b, N_KV * N_SEL_DEC_PAD))
    idx = idx.reshape(N_KV * K_SEL, b).T

    halves = PAGE_SIZE // L_SEL

    def sel_block(i):
        def index_map(bi, pt, ix):
            bs = jnp.minimum(bi, b - 1)
            s = jnp.clip(ix[bs, i], 0, CUR_BLOCK - 1)
            return (jnp.clip(pt[bs, s // halves], 0, n_pool - 1) * N_KV + i // K_SEL, 0, 0)
        return pl.BlockSpec((1, HEAD_DIM, PAGE_SIZE), index_map)

    keys_minor = lambda c: c.transpose(0, 2, 3, 1)
    page_groups = lambda c: keys_minor(c).reshape(n_pool * N_KV, HEAD_DIM, PAGE_SIZE)
    win_t = lambda c: keys_minor(c).reshape(b, KV_WIDTH, wb)
    glog = jnp.broadcast_to(small[:, :3 * N_HEADS].reshape(b, N_HEADS, 3).transpose(0, 2, 1)[..., None],
                            (b, 3, N_HEADS, LANES))
    new_row = lambda a: a.reshape(b, 1, KV_WIDTH)
    n_blk = N_KV * K_SEL
    o_pad, k_win, v_win = pl.pallas_call(
        _nsa_decode_sel_kernel,
        out_shape=[jax.ShapeDtypeStruct((b, N_HEADS, LANES), jnp.bfloat16),
                   jax.ShapeDtypeStruct((b, KV_WIDTH, wb), f32), jax.ShapeDtypeStruct((b, KV_WIDTH, wb), f32)],
        grid_spec=pltpu.PrefetchScalarGridSpec(
            num_scalar_prefetch=2,
            grid=(b,),
            in_specs=[sel_block(i) for i in range(n_blk)] * 2
                     + [per_seq(N_HEADS, LANES), per_seq(N_HEADS, LANES), per_seq(3, N_HEADS, LANES)]
                     + [per_seq(1, KV_WIDTH)] * 4 + [per_seq(KV_WIDTH, wb)] * 2
                     + [full(near), full(bias0), full(bias_w)],
            out_specs=[per_seq(N_HEADS, LANES), per_seq(KV_WIDTH, wb), per_seq(KV_WIDTH, wb)]),
        compiler_params=pltpu.CompilerParams(dimension_semantics=("arbitrary",)),
        name="nsa_decode_sel",
    )(page_table, idx, *([page_groups(cache_k_sel)] * n_blk), *([page_groups(cache_v_sel)] * n_blk),
      q3, oc, glog, new_row(ks_new), new_row(vs_new), new_row(kw_new), new_row(vw_new),
      win_t(cache_k_win), win_t(cache_v_win), near, bias0, bias_w)
    rows_major = lambda a: a.reshape(b, N_KV, HEAD_DIM, wb).transpose(0, 3, 1, 2)
    return o_pad.reshape(b, QPAD_WIDTH), rows_major(k_win), rows_major(v_win)


MLSTM_CHUNK = 128
MLSTM_DEC_TILE = 8
CONV_TAIL = 8


def _mlstm_norm_gate(h, zo, norm_g):
    return h * lax.rsqrt(jnp.mean(h * h, axis=-1, keepdims=True) + EPS) * norm_g * jax.nn.sigmoid(zo)


def _mlstm_prompt_kernel(zqk_ref, zv_ref, zo_ref, sm_ref, cw_ref, cb_ref, wq_ref, wk_ref, gb_ref, ng_ref,
                         o_ref, co_ref, no_ref, mo_ref, xbuf_ref, c_ref, n_ref, m_ref):
    f32, bf16 = jnp.float32, jnp.bfloat16
    L = MLSTM_CHUNK
    c = pl.program_id(1)

    @pl.when(c == 0)
    def _():
        xbuf_ref[:CONV_TAIL] = jnp.zeros((CONV_TAIL, M_WIDTH), f32)
        c_ref[...] = jnp.zeros(c_ref.shape, f32)
        n_ref[...] = jnp.zeros(n_ref.shape, f32)
        m_ref[...] = jnp.zeros(m_ref.shape, f32)

    x = zqk_ref[...]
    xbuf_ref[CONV_TAIL:] = x
    conv = cb_ref[...]
    for j in range(CONV_W):
        conv = conv + xbuf_ref[pl.ds(CONV_TAIL - (CONV_W - 1) + j, L), :] * cw_ref[j:j + 1, :]
    xbuf_ref[:CONV_TAIL] = x[L - CONV_TAIL:]
    a = jax.nn.silu(conv).astype(bf16)

    t_io = lax.broadcasted_iota(jnp.int32, (L, L), 0)
    s_io = lax.broadcasted_iota(jnp.int32, (L, L), 1)
    causal = t_io >= s_io
    sm = sm_ref[...]
    for h in range(M_HEADS):
        hs = slice(h * M_DH, (h + 1) * M_DH)
        q = jnp.dot(a[:, hs], wq_ref[h], preferred_element_type=f32)
        k = jnp.dot(a[:, hs], wk_ref[h], preferred_element_type=f32) * (M_DH ** -0.5)
        v = zv_ref[:, hs]
        qb, kb, vb = q.astype(bf16), k.astype(bf16), v.astype(bf16)
        col = 3 * N_HEADS + h
        ig = jnp.broadcast_to(sm[:, col:col + 1], (L, L)) + gb_ref[0:1, hs]
        lf = jax.nn.log_sigmoid(jnp.broadcast_to(sm[:, col + M_HEADS:col + M_HEADS + 1], (L, L)) + gb_ref[1:2, hs])
        bcum = lf
        sh = 1
        while sh < L:
            bcum = bcum + jnp.where(t_io >= sh, pltpu.roll(bcum, sh, 0), 0.0)
            sh *= 2
        m_old = m_ref[h:h + 1, :]
        c_old = c_ref[h]
        n_old = n_ref[h:h + 1, :]
        dmat = jnp.where(causal, bcum - bcum.T + ig.T, -jnp.inf)
        inter = bcum + m_old
        m_t = jnp.maximum(jnp.max(dmat, axis=-1, keepdims=True), inter)
        sc = _dot_nt(qb, kb) * jnp.exp(dmat - m_t)
        decay = jnp.exp(inter - m_t)
        num = decay * _dot_nt(qb, c_old.astype(bf16)) + jnp.dot(sc.astype(bf16), vb, preferred_element_type=f32)
        den = decay * jnp.sum(q * n_old, axis=-1, keepdims=True) + jnp.sum(sc, axis=-1, keepdims=True)
        hh = num / jnp.maximum(jnp.abs(den), jnp.exp(-m_t))
        o_ref[:, hs] = _mlstm_norm_gate(hh, zo_ref[:, hs], ng_ref[:, hs])

        b_last = bcum[L - 1:L, :]
        w_log = b_last - bcum + ig
        m_new = jnp.maximum(b_last + m_old, jnp.max(w_log, axis=0, keepdims=True))
        w = jnp.exp(w_log - m_new)
        carry = jnp.exp(b_last + m_old - m_new)
        c_ref[h] = carry * c_old + jnp.dot((w * v).T.astype(bf16), kb, preferred_element_type=f32)
        n_ref[h:h + 1, :] = carry * n_old + jnp.sum(w * k, axis=0, keepdims=True)
        m_ref[h:h + 1, :] = m_new

    @pl.when(c == pl.num_programs(1) - 1)
    def _():
        co_ref[0] = c_ref[...]
        no_ref[0] = n_ref[...]
        mo_ref[0] = m_ref[...]


def _mlstm_weights(conv_w, conv_b, wq, wk, gate_bias, norm_g):
    gb = jnp.repeat(gate_bias.astype(jnp.float32), M_DH, axis=1)
    return (conv_w, conv_b.reshape(1, M_WIDTH), wq.astype(jnp.bfloat16), wk.astype(jnp.bfloat16), gb,
            norm_g.reshape(1, M_WIDTH))


def _mlstm_prompt(zmqk, zmv, zmo, small, weights, b, t):
    L = MLSTM_CHUNK
    n_ch = t // L
    rows = lambda width: pl.BlockSpec((L, width), lambda bi, ci: (bi * n_ch + ci, 0))
    full = lambda a: pl.BlockSpec(a.shape, lambda bi, ci: (0,) * a.ndim)
    state = lambda *shape: pl.BlockSpec((1,) + shape, lambda bi, ci: (bi,) + (0,) * len(shape))
    f32 = jnp.float32
    return pl.pallas_call(
        _mlstm_prompt_kernel,
        out_shape=[jax.ShapeDtypeStruct((b * t, M_WIDTH), f32), jax.ShapeDtypeStruct((b, M_HEADS, M_DH, M_DH), f32),
                   jax.ShapeDtypeStruct((b, 8, M_DH), f32), jax.ShapeDtypeStruct((b, 8, M_DH), f32)],
        grid=(b, n_ch),
        in_specs=[rows(M_WIDTH), rows(M_WIDTH), rows(M_WIDTH), rows(LANES)] + [full(a) for a in weights],
        out_specs=[rows(M_WIDTH), state(M_HEADS, M_DH, M_DH), state(8, M_DH), state(8, M_DH)],
        scratch_shapes=[pltpu.VMEM((CONV_TAIL + L, M_WIDTH), f32), pltpu.VMEM((M_HEADS, M_DH, M_DH), f32),
                        pltpu.VMEM((8, M_DH), f32), pltpu.VMEM((8, M_DH), f32)],
        compiler_params=pltpu.CompilerParams(dimension_semantics=("arbitrary", "arbitrary")),
        name="mlstm_prompt",
    )(zmqk, zmv, zmo, small, *weights)


def _mlstm_decode_kernel(zqk_ref, conv_ref, zv_ref, zo_ref, sm_ref, c_ref, n_ref, m_ref,
                         cw_ref, cb_ref, wq_ref, wk_ref, gb_ref, ng_ref, o_ref, co_ref, no_ref, mo_ref):
    f32, bf16 = jnp.float32, jnp.bfloat16
    nt = MLSTM_DEC_TILE
    conv = cb_ref[...] + zqk_ref[...] * cw_ref[CONV_W - 1:CONV_W, :]
    for j in range(CONV_W - 1):
        conv = conv + conv_ref[j] * cw_ref[j:j + 1, :]
    a = jax.nn.silu(conv).astype(bf16)
    sm = sm_ref[...]
    lane = lax.broadcasted_iota(jnp.int32, (M_DH, M_DH), 1)
    pad_rows = jnp.zeros((M_DH - nt, M_DH), f32)
    for h in range(M_HEADS):
        hs = slice(h * M_DH, (h + 1) * M_DH)
        q = jnp.dot(a[:, hs], wq_ref[h], preferred_element_type=f32)
        k = jnp.dot(a[:, hs], wk_ref[h], preferred_element_type=f32) * (M_DH ** -0.5)
        v = zv_ref[:, hs]
        col = 3 * N_HEADS + h
        ig = jnp.broadcast_to(sm[:, col:col + 1], (nt, M_DH)) + gb_ref[0:1, hs]
        lf = jax.nn.log_sigmoid(jnp.broadcast_to(sm[:, col + M_HEADS:col + M_HEADS + 1], (nt, M_DH)) + gb_ref[1:2, hs])
        m_old = m_ref[:, hs]
        n_old = n_ref[:, hs]
        m_new = jnp.maximum(lf + m_old, ig)
        decay = jnp.exp(lf + m_old - m_new)
        w = jnp.exp(ig - m_new)
        sc = jnp.sum(q * k, axis=-1, keepdims=True) * w
        den = decay * jnp.sum(n_old * q, axis=-1, keepdims=True) + sc
        wv_t = jnp.concatenate([w * v, pad_rows], axis=0).T
        cq_t = jnp.zeros((M_DH, M_DH), f32)
        for i in range(nt):
            c_old = c_ref[i, h]
            cq = jnp.sum(c_old * q[i:i + 1, :], axis=-1, keepdims=True)
            cq_t = jnp.where(lane == i, cq, cq_t)
            co_ref[i, h] = decay[i:i + 1, :] * c_old + wv_t[:, i:i + 1] * k[i:i + 1, :]
        num = decay * cq_t.T[:nt] + sc * v
        hh = num / jnp.maximum(jnp.abs(den), jnp.exp(-m_new))
        o_ref[:, hs] = _mlstm_norm_gate(hh, zo_ref[:, hs], ng_ref[:, hs])
        no_ref[:, hs] = decay * n_old + w * k
        mo_ref[:, hs] = m_new


def _mlstm_decode(zmqk, zmv, zmo, small, state_c, state_n, state_m, state_conv, weights):
    b = zmqk.shape[0]
    nt = MLSTM_DEC_TILE
    f32 = jnp.float32
    rows = lambda width: pl.BlockSpec((nt, width), lambda i: (i, 0))
    full = lambda a: pl.BlockSpec(a.shape, lambda i: (0,) * a.ndim)
    cspec = pl.BlockSpec((nt, M_HEADS, M_DH, M_DH), lambda i: (i, 0, 0, 0))
    conv_t = state_conv.transpose(1, 0, 2)
    m_rep = jnp.repeat(state_m, M_DH, axis=1)
    return pl.pallas_call(
        _mlstm_decode_kernel,
        out_shape=[jax.ShapeDtypeStruct((b, M_WIDTH), f32), jax.ShapeDtypeStruct(state_c.shape, f32),
                   jax.ShapeDtypeStruct((b, M_WIDTH), f32), jax.ShapeDtypeStruct((b, M_WIDTH), f32)],
        grid=(b // nt,),
        in_specs=[rows(M_WIDTH), pl.BlockSpec((CONV_W - 1, nt, M_WIDTH), lambda i: (0, i, 0)), rows(M_WIDTH),
                  rows(M_WIDTH), rows(LANES), cspec, rows(M_WIDTH), rows(M_WIDTH)] + [full(a) for a in weights],
        out_specs=[rows(M_WIDTH), cspec, rows(M_WIDTH), rows(M_WIDTH)],
        compiler_params=pltpu.CompilerParams(dimension_semantics=("arbitrary",)),
        name="mlstm_decode",
    )(zmqk, conv_t, zmv, zmo, small, state_c, state_n.reshape(b, M_WIDTH), m_rep, *weights)


def _merge_kernel(x_ref, on_ref, om_ref, zm_ref, wn_ref, wm_ref, wo_ref, o_ref):
    zm = zm_ref[...]
    g_a = jax.nn.sigmoid(zm[:, :D_MODEL])
    g_b = jax.nn.sigmoid(zm[:, D_MODEL:])
    ya = jnp.dot(on_ref[...], wn_ref[...], preferred_element_type=jnp.float32)
    yb = jnp.dot(om_ref[...].astype(jnp.bfloat16), wm_ref[...], preferred_element_type=jnp.float32)
    y = (g_a * ya + g_b * yb).astype(jnp.bfloat16)
    o_ref[...] = x_ref[...] + jnp.dot(y, wo_ref[...], preferred_element_type=jnp.float32)


def _merge(x, o_nsa, o_mlstm, zmerge, w_proj_nsa, w_proj_mlstm, w_out):
    n = x.shape[0]
    tm = min(PROJ_TOKEN_TILE, n)
    row = lambda width: pl.BlockSpec((tm, width), lambda i: (i, 0))
    full = lambda a: pl.BlockSpec(a.shape, lambda i: (0, 0))
    return pl.pallas_call(
        _merge_kernel,
        out_shape=jax.ShapeDtypeStruct((n, D_MODEL), jnp.float32),
        grid=(n // tm,),
        in_specs=[row(D_MODEL), row(o_nsa.shape[1]), row(M_WIDTH), row(2 * D_MODEL),
                  full(w_proj_nsa), full(w_proj_mlstm), full(w_out)],
        out_specs=row(D_MODEL),
        compiler_params=pltpu.CompilerParams(dimension_semantics=("arbitrary",),
                                             vmem_limit_bytes=V7X_VMEM_LIMIT_BYTES),
        name="merge_out_proj",
    )(x, o_nsa, o_mlstm, zmerge, w_proj_nsa, w_proj_mlstm, w_out)


def _kv_rows(a, b, t):
    return a.reshape(b, t, N_KV, HEAD_DIM)


def _mix_prompt(proj, b, t, cmp_k, cmp_v, mlstm_w, rel_bias):
    qpad, kvb, kc_rows, vc_rows, ks_rows, vs_rows, kw_rows, vw_rows, zmqk, zmv, zmo, zmerge, small = proj
    kc, vc = _compress_rows(kc_rows, vc_rows, b, t, cmp_k, cmp_v)
    o_nsa = _nsa_prompt(qpad, small, kvb, kc, vc.transpose(0, 2, 1), rel_bias, b, t)
    o_mlstm, c_f, n_f, m_f = _mlstm_prompt(zmqk, zmv, zmo, small, mlstm_w, b, t)
    n_keep = min(WINDOW, t)
    states = (_kv_rows(kc_rows, b, t), _kv_rows(vc_rows, b, t), _kv_rows(ks_rows, b, t), _kv_rows(vs_rows, b, t),
              _kv_rows(kw_rows, b, t)[:, t - n_keep:], _kv_rows(vw_rows, b, t)[:, t - n_keep:],
              c_f, n_f[:, :M_HEADS], m_f[:, :M_HEADS, 0], zmqk.reshape(b, t, M_WIDTH)[:, t - (CONV_W - 1):])
    return o_nsa, o_mlstm, zmerge, states


def _mix_decode(proj, b, caches, mlstm_state, page_table, cmp_k, cmp_v, mlstm_w, rel_bias):
    cache_k_cmp, cache_v_cmp, cache_k_sel, cache_v_sel, cache_k_win, cache_v_win = caches
    state_c, state_n, state_m, state_conv = mlstm_state
    qpad, _, kc_new, vc_new, ks_new, vs_new, kw_new, vw_new, zmqk, zmv, zmo, zmerge, small = proj
    kc, vc = _compress_pages(cache_k_cmp, cache_v_cmp, page_table, cmp_k, cmp_v)
    o_nsa, k_win, v_win = _nsa_decode(qpad, small, kc, vc, ks_new, vs_new, kw_new, vw_new, cache_k_sel, cache_v_sel,
                                      cache_k_win, cache_v_win, page_table, rel_bias)
    o_mlstm, c_n, n_n, m_rep = _mlstm_decode(zmqk, zmv, zmo, small, state_c, state_n, state_m, state_conv, mlstm_w)
    wb = cache_k_win.shape[1]
    conv_new = jnp.concatenate([state_conv[:, 1:], zmqk[:, None, :]], axis=1)
    states = (_kv_rows(kc_new, b, 1), _kv_rows(vc_new, b, 1), _kv_rows(ks_new, b, 1), _kv_rows(vs_new, b, 1),
              _kv_rows(k_win, b, wb), _kv_rows(v_win, b, wb),
              c_n, n_n.reshape(b, M_HEADS, M_DH), m_rep[:, ::M_DH], conv_new)
    return o_nsa, o_mlstm, zmerge, states


def kernel(x_prompt, x_sample, cache_k_cmp, cache_v_cmp, cache_k_sel, cache_v_sel, cache_k_win, cache_v_win, state_mlstm_C, state_mlstm_n, state_mlstm_m, state_mlstm_conv, page_table, norm_ffn1, ffn1_w_in, ffn1_w_out, norm_mix, w_mix_in, cmp_pos_k, cmp_pos_v, cmp_phi_k1, cmp_phi_k2, cmp_phi_v1, cmp_phi_v2, rel_bias, mlstm_conv_w, mlstm_conv_b, mlstm_wq, mlstm_wk, mlstm_gate_bias, mlstm_norm, w_proj_nsa, w_proj_mlstm, w_out, norm_ffn2, ffn2_w_in, ffn2_w_out, norm_final):
    assert x_sample.shape[1] == DEC_SEQ == 1
    bf = lambda w: w.astype(jnp.bfloat16)
    w1i, w1o, w2i, w2o = bf(ffn1_w_in), bf(ffn1_w_out), bf(ffn2_w_in), bf(ffn2_w_out)
    w_mix = _regroup_mix_weight(w_mix_in)
    w_nsa_out, wm, wo = bf(_pad_nsa_out_weight(w_proj_nsa)), bf(w_proj_mlstm), bf(w_out)
    cmp_k = _compress_weights(cmp_pos_k, cmp_phi_k1, cmp_phi_k2)
    cmp_v = _compress_weights(cmp_pos_v, cmp_phi_v1, cmp_phi_v2)
    mlstm_w = _mlstm_weights(mlstm_conv_w, mlstm_conv_b, mlstm_wq, mlstm_wk, mlstm_gate_bias, mlstm_norm)

    def layer(x3, mix_fn):
        b, t, _ = x3.shape
        x = x3.reshape(b * t, D_MODEL)
        x1 = _ffn(x, norm_ffn1, w1i, w1o, norm_final, final_norm=False)
        proj = _project(x1, norm_mix, w_mix)
        o_nsa, o_mlstm, zmerge, states = mix_fn(proj, b, t)
        x2 = _merge(x1, o_nsa, o_mlstm, zmerge, w_nsa_out, wm, wo)
        y = _ffn(x2, norm_ffn2, w2i, w2o, norm_final, final_norm=True)
        return y.reshape(b, t, D_MODEL), states

    y_prompt, st_p = layer(x_prompt, lambda proj, b, t: _mix_prompt(proj, b, t, cmp_k, cmp_v, mlstm_w, rel_bias))
    y_sample, st_s = layer(x_sample, lambda proj, b, t: _mix_decode(
        proj, b, (cache_k_cmp, cache_v_cmp, cache_k_sel, cache_v_sel, cache_k_win, cache_v_win),
        (state_mlstm_C, state_mlstm_n, state_mlstm_m, state_mlstm_conv), page_table, cmp_k, cmp_v, mlstm_w, rel_bias))
    k_cmp_p, v_cmp_p, k_sel_p, v_sel_p, k_win_p, v_win_p, C_p, n_p, m_p, conv_p = st_p
    k_cmp_s, v_cmp_s, k_sel_s, v_sel_s, k_win_s, v_win_s, C_s, n_s, m_s, conv_s = st_s
    return (y_prompt, y_sample, k_cmp_p, k_cmp_s, v_cmp_p, v_cmp_s, k_sel_p, k_sel_s, v_sel_p, v_sel_s,
            k_win_p, k_win_s, v_win_p, v_win_s, C_p, C_s, n_p, n_s, m_p, m_s, conv_p, conv_s)
```

```python
import math
from functools import partial

import numpy as np
import jax
import jax.numpy as jnp
from jax import lax
from jax.experimental import pallas as pl
from jax.experimental.pallas import tpu as pltpu

D_MODEL = 1024
SEQ = 8192
DEC_SEQ = 1
PAST_LEN = 8192
PAGE_SIZE = 128
N_HEADS = 8
N_KV = 2
GROUP = N_HEADS // N_KV
HEAD_DIM = 64
NSA_WIDTH = N_HEADS * HEAD_DIM
KV_WIDTH = N_KV * HEAD_DIM
L_CMP = 32
D_CMP = 16
CMP_RATIO = L_CMP // D_CMP
L_SEL = 64
K_SEL = 16
WINDOW = 512
Q_BLOCK = 128
FORCE_SCORE = 1000.0
N_BUCKETS = 32
MAX_DIST = 128
M_HEADS = 4
M_DH = 128
M_WIDTH = M_HEADS * M_DH
CONV_W = 4
M_CHUNK = 64
D_FF = 2816
EPS = 1e-6

V7X_VMEM_LIMIT_BYTES = 56 * 1024 * 1024
LANES = 128

QPAD_WIDTH = N_HEADS * LANES
N_SEL_PAD = 128
FAR_TILES = 4
NEG_BIG = -1e30
SEL_OFF = -32768.0

FFN_TOKEN_TILE = 512
FFN_FF_TILE = 1408
PROJ_TOKEN_TILE = 512

_MIX_SIZES = (NSA_WIDTH, KV_WIDTH, KV_WIDTH, KV_WIDTH, KV_WIDTH, KV_WIDTH, KV_WIDTH, 3 * N_HEADS,
              M_WIDTH, M_WIDTH, 2 * M_HEADS, M_WIDTH, 2 * D_MODEL)
_MIX_OFFS = np.concatenate([[0], np.cumsum(_MIX_SIZES)]).tolist()
_PROJ_SEGS = (("kv", 6 * KV_WIDTH), ("mqk", M_WIDTH), ("mv", M_WIDTH),
              ("mo", M_WIDTH), ("merge", 2 * D_MODEL), ("small", LANES))


def _rms(x, g):
    return x * lax.rsqrt(jnp.mean(x * x, axis=-1, keepdims=True) + EPS) * g


def _ffn_kernel(x_ref, g_ref, wg_ref, wu_ref, wo_ref, gf_ref, o_ref, h_ref, acc_ref, *, final_norm):
    j = pl.program_id(1)

    @pl.when(j == 0)
    def _():
        h_ref[...] = _rms(x_ref[...], g_ref[...]).astype(jnp.bfloat16)
        acc_ref[...] = jnp.zeros_like(acc_ref)

    h = h_ref[...]
    gate = jnp.dot(h, wg_ref[...], preferred_element_type=jnp.float32)
    up = jnp.dot(h, wu_ref[...], preferred_element_type=jnp.float32)
    a = (jax.nn.silu(gate) * up).astype(jnp.bfloat16)
    acc_ref[...] += jnp.dot(a, wo_ref[...], preferred_element_type=jnp.float32)

    @pl.when(j == pl.num_programs(1) - 1)
    def _():
        y = x_ref[...] + 0.5 * acc_ref[...]
        if final_norm:
            y = _rms(y, gf_ref[...])
        o_ref[...] = y


def _ffn(x, g, w_in, w_out, g_final, *, final_norm):
    n = x.shape[0]
    tm = min(FFN_TOKEN_TILE, n)
    nj = D_FF // FFN_FF_TILE
    return pl.pallas_call(
        partial(_ffn_kernel, final_norm=final_norm),
        out_shape=jax.ShapeDtypeStruct((n, D_MODEL), jnp.float32),
        grid=(n // tm, nj),
        in_specs=[
            pl.BlockSpec((tm, D_MODEL), lambda i, j: (i, 0)),
            pl.BlockSpec((1, D_MODEL), lambda i, j: (0, 0)),
            pl.BlockSpec((D_MODEL, FFN_FF_TILE), lambda i, j: (0, j)),
            pl.BlockSpec((D_MODEL, FFN_FF_TILE), lambda i, j: (0, j + D_FF // FFN_FF_TILE)),
            pl.BlockSpec((FFN_FF_TILE, D_MODEL), lambda i, j: (j, 0)),
            pl.BlockSpec((1, D_MODEL), lambda i, j: (0, 0)),
        ],
        out_specs=pl.BlockSpec((tm, D_MODEL), lambda i, j: (i, 0)),
        scratch_shapes=[pltpu.VMEM((tm, D_MODEL), jnp.bfloat16), pltpu.VMEM((tm, D_MODEL), jnp.float32)],
        compiler_params=pltpu.CompilerParams(dimension_semantics=("arbitrary", "arbitrary"),
                                             vmem_limit_bytes=V7X_VMEM_LIMIT_BYTES),
        name="ffn",
    )(x, g.reshape(1, D_MODEL), w_in, w_in, w_out, g_final.reshape(1, D_MODEL))


def _proj_kernel(x_ref, g_ref, w_ref, qpad_ref, kvb_ref, *o_refs):
    h = _rms(x_ref[...], g_ref[...]).astype(jnp.bfloat16)
    zq = jnp.dot(h, w_ref[:, :QPAD_WIDTH], preferred_element_type=jnp.float32)
    qpad_ref[...] = (zq * (HEAD_DIM ** -0.5)).astype(jnp.bfloat16)
    off = QPAD_WIDTH
    o_refs = list(o_refs)
    for name, width in _PROJ_SEGS:
        z = jnp.dot(h, w_ref[:, off:off + width], preferred_element_type=jnp.float32)
        if name == "kv":
            kvb_ref[...] = z.astype(jnp.bfloat16)
            for i in range(6):
                o_refs.pop(0)[...] = z[:, i * KV_WIDTH:(i + 1) * KV_WIDTH]
        else:
            o_refs.pop(0)[...] = z
        off += width


def _regroup_mix_weight(w_mix_in):
    seg = lambda i: w_mix_in[:, _MIX_OFFS[i]:_MIX_OFFS[i + 1]]
    wq = seg(0)
    zeros = jnp.zeros((D_MODEL, HEAD_DIM), w_mix_in.dtype)
    qpad = []
    for h in range(N_HEADS):
        wh = wq[:, h * HEAD_DIM:(h + 1) * HEAD_DIM]
        qpad += [wh, zeros] if h // GROUP == 0 else [zeros, wh]
    small = jnp.concatenate([seg(7), seg(10), jnp.zeros((D_MODEL, LANES - 3 * N_HEADS - 2 * M_HEADS), w_mix_in.dtype)], axis=1)
    cols = qpad + [seg(i) for i in range(1, 7)] + [seg(8), seg(9), seg(11), seg(12), small]
    return jnp.concatenate(cols, axis=1).astype(jnp.bfloat16)


def _project(x, g, w_regrouped):
    n = x.shape[0]
    tm = min(PROJ_TOKEN_TILE, n)
    wcols = w_regrouped.shape[1]
    row = lambda width: pl.BlockSpec((tm, width), lambda i: (i, 0))
    widths = []
    for name, width in _PROJ_SEGS:
        widths += [KV_WIDTH] * 6 if name == "kv" else [width]
    return pl.pallas_call(
        _proj_kernel,
        out_shape=[jax.ShapeDtypeStruct((n, QPAD_WIDTH), jnp.bfloat16),
                   jax.ShapeDtypeStruct((n, 6 * KV_WIDTH), jnp.bfloat16)]
                  + [jax.ShapeDtypeStruct((n, width), jnp.float32) for width in widths],
        grid=(n // tm,),
        in_specs=[
            row(D_MODEL),
            pl.BlockSpec((1, D_MODEL), lambda i: (0, 0)),
            pl.BlockSpec((D_MODEL, wcols), lambda i: (0, 0)),
        ],
        out_specs=[row(QPAD_WIDTH), row(6 * KV_WIDTH)] + [row(width) for width in widths],
        compiler_params=pltpu.CompilerParams(dimension_semantics=("arbitrary",),
                                             vmem_limit_bytes=V7X_VMEM_LIMIT_BYTES),
        name="mix_in_proj",
    )(x, g.reshape(1, D_MODEL), w_regrouped)


def _dot_nt(a, b):
    return lax.dot_general(a, b, (((1,), (1,)), ((), ())), preferred_element_type=jnp.float32)


def _bucket_table():
    d = np.arange(MAX_DIST + 1)
    max_exact = N_BUCKETS // 2
    nf = np.maximum(d, 1).astype(np.float64)
    large = max_exact + (np.log(nf / max_exact) / math.log(MAX_DIST / max_exact) * (N_BUCKETS - max_exact)).astype(np.int64)
    return np.where(d < max_exact, d, np.minimum(large, N_BUCKETS - 1)).astype(np.int32)


def _bias_tables(rel_bias, n_c):
    tab = rel_bias.astype(jnp.float32)[_bucket_table()].T
    i = np.arange(Q_BLOCK)[:, None]
    j = np.arange(Q_BLOCK)[None, :]
    far = tab[:, MAX_DIST][:, None, None]
    t0 = tab[:, np.clip(i - j, 0, MAX_DIST)] - far
    t1 = tab[:, np.minimum(Q_BLOCK + i - j, MAX_DIST)] - far
    tiles = jnp.stack([t0, t1]).reshape(2, N_HEADS * Q_BLOCK, Q_BLOCK)
    lo, hi = -2 * Q_BLOCK // D_CMP, Q_BLOCK // D_CMP
    dist = np.arange(Q_BLOCK)[None, :] - (L_CMP - 1) - D_CMP * np.arange(lo, hi)[:, None]
    band = jnp.where(jnp.asarray(dist >= 0), tab[:, np.clip(dist, 0, MAX_DIST)], NEG_BIG)
    bc = jnp.concatenate([jnp.broadcast_to(tab[:, MAX_DIST][:, None, None], (N_HEADS, n_c + lo, Q_BLOCK)), band,
                          jnp.full((N_HEADS, n_c - hi, Q_BLOCK), NEG_BIG, jnp.float32)], axis=1)
    bct = bc.transpose(1, 0, 2).reshape(2 * n_c, N_HEADS * Q_BLOCK)
    return tiles, bct


def _nsa_prompt_kernel(q_ref, gate_ref, kc_ref, vct_ref, ks_ref, vs_ref, kw_ref, vw_ref, tiles_ref, bct_ref,
                       o_ref, m_ref, l_ref, acc_ref, *, n_c):
    f32, bf16 = jnp.float32, jnp.bfloat16
    qb = pl.program_id(1)
    rows_all = N_HEADS * Q_BLOCK
    qi = lax.broadcasted_iota(jnp.int32, (rows_all, Q_BLOCK), 0) & (Q_BLOCK - 1)
    kj = lax.broadcasted_iota(jnp.int32, (rows_all, Q_BLOCK), 1)
    causal = qi >= kj
    window_edge = kj >= qi
    sig = jax.nn.sigmoid(gate_ref[...])

    s_io = lax.broadcasted_iota(jnp.int32, (N_SEL_PAD, N_KV * Q_BLOCK), 0)
    i_io = lax.broadcasted_iota(jnp.int32, (N_SEL_PAD, N_KV * Q_BLOCK), 1) & (Q_BLOCK - 1)
    qpos = qb * Q_BLOCK + i_io
    cur = 2 * qb + (i_io >= L_SEL).astype(jnp.int32)
    blk_valid = s_io * L_SEL <= qpos
    blk_forced = (s_io == 0) | (s_io == cur) | (s_io == cur - 1)
    ov_s = lax.broadcasted_iota(jnp.int32, (N_SEL_PAD, n_c), 0)
    ov_c = lax.broadcasted_iota(jnp.int32, (N_SEL_PAD, n_c), 1)
    overlap_t = jnp.where((ov_c >= 4 * ov_s - 1) & (ov_c <= 4 * ov_s + 3), 1.0, 0.0).astype(bf16)

    def flash_init():
        m_ref[...] = jnp.full(m_ref.shape, NEG_BIG, f32)
        l_ref[...] = jnp.zeros(l_ref.shape, f32)
        acc_ref[...] = jnp.zeros(acc_ref.shape, f32)

    def flash_step(s, v_tile):
        m_old = m_ref[...]
        m_new = jnp.maximum(m_old, jnp.max(s, axis=-1, keepdims=True))
        alpha = jnp.exp(m_old - m_new)
        p = jnp.exp(s - jnp.concatenate([m_new] * (s.shape[1] // LANES), axis=1))
        l_ref[...] = alpha * l_ref[...] + jnp.sum(p, axis=-1, keepdims=True)
        acc_ref[...] = alpha * acc_ref[...] + jnp.dot(p.astype(bf16), v_tile, preferred_element_type=f32)
        m_ref[...] = m_new

    def key_rows(kt, n_tiles=1):
        return pl.ds(pl.multiple_of(kt * Q_BLOCK, Q_BLOCK), n_tiles * Q_BLOCK)

    q_all = jnp.concatenate([q_ref[:, h * LANES:(h + 1) * LANES] for h in range(N_HEADS)], axis=0)

    st = _dot_nt(kc_ref[0], q_all)
    st = st + bct_ref[pl.ds(pl.multiple_of(n_c - 8 * qb, 8), n_c), :]
    mx = jnp.max(st, axis=0, keepdims=True)
    mx = jnp.where(mx < 0.1 * NEG_BIG, 0.0, mx)
    p = jnp.exp(st - mx)
    p = p / jnp.maximum(jnp.sum(p, axis=0, keepdims=True), 1e-30)
    pb = p.astype(bf16)
    oc_t = jnp.dot(vct_ref[0], pb, preferred_element_type=f32)

    imps = []
    for g in range(N_KV):
        imp_g = None
        for r in range(GROUP):
            h = GROUP * g + r
            part = jnp.dot(overlap_t, pb[:, h * Q_BLOCK:(h + 1) * Q_BLOCK], preferred_element_type=f32)
            imp_g = part if imp_g is None else imp_g + part
        imps.append(imp_g)
    imp = jnp.concatenate(imps, axis=1)
    imp = jnp.where(blk_valid, imp + jnp.where(blk_forced, FORCE_SCORE, 0.0), -jnp.inf)

    s_f = s_io.astype(f32)

    def pick_body(_, carry):
        vals, sel_off = carry
        best = jnp.max(vals, axis=0, keepdims=True)
        first = jnp.min(jnp.where(vals == best, s_f, float(N_SEL_PAD)), axis=0, keepdims=True)
        hit = s_f == first
        return jnp.where(hit, -jnp.inf, vals), jnp.where(hit, 0.0, sel_off)

    _, sel_off = lax.fori_loop(0, K_SEL, pick_body, (imp, jnp.full(imp.shape, SEL_OFF, f32)))
    sel_rows = []
    for g in range(N_KV):
        sel_rows += [sel_off[:, g * Q_BLOCK:(g + 1) * Q_BLOCK].T.astype(bf16)] * GROUP
    lhs_sel = jnp.concatenate([jnp.concatenate(sel_rows, axis=0), q_all], axis=1)

    def sel_scores(kt, n_tiles=1):
        key_blk = jnp.right_shift(lax.broadcasted_iota(jnp.int32, (n_tiles * Q_BLOCK, N_SEL_PAD), 0),
                                  L_SEL.bit_length() - 1)
        blk_lane = lax.broadcasted_iota(jnp.int32, (n_tiles * Q_BLOCK, N_SEL_PAD), 1)
        onehot = jnp.where(blk_lane == 2 * kt + key_blk, 1.0, 0.0).astype(bf16)
        rhs = jnp.concatenate([onehot, ks_ref[0, key_rows(kt, n_tiles), :]], axis=1)
        return _dot_nt(lhs_sel, rhs)

    flash_init()

    n_far = jnp.maximum(qb - 1, 0)

    def far_step(kt, n_tiles):
        flash_step(sel_scores(kt, n_tiles), vs_ref[0, key_rows(kt, n_tiles), :])

    def far_body(i, carry):
        far_step(FAR_TILES * i, FAR_TILES)
        return carry

    lax.fori_loop(0, n_far // FAR_TILES, far_body, 0)
    width = FAR_TILES // 2
    while width >= 1:
        @pl.when((n_far & width) != 0)
        def _(width=width):
            far_step((n_far // (2 * width)) * (2 * width), width)
        width //= 2

    @pl.when(qb >= 1)
    def _():
        flash_step(sel_scores(qb - 1) + tiles_ref[1], vs_ref[0, key_rows(qb - 1), :])

    flash_step(jnp.where(causal, sel_scores(qb) + tiles_ref[0], NEG_BIG), vs_ref[0, key_rows(qb), :])
    o_sel = acc_ref[...] / l_ref[...]

    scores, values = [], []
    for dt in range(WINDOW // Q_BLOCK + 1):
        rows = key_rows(jnp.maximum(qb - dt, 0))
        s = _dot_nt(q_all, kw_ref[0, rows, :])
        if dt < 2:
            s = s + tiles_ref[dt]
        if dt == 0:
            s = jnp.where(causal, s, NEG_BIG)
        else:
            visible = window_edge if dt == WINDOW // Q_BLOCK else True
            s = jnp.where(visible & (qb >= dt), s, NEG_BIG)
        scores.append(s)
        values.append(vw_ref[0, rows, :])
    s = jnp.concatenate(scores, axis=1)
    p = jnp.exp(s - jnp.max(s, axis=-1, keepdims=True))
    o_win = (jnp.dot(p.astype(bf16), jnp.concatenate(values, axis=0), preferred_element_type=f32)
             / jnp.sum(p, axis=-1, keepdims=True))

    for h in range(N_HEADS):
        rs = slice(h * Q_BLOCK, (h + 1) * Q_BLOCK)
        o = (oc_t[:, rs].T * sig[:, 3 * h:3 * h + 1] + o_sel[rs] * sig[:, 3 * h + 1:3 * h + 2]
             + o_win[rs] * sig[:, 3 * h + 2:3 * h + 3])
        o_ref[:, h * LANES:(h + 1) * LANES] = o.astype(bf16)


def _nsa_prompt(qpad, small, kvb, kcb, vct, rel_bias, b, t):
    n_c = t // D_CMP
    n_qb = t // Q_BLOCK
    tiles, bct = _bias_tables(rel_bias, n_c)
    kv3 = kvb.reshape(b, t, 6 * KV_WIDTH)
    rows = lambda width: pl.BlockSpec((Q_BLOCK, width), lambda bi, qi: (bi * n_qb + qi, 0))
    seq = lambda lane_block: pl.BlockSpec((1, t, KV_WIDTH), lambda bi, qi: (bi, 0, lane_block))
    full = lambda a: pl.BlockSpec(a.shape, lambda bi, qi: (0,) * a.ndim)
    rows_all = N_HEADS * Q_BLOCK
    return pl.pallas_call(
        partial(_nsa_prompt_kernel, n_c=n_c),
        out_shape=jax.ShapeDtypeStruct((b * t, QPAD_WIDTH), jnp.bfloat16),
        grid=(b, n_qb),
        in_specs=[rows(QPAD_WIDTH), rows(LANES),
                  pl.BlockSpec((1, n_c, KV_WIDTH), lambda bi, qi: (bi, 0, 0)),
                  pl.BlockSpec((1, KV_WIDTH, n_c), lambda bi, qi: (bi, 0, 0)),
                  seq(2), seq(3), seq(4), seq(5), full(tiles), full(bct)],
        out_specs=rows(QPAD_WIDTH),
        scratch_shapes=[pltpu.VMEM((rows_all, LANES), jnp.float32), pltpu.VMEM((rows_all, LANES), jnp.float32),
                        pltpu.VMEM((rows_all, KV_WIDTH), jnp.float32)],
        compiler_params=pltpu.CompilerParams(dimension_semantics=("arbitrary", "arbitrary"),
                                             vmem_limit_bytes=V7X_VMEM_LIMIT_BYTES),
        name="nsa_prompt",
    )(qpad, small, kcb, vct, kv3, kv3, kv3, kv3, tiles, bct)


def _pad_nsa_out_weight(w_proj_nsa):
    zeros = jnp.zeros((HEAD_DIM, D_MODEL), w_proj_nsa.dtype)
    rows = []
    for h in range(N_HEADS):
        wh = w_proj_nsa[h * HEAD_DIM:(h + 1) * HEAD_DIM]
        rows += [wh, zeros] if h // GROUP == 0 else [zeros, wh]
    return jnp.concatenate(rows, axis=0)


SUB_WIDTH = D_CMP * KV_WIDTH
PHI_HIDDEN = 2 * HEAD_DIM
PAGE_SUBS = PAGE_SIZE // D_CMP
CMP_PAGES_PER_STEP = 32


def _compress_weights(pos_emb, w1, w2):
    eye = jnp.eye(N_KV, dtype=w1.dtype)
    halves = []
    for r in range(CMP_RATIO):
        w1r = w1[r * D_CMP:(r + 1) * D_CMP]
        halves.append(jnp.einsum('ldh,gk->lgdkh', w1r, eye).reshape(SUB_WIDTH, N_KV * PHI_HIDDEN))
    w1big = jnp.concatenate(halves, axis=1).astype(jnp.bfloat16)
    pos = jnp.broadcast_to(pos_emb.reshape(CMP_RATIO, D_CMP, 1, HEAD_DIM), (CMP_RATIO, D_CMP, N_KV, HEAD_DIM))
    w2big = jnp.einsum('hd,gk->ghkd', w2, eye).reshape(N_KV * PHI_HIDDEN, KV_WIDTH).astype(jnp.bfloat16)
    return w1big, pos.reshape(CMP_RATIO, 1, SUB_WIDTH), w2big


def _compress_tokens(lhs0, lhs1, w1big, w2big):
    hw = N_KV * PHI_HIDDEN
    p0 = jnp.dot(lhs0, w1big[:, :hw], preferred_element_type=jnp.float32)
    p1 = jnp.dot(lhs1, w1big[:, hw:], preferred_element_type=jnp.float32)
    n_sub = p1.shape[0]
    hidden = p0 + pltpu.roll(p1, n_sub - 1, 0)
    return jnp.dot(jax.nn.gelu(hidden).astype(jnp.bfloat16), w2big, preferred_element_type=jnp.float32)


def _compress_rows_kernel(rk_ref, rv_ref, w1k_ref, pk_ref, w2k_ref, w1v_ref, pv_ref, w2v_ref, kc_ref, vc_ref):
    for r_ref, w1_ref, p_ref, w2_ref, o_ref in ((rk_ref, w1k_ref, pk_ref, w2k_ref, kc_ref),
                                                 (rv_ref, w1v_ref, pv_ref, w2v_ref, vc_ref)):
        x = r_ref[0]
        lhs = [(x + p_ref[r]).astype(jnp.bfloat16) for r in range(CMP_RATIO)]
        o_ref[0] = _compress_tokens(lhs[0], lhs[1], w1_ref[...], w2_ref[...]).astype(o_ref.dtype)


def _compress_rows(k_rows, v_rows, b, t, wk, wv):
    n_sub = t // D_CMP
    view = lambda a: a.reshape(b, n_sub, SUB_WIDTH)
    seq = pl.BlockSpec((1, n_sub, SUB_WIDTH), lambda bi: (bi, 0, 0))
    full = lambda a: pl.BlockSpec(a.shape, lambda bi: (0,) * a.ndim)
    out = pl.BlockSpec((1, n_sub, KV_WIDTH), lambda bi: (bi, 0, 0))
    return pl.pallas_call(
        _compress_rows_kernel,
        out_shape=[jax.ShapeDtypeStruct((b, n_sub, KV_WIDTH), jnp.bfloat16)] * 2,
        grid=(b,),
        in_specs=[seq, seq] + [full(a) for a in (*wk, *wv)],
        out_specs=[out, out],
        compiler_params=pltpu.CompilerParams(dimension_semantics=("arbitrary",),
                                             vmem_limit_bytes=V7X_VMEM_LIMIT_BYTES),
        name="compress_rows",
    )(view(k_rows), view(v_rows), *wk, *wv)


def _compress_pages_kernel(pt_ref, *refs):
    n_pg = CMP_PAGES_PER_STEP
    k_pages, v_pages = refs[:n_pg], refs[n_pg:2 * n_pg]
    w1k_ref, pk_ref, w2k_ref, w1v_ref, pv_ref, w2v_ref, kc_ref, vc_ref, lhs_ref, rows_ref = refs[2 * n_pg:]
    step = pl.program_id(1)
    for c, (pages, p_ref) in enumerate(((k_pages, pk_ref), (v_pages, pv_ref))):
        for j in range(0, n_pg, 2):
            rows_ref[c, j] = pages[j][0].T
            rows_ref[c, j + 1] = pages[j + 1][0].T
            rows = pl.ds(pl.multiple_of(step * n_pg * PAGE_SUBS + j * PAGE_SUBS, 2 * PAGE_SUBS), 2 * PAGE_SUBS)
            for l in range(D_CMP):
                x = jnp.concatenate([rows_ref[c, j, pl.ds(l, PAGE_SUBS, stride=D_CMP), :],
                                     rows_ref[c, j + 1, pl.ds(l, PAGE_SUBS, stride=D_CMP), :]], axis=0)
                lanes = slice(l * KV_WIDTH, (l + 1) * KV_WIDTH)
                for r in range(CMP_RATIO):
                    lhs_ref[c, r, rows, lanes] = (x + p_ref[r, :, lanes]).astype(jnp.bfloat16)

    @pl.when(step == pl.num_programs(1) - 1)
    def _():
        for c, (w1_ref, w2_ref, o_ref) in enumerate(((w1k_ref, w2k_ref, kc_ref), (w1v_ref, w2v_ref, vc_ref))):
            o_ref[0] = _compress_tokens(lhs_ref[c, 0], lhs_ref[c, 1], w1_ref[...], w2_ref[...]).astype(o_ref.dtype)


def _compress_pages(cache_k, cache_v, page_table, wk, wv):
    b, n_pages = page_table.shape
    n_pool = cache_k.shape[0]
    n_sub = n_pages * PAGE_SUBS
    n_pg = CMP_PAGES_PER_STEP
    view = lambda c: c.transpose(0, 2, 3, 1).reshape(n_pool, KV_WIDTH, PAGE_SIZE)
    page = lambda j: pl.BlockSpec(
        (1, KV_WIDTH, PAGE_SIZE),
        lambda bi, si, pt: (jnp.clip(pt[jnp.minimum(bi, b - 1), jnp.minimum(si, n_pages // n_pg - 1) * n_pg + j],
                                     0, n_pool - 1), 0, 0))
    full = lambda a: pl.BlockSpec(a.shape, lambda bi, si, pt: (0,) * a.ndim)
    out = pl.BlockSpec((1, n_sub, KV_WIDTH), lambda bi, si, pt: (bi, 0, 0))
    return pl.pallas_call(
        _compress_pages_kernel,
        out_shape=[jax.ShapeDtypeStruct((b, n_sub, KV_WIDTH), jnp.bfloat16)] * 2,
        grid_spec=pltpu.PrefetchScalarGridSpec(
            num_scalar_prefetch=1,
            grid=(b, n_pages // n_pg),
            in_specs=[page(j) for j in range(n_pg)] * 2 + [full(a) for a in (*wk, *wv)],
            out_specs=[out, out],
            scratch_shapes=[pltpu.VMEM((2, CMP_RATIO, n_sub, SUB_WIDTH), jnp.bfloat16),
                            pltpu.VMEM((2, n_pg, PAGE_SIZE, KV_WIDTH), jnp.float32)]),
        compiler_params=pltpu.CompilerParams(dimension_semantics=("arbitrary", "arbitrary"),
                                             vmem_limit_bytes=V7X_VMEM_LIMIT_BYTES),
        name="compress_pages",
    )(page_table, *([view(cache_k)] * n_pg), *([view(cache_v)] * n_pg), *wk, *wv)


N_SEL_DEC = PAST_LEN // L_SEL + 1
N_SEL_DEC_PAD = 256
CUR_BLOCK = PAST_LEN // L_SEL


def _nsa_decode_cmp_kernel(q_ref, kc_ref, vc_ref, bias_ref, oc_ref, imp_ref):
    f32, bf16 = jnp.float32, jnp.bfloat16
    n_c = kc_ref.shape[1]
    s = _dot_nt(q_ref[0], kc_ref[0]) + bias_ref[...]
    mx = jnp.max(s, axis=-1, keepdims=True)
    mx = jnp.where(mx < 0.1 * NEG_BIG, 0.0, mx)
    p = jnp.exp(s - mx)
    p = p / jnp.maximum(jnp.sum(p, axis=-1, keepdims=True), 1e-30)
    pb = p.astype(bf16)
    oc_ref[0] = jnp.dot(pb, vc_ref[0], preferred_element_type=f32)
    ov_c = lax.broadcasted_iota(jnp.int32, (n_c, N_SEL_DEC_PAD), 0)
    ov_s = lax.broadcasted_iota(jnp.int32, (n_c, N_SEL_DEC_PAD), 1)
    overlap = jnp.where((ov_c >= 4 * ov_s - 1) & (ov_c <= 4 * ov_s + 3), 1.0, 0.0).astype(bf16)
    imp_h = jnp.dot(pb, overlap, preferred_element_type=f32)
    head = lax.broadcasted_iota(jnp.int32, imp_h.shape, 0)
    imp_ref[0] = jnp.concatenate(
        [jnp.sum(jnp.where((head >= g * GROUP) & (head < (g + 1) * GROUP), imp_h, 0.0), axis=0, keepdims=True)
         for g in range(N_KV)], axis=1)


def _nsa_decode_topk_kernel(imp_ref, idx_ref, val_ref):
    f32 = jnp.float32
    nb = imp_ref.shape[0]
    s_io = lax.broadcasted_iota(jnp.int32, (N_SEL_DEC_PAD, nb), 0)
    visible = s_io < N_SEL_DEC
    forced = (s_io == 0) | (s_io == CUR_BLOCK) | (s_io == CUR_BLOCK - 1)
    tri_r = lax.broadcasted_iota(jnp.int32, (N_SEL_DEC_PAD, N_SEL_DEC_PAD), 0)
    tri_c = lax.broadcasted_iota(jnp.int32, (N_SEL_DEC_PAD, N_SEL_DEC_PAD), 1)
    before = jnp.where(tri_c < tri_r, 1.0, 0.0).astype(jnp.bfloat16)
    for g in range(N_KV):
        x = imp_ref[:, g * N_SEL_DEC_PAD:(g + 1) * N_SEL_DEC_PAD]
        xt = jnp.concatenate([x[:, i * LANES:(i + 1) * LANES].T for i in range(N_SEL_DEC_PAD // LANES)], axis=0)
        val = jnp.where(visible, xt + jnp.where(forced, FORCE_SCORE, 0.0), -jnp.inf)
        val_ref[...] = val

        def rank_body(s, rank):
            other = val_ref[pl.ds(s, 1), :]
            beats = (other > val) | ((other == val) & (s_io > s))
            return rank + jnp.where(beats, 1, 0)

        rank = lax.fori_loop(0, N_SEL_DEC, rank_body, jnp.zeros(val.shape, jnp.int32))
        sel = (rank < K_SEL) & visible
        n_before = jnp.dot(before, jnp.where(sel, 1.0, 0.0).astype(jnp.bfloat16), preferred_element_type=f32)
        for k in range(K_SEL):
            hit = sel & (n_before == float(k))
            idx_ref[g, pl.ds(k, 1), :] = jnp.sum(jnp.where(hit, s_io.astype(f32), 0.0), axis=0,
                                                 keepdims=True).astype(jnp.int32)


def _nsa_decode_sel_kernel(pt_ref, ix_ref, *refs):
    f32, bf16 = jnp.float32, jnp.bfloat16
    n_blk = N_KV * K_SEL
    k_blocks, v_blocks = refs[:n_blk], refs[n_blk:2 * n_blk]
    (q_ref, oc_ref, gl_ref, ksn_ref, vsn_ref, kwn_ref, vwn_ref, kwin_ref, vwin_ref, near_ref, b0_ref, bw_ref,
     o_ref, kwo_ref, vwo_ref) = refs[2 * n_blk:]
    b = pl.program_id(0)
    q = q_ref[0]
    qf = q.astype(f32)
    round_bf = lambda a: a.astype(bf16).astype(f32)
    group1 = lax.broadcasted_iota(jnp.int32, (N_HEADS, LANES), 0) >= GROUP
    second_half = lax.broadcasted_iota(jnp.int32, (N_HEADS, LANES), 1) >= L_SEL
    neg_tile = jnp.full((N_HEADS, LANES), NEG_BIG, f32)
    no_rows = jnp.zeros((HEAD_DIM, PAGE_SIZE), bf16)

    def block_bias(s):
        tile = jnp.where(s == CUR_BLOCK, neg_tile,
                         jnp.where(s == CUR_BLOCK - 1, near_ref[1],
                                   jnp.where(s == CUR_BLOCK - 2, near_ref[0], near_ref[2])))
        return jnp.where(second_half == ((s & 1) == 1), tile, neg_tile)

    def group_rows(x, g):
        x = x.astype(bf16)
        return jnp.concatenate([x, no_rows] if g == 0 else [no_rows, x], axis=0)

    scores, values, has_new = [], [], []
    for g in range(N_KV):
        ids = [ix_ref[b, g * K_SEL + k] for k in range(K_SEL)]
        kt = jnp.concatenate([group_rows(k_blocks[g * K_SEL + k][0], g) for k in range(K_SEL)], axis=1)
        values.append(jnp.concatenate([group_rows(v_blocks[g * K_SEL + k][0], g) for k in range(K_SEL)], axis=1))
        bias = jnp.concatenate([block_bias(ids[k]) for k in range(K_SEL)], axis=1)
        scores.append(jnp.dot(q, kt, preferred_element_type=f32) + bias)
        flag = ids[0] == CUR_BLOCK
        for k in range(1, K_SEL):
            flag = flag | (ids[k] == CUR_BLOCK)
        has_new.append(jnp.where(flag, 0.0, NEG_BIG))
    s = jnp.where(group1[:, :1], scores[1], scores[0])
    s_new = jnp.sum(qf * round_bf(ksn_ref[0]), axis=-1, keepdims=True) + b0_ref[...]
    s_new = s_new + jnp.where(group1, has_new[1], has_new[0])
    m = jnp.maximum(jnp.max(s, axis=-1, keepdims=True), s_new)
    p = jnp.exp(s - m[:, :1])
    p_new = jnp.exp(s_new - m)
    l = jnp.sum(p, axis=-1, keepdims=True) + p_new
    pb = p.astype(bf16)
    o_sel = jnp.where(group1, _dot_nt(pb, values[1]), _dot_nt(pb, values[0]))
    o_sel = (o_sel + round_bf(p_new) * round_bf(vsn_ref[0])) / l

    kwin, vwin = kwin_ref[0], vwin_ref[0]
    s = jnp.dot(q, kwin.astype(bf16), preferred_element_type=f32) + bw_ref[...]
    s_new = jnp.sum(qf * round_bf(kwn_ref[0]), axis=-1, keepdims=True) + b0_ref[...]
    m = jnp.maximum(jnp.max(s, axis=-1, keepdims=True), s_new)
    p = jnp.exp(s - m[:, :1])
    p_new = jnp.exp(s_new - m)
    l = jnp.sum(p, axis=-1, keepdims=True) + p_new
    o_win = _dot_nt(p.astype(bf16), vwin.astype(bf16))
    o_win = (o_win + round_bf(p_new) * round_bf(vwn_ref[0])) / l

    gates = jax.nn.sigmoid(gl_ref[0])
    o_ref[0] = (oc_ref[0] * gates[0] + o_sel * gates[1] + o_win * gates[2]).astype(bf16)

    wb = kwin.shape[1]
    last = lax.broadcasted_iota(jnp.int32, kwin.shape, 1) == wb - 1
    as_column = lambda row: jnp.broadcast_to(row, (KV_WIDTH, KV_WIDTH)).T[:, :1]
    kwo_ref[0] = jnp.where(last, as_column(kwn_ref[0]), pltpu.roll(kwin, wb - 1, 1))
    vwo_ref[0] = jnp.where(last, as_column(vwn_ref[0]), pltpu.roll(vwin, wb - 1, 1))


def _nsa_decode(qpad, small, kc, vc, ks_new, vs_new, kw_new, vw_new, cache_k_sel, cache_v_sel,
                cache_k_win, cache_v_win, page_table, rel_bias):
    f32 = jnp.float32
    b = qpad.shape[0]
    n_c = kc.shape[1]
    n_pool = cache_k_sel.shape[0]
    wb = cache_k_win.shape[1]
    tab = rel_bias.astype(f32)[_bucket_table()].T
    dist_c = PAST_LEN - (L_CMP - 1) - D_CMP * np.arange(n_c)
    bias_c = jnp.where(jnp.asarray(dist_c >= 0)[None, :], tab[:, np.clip(dist_c, 0, MAX_DIST)], NEG_BIG)
    j = np.arange(LANES) % L_SEL
    near = jnp.stack([tab[:, 2 * L_SEL - j], tab[:, L_SEL - j],
                      jnp.broadcast_to(tab[:, MAX_DIST:], (N_HEADS, LANES))])
    bias0 = jnp.broadcast_to(tab[:, :1], (N_HEADS, LANES))
    bias_w = tab[:, np.minimum(wb - np.arange(wb), MAX_DIST)]
    q3 = qpad.reshape(b, N_HEADS, LANES)

    per_seq = lambda *shape: pl.BlockSpec((1,) + shape, lambda bi, *_: (bi,) + (0,) * len(shape))
    full = lambda a: pl.BlockSpec(a.shape, lambda bi, *_: (0,) * a.ndim)
    oc, imp = pl.pallas_call(
        _nsa_decode_cmp_kernel,
        out_shape=[jax.ShapeDtypeStruct((b, N_HEADS, LANES), f32),
                   jax.ShapeDtypeStruct((b, 1, N_KV * N_SEL_DEC_PAD), f32)],
        grid=(b,),
        in_specs=[per_seq(N_HEADS, LANES), per_seq(n_c, KV_WIDTH), per_seq(n_c, KV_WIDTH), full(bias_c)],
        out_specs=[per_seq(N_HEADS, LANES), per_seq(1, N_KV * N_SEL_DEC_PAD)],
        compiler_params=pltpu.CompilerParams(dimension_semantics=("arbitrary",)),
        name="nsa_decode_cmp",
    )(q3, kc, vc, bias_c)

    idx = pl.pallas_call(
        _nsa_decode_topk_kernel,
        out_shape=jax.ShapeDtypeStruct((N_KV, K_SEL, b), jnp.int32),
        scratch_shapes=[pltpu.VMEM((N_SEL_DEC_PAD, b), f32)],
        name="nsa_decode_topk",
    )(imp.reshape(b, N_KV * N_SEL_DEC_PAD))
    idx = idx.reshape(N_KV * K_SEL, b).T

    halves = PAGE_SIZE // L_SEL

    def sel_block(i):
        def index_map(bi, pt, ix):
            bs = jnp.minimum(bi, b - 1)
            s = jnp.clip(ix[bs, i], 0, CUR_BLOCK - 1)
            return (jnp.clip(pt[bs, s // halves], 0, n_pool - 1) * N_KV + i // K_SEL, 0, 0)
        return pl.BlockSpec((1, HEAD_DIM, PAGE_SIZE), index_map)

    keys_minor = lambda c: c.transpose(0, 2, 3, 1)
    page_groups = lambda c: keys_minor(c).reshape(n_pool * N_KV, HEAD_DIM, PAGE_SIZE)
    win_t = lambda c: keys_minor(c).reshape(b, KV_WIDTH, wb)
    glog = jnp.broadcast_to(small[:, :3 * N_HEADS].reshape(b, N_HEADS, 3).transpose(0, 2, 1)[..., None],
                            (b, 3, N_HEADS, LANES))
    new_row = lambda a: a.reshape(b, 1, KV_WIDTH)
    n_blk = N_KV * K_SEL
    o_pad, k_win, v_win = pl.pallas_call(
        _nsa_decode_sel_kernel,
        out_shape=[jax.ShapeDtypeStruct((b, N_HEADS, LANES), jnp.bfloat16),
                   jax.ShapeDtypeStruct((b, KV_WIDTH, wb), f32), jax.ShapeDtypeStruct((b, KV_WIDTH, wb), f32)],
        grid_spec=pltpu.PrefetchScalarGridSpec(
            num_scalar_prefetch=2,
            grid=(b,),
            in_specs=[sel_block(i) for i in range(n_blk)] * 2
                     + [per_seq(N_HEADS, LANES), per_seq(N_HEADS, LANES), per_seq(3, N_HEADS, LANES)]
                     + [per_seq(1, KV_WIDTH)] * 4 + [per_seq(KV_WIDTH, wb)] * 2
                     + [full(near), full(bias0), full(bias_w)],
            out_specs=[per_seq(N_HEADS, LANES), per_seq(KV_WIDTH, wb), per_seq(KV_WIDTH, wb)]),
        compiler_params=pltpu.CompilerParams(dimension_semantics=("arbitrary",)),
        name="nsa_decode_sel",
    )(page_table, idx, *([page_groups(cache_k_sel)] * n_blk), *([page_groups(cache_v_sel)] * n_blk),
      q3, oc, glog, new_row(ks_new), new_row(vs_new), new_row(kw_new), new_row(vw_new),
      win_t(cache_k_win), win_t(cache_v_win), near, bias0, bias_w)
    rows_major = lambda a: a.reshape(b, N_KV, HEAD_DIM, wb).transpose(0, 3, 1, 2)
    return o_pad.reshape(b, QPAD_WIDTH), rows_major(k_win), rows_major(v_win)


MLSTM_CHUNK = 128
MLSTM_DEC_TILE = 8
CONV_TAIL = 8


def _mlstm_norm_gate(h, zo, norm_g):
    return h * lax.rsqrt(jnp.mean(h * h, axis=-1, keepdims=True) + EPS) * norm_g * jax.nn.sigmoid(zo)


def _mlstm_prompt_kernel(zqk_ref, zv_ref, zo_ref, sm_ref, cw_ref, cb_ref, wq_ref, wk_ref, gb_ref, ng_ref,
                         o_ref, co_ref, no_ref, mo_ref, xbuf_ref, c_ref, n_ref, m_ref):
    f32, bf16 = jnp.float32, jnp.bfloat16
    L = MLSTM_CHUNK
    c = pl.program_id(1)

    @pl.when(c == 0)
    def _():
        xbuf_ref[:CONV_TAIL] = jnp.zeros((CONV_TAIL, M_WIDTH), f32)
        c_ref[...] = jnp.zeros(c_ref.shape, f32)
        n_ref[...] = jnp.zeros(n_ref.shape, f32)
        m_ref[...] = jnp.zeros(m_ref.shape, f32)

    x = zqk_ref[...]
    xbuf_ref[CONV_TAIL:] = x
    conv = cb_ref[...]
    for j in range(CONV_W):
        conv = conv + xbuf_ref[pl.ds(CONV_TAIL - (CONV_W - 1) + j, L), :] * cw_ref[j:j + 1, :]
    xbuf_ref[:CONV_TAIL] = x[L - CONV_TAIL:]
    a = jax.nn.silu(conv).astype(bf16)

    t_io = lax.broadcasted_iota(jnp.int32, (L, L), 0)
    s_io = lax.broadcasted_iota(jnp.int32, (L, L), 1)
    causal = t_io >= s_io
    sm = sm_ref[...]
    for h in range(M_HEADS):
        hs = slice(h * M_DH, (h + 1) * M_DH)
        q = jnp.dot(a[:, hs], wq_ref[h], preferred_element_type=f32)
        k = jnp.dot(a[:, hs], wk_ref[h], preferred_element_type=f32) * (M_DH ** -0.5)
        v = zv_ref[:, hs]
        qb, kb, vb = q.astype(bf16), k.astype(bf16), v.astype(bf16)
        col = 3 * N_HEADS + h
        ig = jnp.broadcast_to(sm[:, col:col + 1], (L, L)) + gb_ref[0:1, hs]
        lf = jax.nn.log_sigmoid(jnp.broadcast_to(sm[:, col + M_HEADS:col + M_HEADS + 1], (L, L)) + gb_ref[1:2, hs])
        bcum = lf
        sh = 1
        while sh < L:
            bcum = bcum + jnp.where(t_io >= sh, pltpu.roll(bcum, sh, 0), 0.0)
            sh *= 2
        m_old = m_ref[h:h + 1, :]
        c_old = c_ref[h]
        n_old = n_ref[h:h + 1, :]
        dmat = jnp.where(causal, bcum - bcum.T + ig.T, -jnp.inf)
        inter = bcum + m_old
        m_t = jnp.maximum(jnp.max(dmat, axis=-1, keepdims=True), inter)
        sc = _dot_nt(qb, kb) * jnp.exp(dmat - m_t)
        decay = jnp.exp(inter - m_t)
        num = decay * _dot_nt(qb, c_old.astype(bf16)) + jnp.dot(sc.astype(bf16), vb, preferred_element_type=f32)
        den = decay * jnp.sum(q * n_old, axis=-1, keepdims=True) + jnp.sum(sc, axis=-1, keepdims=True)
        hh = num / jnp.maximum(jnp.abs(den), jnp.exp(-m_t))
        o_ref[:, hs] = _mlstm_norm_gate(hh, zo_ref[:, hs], ng_ref[:, hs])

        b_last = bcum[L - 1:L, :]
        w_log = b_last - bcum + ig
        m_new = jnp.maximum(b_last + m_old, jnp.max(w_log, axis=0, keepdims=True))
        w = jnp.exp(w_log - m_new)
        carry = jnp.exp(b_last + m_old - m_new)
        c_ref[h] = carry * c_old + jnp.dot((w * v).T.astype(bf16), kb, preferred_element_type=f32)
        n_ref[h:h + 1, :] = carry * n_old + jnp.sum(w * k, axis=0, keepdims=True)
        m_ref[h:h + 1, :] = m_new

    @pl.when(c == pl.num_programs(1) - 1)
    def _():
        co_ref[0] = c_ref[...]
        no_ref[0] = n_ref[...]
        mo_ref[0] = m_ref[...]


def _mlstm_weights(conv_w, conv_b, wq, wk, gate_bias, norm_g):
    gb = jnp.repeat(gate_bias.astype(jnp.float32), M_DH, axis=1)
    return (conv_w, conv_b.reshape(1, M_WIDTH), wq.astype(jnp.bfloat16), wk.astype(jnp.bfloat16), gb,
            norm_g.reshape(1, M_WIDTH))


def _mlstm_prompt(zmqk, zmv, zmo, small, weights, b, t):
    L = MLSTM_CHUNK
    n_ch = t // L
    rows = lambda width: pl.BlockSpec((L, width), lambda bi, ci: (bi * n_ch + ci, 0))
    full = lambda a: pl.BlockSpec(a.shape, lambda bi, ci: (0,) * a.ndim)
    state = lambda *shape: pl.BlockSpec((1,) + shape, lambda bi, ci: (bi,) + (0,) * len(shape))
    f32 = jnp.float32
    return pl.pallas_call(
        _mlstm_prompt_kernel,
        out_shape=[jax.ShapeDtypeStruct((b * t, M_WIDTH), f32), jax.ShapeDtypeStruct((b, M_HEADS, M_DH, M_DH), f32),
                   jax.ShapeDtypeStruct((b, 8, M_DH), f32), jax.ShapeDtypeStruct((b, 8, M_DH), f32)],
        grid=(b, n_ch),
        in_specs=[rows(M_WIDTH), rows(M_WIDTH), rows(M_WIDTH), rows(LANES)] + [full(a) for a in weights],
        out_specs=[rows(M_WIDTH), state(M_HEADS, M_DH, M_DH), state(8, M_DH), state(8, M_DH)],
        scratch_shapes=[pltpu.VMEM((CONV_TAIL + L, M_WIDTH), f32), pltpu.VMEM((M_HEADS, M_DH, M_DH), f32),
                        pltpu.VMEM((8, M_DH), f32), pltpu.VMEM((8, M_DH), f32)],
        compiler_params=pltpu.CompilerParams(dimension_semantics=("arbitrary", "arbitrary")),
        name="mlstm_prompt",
    )(zmqk, zmv, zmo, small, *weights)


def _mlstm_decode_kernel(zqk_ref, conv_ref, zv_ref, zo_ref, sm_ref, c_ref, n_ref, m_ref,
                         cw_ref, cb_ref, wq_ref, wk_ref, gb_ref, ng_ref, o_ref, co_ref, no_ref, mo_ref):
    f32, bf16 = jnp.float32, jnp.bfloat16
    nt = MLSTM_DEC_TILE
    conv = cb_ref[...] + zqk_ref[...] * cw_ref[CONV_W - 1:CONV_W, :]
    for j in range(CONV_W - 1):
        conv = conv + conv_ref[j] * cw_ref[j:j + 1, :]
    a = jax.nn.silu(conv).astype(bf16)
    sm = sm_ref[...]
    lane = lax.broadcasted_iota(jnp.int32, (M_DH, M_DH), 1)
    pad_rows = jnp.zeros((M_DH - nt, M_DH), f32)
    for h in range(M_HEADS):
        hs = slice(h * M_DH, (h + 1) * M_DH)
        q = jnp.dot(a[:, hs], wq_ref[h], preferred_element_type=f32)
        k = jnp.dot(a[:, hs], wk_ref[h], preferred_element_type=f32) * (M_DH ** -0.5)
        v = zv_ref[:, hs]
        col = 3 * N_HEADS + h
        ig = jnp.broadcast_to(sm[:, col:col + 1], (nt, M_DH)) + gb_ref[0:1, hs]
        lf = jax.nn.log_sigmoid(jnp.broadcast_to(sm[:, col + M_HEADS:col + M_HEADS + 1], (nt, M_DH)) + gb_ref[1:2, hs])
        m_old = m_ref[:, hs]
        n_old = n_ref[:, hs]
        m_new = jnp.maximum(lf + m_old, ig)
        decay = jnp.exp(lf + m_old - m_new)
        w = jnp.exp(ig - m_new)
        sc = jnp.sum(q * k, axis=-1, keepdims=True) * w
        den = decay * jnp.sum(n_old * q, axis=-1, keepdims=True) + sc
        wv_t = jnp.concatenate([w * v, pad_rows], axis=0).T
        cq_t = jnp.zeros((M_DH, M_DH), f32)
        for i in range(nt):
            c_old = c_ref[i, h]
            cq = jnp.sum(c_old * q[i:i + 1, :], axis=-1, keepdims=True)
            cq_t = jnp.where(lane == i, cq, cq_t)
            co_ref[i, h] = decay[i:i + 1, :] * c_old + wv_t[:, i:i + 1] * k[i:i + 1, :]
        num = decay * cq_t.T[:nt] + sc * v
        hh = num / jnp.maximum(jnp.abs(den), jnp.exp(-m_new))
        o_ref[:, hs] = _mlstm_norm_gate(hh, zo_ref[:, hs], ng_ref[:, hs])
        no_ref[:, hs] = decay * n_old + w * k
        mo_ref[:, hs] = m_new


def _mlstm_decode(zmqk, zmv, zmo, small, state_c, state_n, state_m, state_conv, weights):
    b = zmqk.shape[0]
    nt = MLSTM_DEC_TILE
    f32 = jnp.float32
    rows = lambda width: pl.BlockSpec((nt, width), lambda i: (i, 0))
    full = lambda a: pl.BlockSpec(a.shape, lambda i: (0,) * a.ndim)
    cspec = pl.BlockSpec((nt, M_HEADS, M_DH, M_DH), lambda i: (i, 0, 0, 0))
    conv_t = state_conv.transpose(1, 0, 2)
    m_rep = jnp.repeat(state_m, M_DH, axis=1)
    return pl.pallas_call(
        _mlstm_decode_kernel,
        out_shape=[jax.ShapeDtypeStruct((b, M_WIDTH), f32), jax.ShapeDtypeStruct(state_c.shape, f32),
                   jax.ShapeDtypeStruct((b, M_WIDTH), f32), jax.ShapeDtypeStruct((b, M_WIDTH), f32)],
        grid=(b // nt,),
        in_specs=[rows(M_WIDTH), pl.BlockSpec((CONV_W - 1, nt, M_WIDTH), lambda i: (0, i, 0)), rows(M_WIDTH),
                  rows(M_WIDTH), rows(LANES), cspec, rows(M_WIDTH), rows(M_WIDTH)] + [full(a) for a in weights],
        out_specs=[rows(M_WIDTH), cspec, rows(M_WIDTH), rows(M_WIDTH)],
        compiler_params=pltpu.CompilerParams(dimension_semantics=("arbitrary",)),
        name="mlstm_decode",
    )(zmqk, conv_t, zmv, zmo, small, state_c, state_n.reshape(b, M_WIDTH), m_rep, *weights)


def _merge_kernel(x_ref, on_ref, om_ref, zm_ref, wn_ref, wm_ref, wo_ref, o_ref):
    zm = zm_ref[...]
    g_a = jax.nn.sigmoid(zm[:, :D_MODEL])
    g_b = jax.nn.sigmoid(zm[:, D_MODEL:])
    ya = jnp.dot(on_ref[...], wn_ref[...], preferred_element_type=jnp.float32)
    yb = jnp.dot(om_ref[...].astype(jnp.bfloat16), wm_ref[...], preferred_element_type=jnp.float32)
    y = (g_a * ya + g_b * yb).astype(jnp.bfloat16)
    o_ref[...] = x_ref[...] + jnp.dot(y, wo_ref[...], preferred_element_type=jnp.float32)


def _merge(x, o_nsa, o_mlstm, zmerge, w_proj_nsa, w_proj_mlstm, w_out):
    n = x.shape[0]
    tm = min(PROJ_TOKEN_TILE, n)
    row = lambda width: pl.BlockSpec((tm, width), lambda i: (i, 0))
    full = lambda a: pl.BlockSpec(a.shape, lambda i: (0, 0))
    return pl.pallas_call(
        _merge_kernel,
        out_shape=jax.ShapeDtypeStruct((n, D_MODEL), jnp.float32),
        grid=(n // tm,),
        in_specs=[row(D_MODEL), row(o_nsa.shape[1]), row(M_WIDTH), row(2 * D_MODEL),
                  full(w_proj_nsa), full(w_proj_mlstm), full(w_out)],
        out_specs=row(D_MODEL),
        compiler_params=pltpu.CompilerParams(dimension_semantics=("arbitrary",),
                                             vmem_limit_bytes=V7X_VMEM_LIMIT_BYTES),
        name="merge_out_proj",
    )(x, o_nsa, o_mlstm, zmerge, w_proj_nsa, w_proj_mlstm, w_out)


def _kv_rows(a, b, t):
    return a.reshape(b, t, N_KV, HEAD_DIM)


def _mix_prompt(proj, b, t, cmp_k, cmp_v, mlstm_w, rel_bias):
    qpad, kvb, kc_rows, vc_rows, ks_rows, vs_rows, kw_rows, vw_rows, zmqk, zmv, zmo, zmerge, small = proj
    kc, vc = _compress_rows(kc_rows, vc_rows, b, t, cmp_k, cmp_v)
    o_nsa = _nsa_prompt(qpad, small, kvb, kc, vc.transpose(0, 2, 1), rel_bias, b, t)
    o_mlstm, c_f, n_f, m_f = _mlstm_prompt(zmqk, zmv, zmo, small, mlstm_w, b, t)
    n_keep = min(WINDOW, t)
    states = (_kv_rows(kc_rows, b, t), _kv_rows(vc_rows, b, t), _kv_rows(ks_rows, b, t), _kv_rows(vs_rows, b, t),
              _kv_rows(kw_rows, b, t)[:, t - n_keep:], _kv_rows(vw_rows, b, t)[:, t - n_keep:],
              c_f, n_f[:, :M_HEADS], m_f[:, :M_HEADS, 0], zmqk.reshape(b, t, M_WIDTH)[:, t - (CONV_W - 1):])
    return o_nsa, o_mlstm, zmerge, states


def _mix_decode(proj, b, caches, mlstm_state, page_table, cmp_k, cmp_v, mlstm_w, rel_bias):
    cache_k_cmp, cache_v_cmp, cache_k_sel, cache_v_sel, cache_k_win, cache_v_win = caches
    state_c, state_n, state_m, state_conv = mlstm_state
    qpad, _, kc_new, vc_new, ks_new, vs_new, kw_new, vw_new, zmqk, zmv, zmo, zmerge, small = proj
    kc, vc = _compress_pages(cache_k_cmp, cache_v_cmp, page_table, cmp_k, cmp_v)
    o_nsa, k_win, v_win = _nsa_decode(qpad, small, kc, vc, ks_new, vs_new, kw_new, vw_new, cache_k_sel, cache_v_sel,
                                      cache_k_win, cache_v_win, page_table, rel_bias)
    o_mlstm, c_n, n_n, m_rep = _mlstm_decode(zmqk, zmv, zmo, small, state_c, state_n, state_m, state_conv, mlstm_w)
    wb = cache_k_win.shape[1]
    conv_new = jnp.concatenate([state_conv[:, 1:], zmqk[:, None, :]], axis=1)
    states = (_kv_rows(kc_new, b, 1), _kv_rows(vc_new, b, 1), _kv_rows(ks_new, b, 1), _kv_rows(vs_new, b, 1),
              _kv_rows(k_win, b, wb), _kv_rows(v_win, b, wb),
              c_n, n_n.reshape(b, M_HEADS, M_DH), m_rep[:, ::M_DH], conv_new)
    return o_nsa, o_mlstm, zmerge, states


def kernel(x_prompt, x_sample, cache_k_cmp, cache_v_cmp, cache_k_sel, cache_v_sel, cache_k_win, cache_v_win, state_mlstm_C, state_mlstm_n, state_mlstm_m, state_mlstm_conv, page_table, norm_ffn1, ffn1_w_in, ffn1_w_out, norm_mix, w_mix_in, cmp_pos_k, cmp_pos_v, cmp_phi_k1, cmp_phi_k2, cmp_phi_v1, cmp_phi_v2, rel_bias, mlstm_conv_w, mlstm_conv_b, mlstm_wq, mlstm_wk, mlstm_gate_bias, mlstm_norm, w_proj_nsa, w_proj_mlstm, w_out, norm_ffn2, ffn2_w_in, ffn2_w_out, norm_final):
    assert x_sample.shape[1] == DEC_SEQ == 1
    bf = lambda w: w.astype(jnp.bfloat16)
    w1i, w1o, w2i, w2o = bf(ffn1_w_in), bf(ffn1_w_out), bf(ffn2_w_in), bf(ffn2_w_out)
    w_mix = _regroup_mix_weight(w_mix_in)
    w_nsa_out, wm, wo = bf(_pad_nsa_out_weight(w_proj_nsa)), bf(w_proj_mlstm), bf(w_out)
    cmp_k = _compress_weights(cmp_pos_k, cmp_phi_k1, cmp_phi_k2)
    cmp_v = _compress_weights(cmp_pos_v, cmp_phi_v1, cmp_phi_v2)
    mlstm_w = _mlstm_weights(mlstm_conv_w, mlstm_conv_b, mlstm_wq, mlstm_wk, mlstm_gate_bias, mlstm_norm)

    def layer(x3, mix_fn):
        b, t, _ = x3.shape
        x = x3.reshape(b * t, D_MODEL)
        x1 = _ffn(x, norm_ffn1, w1i, w1o, norm_final, final_norm=False)
        proj = _project(x1, norm_mix, w_mix)
        o_nsa, o_mlstm, zmerge, states = mix_fn(proj, b, t)
        x2 = _merge(x1, o_nsa, o_mlstm, zmerge, w_nsa_out, wm, wo)
        y = _ffn(x2, norm_ffn2, w2i, w2o, norm_final, final_norm=True)
        return y.reshape(b, t, D_MODEL), states

    y_prompt, st_p = layer(x_prompt, lambda proj, b, t: _mix_prompt(proj, b, t, cmp_k, cmp_v, mlstm_w, rel_bias))
    y_sample, st_s = layer(x_sample, lambda proj, b, t: _mix_decode(
        proj, b, (cache_k_cmp, cache_v_cmp, cache_k_sel, cache_v_sel, cache_k_win, cache_v_win),
        (state_mlstm_C, state_mlstm_n, state_mlstm_m, state_mlstm_conv), page_table, cmp_k, cmp_v, mlstm_w, rel_bias))
    k_cmp_p, v_cmp_p, k_sel_p, v_sel_p, k_win_p, v_win_p, C_p, n_p, m_p, conv_p = st_p
    k_cmp_s, v_cmp_s, k_sel_s, v_sel_s, k_win_s, v_win_s, C_s, n_s, m_s, conv_s = st_s
    return (y_prompt, y_sample, k_cmp_p, k_cmp_s, v_cmp_p, v_cmp_s, k_sel_p, k_sel_s, v_sel_p, v_sel_s,
            k_win_p, k_win_s, v_win_p, v_win_s, C_p, C_s, n_p, n_s, m_p, m_s, conv_p, conv_s)
```

```python
import math
from functools import partial

import numpy as np
import jax
import jax.numpy as jnp
from jax import lax
from jax.experimental import pallas as pl
from jax.experimental.pallas import tpu as pltpu

D_MODEL = 1024
SEQ = 8192
DEC_SEQ = 1
PAST_LEN = 8192
PAGE_SIZE = 128
N_HEADS = 8
N_KV = 2
GROUP = N_HEADS // N_KV
HEAD_DIM = 64
NSA_WIDTH = N_HEADS * HEAD_DIM
KV_WIDTH = N_KV * HEAD_DIM
L_CMP = 32
D_CMP = 16
CMP_RATIO = L_CMP // D_CMP
L_SEL = 64
K_SEL = 16
WINDOW = 512
Q_BLOCK = 128
FORCE_SCORE = 1000.0
N_BUCKETS = 32
MAX_DIST = 128
M_HEADS = 4
M_DH = 128
M_WIDTH = M_HEADS * M_DH
CONV_W = 4
M_CHUNK = 64
D_FF = 2816
EPS = 1e-6

V7X_VMEM_LIMIT_BYTES = 56 * 1024 * 1024
LANES = 128

QPAD_WIDTH = N_HEADS * LANES
N_SEL_PAD = 128
FAR_TILES = 8
NEG_BIG = -1e30
SEL_OFF = -32768.0

FFN_TOKEN_TILE = 512
FFN_FF_TILE = 1408
PROJ_TOKEN_TILE = 512

_MIX_SIZES = (NSA_WIDTH, KV_WIDTH, KV_WIDTH, KV_WIDTH, KV_WIDTH, KV_WIDTH, KV_WIDTH, 3 * N_HEADS,
              M_WIDTH, M_WIDTH, 2 * M_HEADS, M_WIDTH, 2 * D_MODEL)
_MIX_OFFS = np.concatenate([[0], np.cumsum(_MIX_SIZES)]).tolist()
_PROJ_SEGS = (("kv", 6 * KV_WIDTH), ("mqk", M_WIDTH), ("mv", M_WIDTH),
              ("mo", M_WIDTH), ("merge", 2 * D_MODEL), ("small", LANES))


def _rms(x, g):
    return x * lax.rsqrt(jnp.mean(x * x, axis=-1, keepdims=True) + EPS) * g


def _ffn_kernel(x_ref, g_ref, wg_ref, wu_ref, wo_ref, gf_ref, o_ref, h_ref, acc_ref, *, final_norm):
    j = pl.program_id(1)

    @pl.when(j == 0)
    def _():
        h_ref[...] = _rms(x_ref[...], g_ref[...]).astype(jnp.bfloat16)
        acc_ref[...] = jnp.zeros_like(acc_ref)

    h = h_ref[...]
    gate = jnp.dot(h, wg_ref[...], preferred_element_type=jnp.float32)
    up = jnp.dot(h, wu_ref[...], preferred_element_type=jnp.float32)
    a = (jax.nn.silu(gate) * up).astype(jnp.bfloat16)
    acc_ref[...] += jnp.dot(a, wo_ref[...], preferred_element_type=jnp.float32)

    @pl.when(j == pl.num_programs(1) - 1)
    def _():
        y = x_ref[...] + 0.5 * acc_ref[...]
        if final_norm:
            y = _rms(y, gf_ref[...])
        o_ref[...] = y


def _ffn(x, g, w_in, w_out, g_final, *, final_norm):
    n = x.shape[0]
    tm = min(FFN_TOKEN_TILE, n)
    nj = D_FF // FFN_FF_TILE
    return pl.pallas_call(
        partial(_ffn_kernel, final_norm=final_norm),
        out_shape=jax.ShapeDtypeStruct((n, D_MODEL), jnp.float32),
        grid=(n // tm, nj),
        in_specs=[
            pl.BlockSpec((tm, D_MODEL), lambda i, j: (i, 0)),
            pl.BlockSpec((1, D_MODEL), lambda i, j: (0, 0)),
            pl.BlockSpec((D_MODEL, FFN_FF_TILE), lambda i, j: (0, j)),
            pl.BlockSpec((D_MODEL, FFN_FF_TILE), lambda i, j: (0, j + D_FF // FFN_FF_TILE)),
            pl.BlockSpec((FFN_FF_TILE, D_MODEL), lambda i, j: (j, 0)),
            pl.BlockSpec((1, D_MODEL), lambda i, j: (0, 0)),
        ],
        out_specs=pl.BlockSpec((tm, D_MODEL), lambda i, j: (i, 0)),
        scratch_shapes=[pltpu.VMEM((tm, D_MODEL), jnp.bfloat16), pltpu.VMEM((tm, D_MODEL), jnp.float32)],
        compiler_params=pltpu.CompilerParams(dimension_semantics=("arbitrary", "arbitrary"),
                                             vmem_limit_bytes=V7X_VMEM_LIMIT_BYTES),
        name="ffn",
    )(x, g.reshape(1, D_MODEL), w_in, w_in, w_out, g_final.reshape(1, D_MODEL))


def _proj_kernel(x_ref, g_ref, w_ref, qpad_ref, kvb_ref, *o_refs):
    h = _rms(x_ref[...], g_ref[...]).astype(jnp.bfloat16)
    zq = jnp.dot(h, w_ref[:, :QPAD_WIDTH], preferred_element_type=jnp.float32)
    qpad_ref[...] = (zq * (HEAD_DIM ** -0.5)).astype(jnp.bfloat16)
    off = QPAD_WIDTH
    o_refs = list(o_refs)
    for name, width in _PROJ_SEGS:
        z = jnp.dot(h, w_ref[:, off:off + width], preferred_element_type=jnp.float32)
        if name == "kv":
            kvb_ref[...] = z.astype(jnp.bfloat16)
            for i in range(6):
                o_refs.pop(0)[...] = z[:, i * KV_WIDTH:(i + 1) * KV_WIDTH]
        else:
            o_refs.pop(0)[...] = z
        off += width


def _regroup_mix_weight(w_mix_in):
    seg = lambda i: w_mix_in[:, _MIX_OFFS[i]:_MIX_OFFS[i + 1]]
    wq = seg(0)
    zeros = jnp.zeros((D_MODEL, HEAD_DIM), w_mix_in.dtype)
    qpad = []
    for h in range(N_HEADS):
        wh = wq[:, h * HEAD_DIM:(h + 1) * HEAD_DIM]
        qpad += [wh, zeros] if h // GROUP == 0 else [zeros, wh]
    small = jnp.concatenate([seg(7), seg(10), jnp.zeros((D_MODEL, LANES - 3 * N_HEADS - 2 * M_HEADS), w_mix_in.dtype)], axis=1)
    cols = qpad + [seg(i) for i in range(1, 7)] + [seg(8), seg(9), seg(11), seg(12), small]
    return jnp.concatenate(cols, axis=1).astype(jnp.bfloat16)


def _project(x, g, w_regrouped):
    n = x.shape[0]
    tm = min(PROJ_TOKEN_TILE, n)
    wcols = w_regrouped.shape[1]
    row = lambda width: pl.BlockSpec((tm, width), lambda i: (i, 0))
    widths = []
    for name, width in _PROJ_SEGS:
        widths += [KV_WIDTH] * 6 if name == "kv" else [width]
    return pl.pallas_call(
        _proj_kernel,
        out_shape=[jax.ShapeDtypeStruct((n, QPAD_WIDTH), jnp.bfloat16),
                   jax.ShapeDtypeStruct((n, 6 * KV_WIDTH), jnp.bfloat16)]
                  + [jax.ShapeDtypeStruct((n, width), jnp.float32) for width in widths],
        grid=(n // tm,),
        in_specs=[
            row(D_MODEL),
            pl.BlockSpec((1, D_MODEL), lambda i: (0, 0)),
            pl.BlockSpec((D_MODEL, wcols), lambda i: (0, 0)),
        ],
        out_specs=[row(QPAD_WIDTH), row(6 * KV_WIDTH)] + [row(width) for width in widths],
        compiler_params=pltpu.CompilerParams(dimension_semantics=("arbitrary",),
                                             vmem_limit_bytes=V7X_VMEM_LIMIT_BYTES),
        name="mix_in_proj",
    )(x, g.reshape(1, D_MODEL), w_regrouped)


def _dot_nt(a, b):
    return lax.dot_general(a, b, (((1,), (1,)), ((), ())), preferred_element_type=jnp.float32)


def _bucket_table():
    d = np.arange(MAX_DIST + 1)
    max_exact = N_BUCKETS // 2
    nf = np.maximum(d, 1).astype(np.float64)
    large = max_exact + (np.log(nf / max_exact) / math.log(MAX_DIST / max_exact) * (N_BUCKETS - max_exact)).astype(np.int64)
    return np.where(d < max_exact, d, np.minimum(large, N_BUCKETS - 1)).astype(np.int32)


def _bias_tables(rel_bias, n_c):
    tab = rel_bias.astype(jnp.float32)[_bucket_table()].T
    i = np.arange(Q_BLOCK)[:, None]
    j = np.arange(Q_BLOCK)[None, :]
    far = tab[:, MAX_DIST][:, None, None]
    t0 = tab[:, np.clip(i - j, 0, MAX_DIST)] - far
    t1 = tab[:, np.minimum(Q_BLOCK + i - j, MAX_DIST)] - far
    tiles = jnp.stack([t0, t1]).reshape(2, N_HEADS * Q_BLOCK, Q_BLOCK)
    lo, hi = -2 * Q_BLOCK // D_CMP, Q_BLOCK // D_CMP
    dist = np.arange(Q_BLOCK)[None, :] - (L_CMP - 1) - D_CMP * np.arange(lo, hi)[:, None]
    band = jnp.where(jnp.asarray(dist >= 0), tab[:, np.clip(dist, 0, MAX_DIST)], NEG_BIG)
    bc = jnp.concatenate([jnp.broadcast_to(tab[:, MAX_DIST][:, None, None], (N_HEADS, n_c + lo, Q_BLOCK)), band,
                          jnp.full((N_HEADS, n_c - hi, Q_BLOCK), NEG_BIG, jnp.float32)], axis=1)
    bct = bc.transpose(1, 0, 2).reshape(2 * n_c, N_HEADS * Q_BLOCK)
    return tiles, bct


def _nsa_prompt_kernel(q_ref, gate_ref, kc_ref, vct_ref, ks_ref, vs_ref, kw_ref, vw_ref, tiles_ref, bct_ref,
                       o_ref, m_ref, l_ref, acc_ref, *, n_c):
    f32, bf16 = jnp.float32, jnp.bfloat16
    qb = pl.program_id(1)
    rows_all = N_HEADS * Q_BLOCK
    qi = lax.broadcasted_iota(jnp.int32, (rows_all, Q_BLOCK), 0) & (Q_BLOCK - 1)
    kj = lax.broadcasted_iota(jnp.int32, (rows_all, Q_BLOCK), 1)
    causal = qi >= kj
    window_edge = kj >= qi
    sig = jax.nn.sigmoid(gate_ref[...])

    s_io = lax.broadcasted_iota(jnp.int32, (N_SEL_PAD, N_KV * Q_BLOCK), 0)
    i_io = lax.broadcasted_iota(jnp.int32, (N_SEL_PAD, N_KV * Q_BLOCK), 1) & (Q_BLOCK - 1)
    qpos = qb * Q_BLOCK + i_io
    cur = 2 * qb + (i_io >= L_SEL).astype(jnp.int32)
    blk_valid = s_io * L_SEL <= qpos
    blk_forced = (s_io == 0) | (s_io == cur) | (s_io == cur - 1)
    ov_s = lax.broadcasted_iota(jnp.int32, (N_SEL_PAD, n_c), 0)
    ov_c = lax.broadcasted_iota(jnp.int32, (N_SEL_PAD, n_c), 1)
    overlap_t = jnp.where((ov_c >= 4 * ov_s - 1) & (ov_c <= 4 * ov_s + 3), 1.0, 0.0).astype(bf16)

    def flash_init():
        m_ref[...] = jnp.full(m_ref.shape, NEG_BIG, f32)
        l_ref[...] = jnp.zeros(l_ref.shape, f32)
        acc_ref[...] = jnp.zeros(acc_ref.shape, f32)

    def flash_step(s, v_tile):
        m_old = m_ref[...]
        m_new = jnp.maximum(m_old, jnp.max(s, axis=-1, keepdims=True))
        alpha = jnp.exp(m_old - m_new)
        p = jnp.exp(s - jnp.concatenate([m_new] * (s.shape[1] // LANES), axis=1))
        l_ref[...] = alpha * l_ref[...] + jnp.sum(p, axis=-1, keepdims=True)
        acc_ref[...] = alpha * acc_ref[...] + jnp.dot(p.astype(bf16), v_tile, preferred_element_type=f32)
        m_ref[...] = m_new

    def key_rows(kt, n_tiles=1):
        return pl.ds(pl.multiple_of(kt * Q_BLOCK, Q_BLOCK), n_tiles * Q_BLOCK)

    q_all = jnp.concatenate([q_ref[:, h * LANES:(h + 1) * LANES] for h in range(N_HEADS)], axis=0)

    st = _dot_nt(kc_ref[0], q_all)
    st = st + bct_ref[pl.ds(pl.multiple_of(n_c - 8 * qb, 8), n_c), :]
    mx = jnp.max(st, axis=0, keepdims=True)
    mx = jnp.where(mx < 0.1 * NEG_BIG, 0.0, mx)
    p = jnp.exp(st - mx)
    p = p / jnp.maximum(jnp.sum(p, axis=0, keepdims=True), 1e-30)
    pb = p.astype(bf16)
    oc_t = jnp.dot(vct_ref[0], pb, preferred_element_type=f32)

    imps = []
    for g in range(N_KV):
        imp_g = None
        for r in range(GROUP):
            h = GROUP * g + r
            part = jnp.dot(overlap_t, pb[:, h * Q_BLOCK:(h + 1) * Q_BLOCK], preferred_element_type=f32)
            imp_g = part if imp_g is None else imp_g + part
        imps.append(imp_g)
    imp = jnp.concatenate(imps, axis=1)
    imp = jnp.where(blk_valid, imp + jnp.where(blk_forced, FORCE_SCORE, 0.0), -jnp.inf)

    s_f = s_io.astype(f32)

    def pick_body(_, carry):
        vals, sel_off = carry
        best = jnp.max(vals, axis=0, keepdims=True)
        first = jnp.min(jnp.where(vals == best, s_f, float(N_SEL_PAD)), axis=0, keepdims=True)
        hit = s_f == first
        return jnp.where(hit, -jnp.inf, vals), jnp.where(hit, 0.0, sel_off)

    _, sel_off = lax.fori_loop(0, K_SEL, pick_body, (imp, jnp.full(imp.shape, SEL_OFF, f32)))
    sel_rows = []
    for g in range(N_KV):
        sel_rows += [sel_off[:, g * Q_BLOCK:(g + 1) * Q_BLOCK].T.astype(bf16)] * GROUP
    lhs_sel = jnp.concatenate([jnp.concatenate(sel_rows, axis=0), q_all], axis=1)

    def sel_scores(kt, n_tiles=1):
        key_blk = jnp.right_shift(lax.broadcasted_iota(jnp.int32, (n_tiles * Q_BLOCK, N_SEL_PAD), 0),
                                  L_SEL.bit_length() - 1)
        blk_lane = lax.broadcasted_iota(jnp.int32, (n_tiles * Q_BLOCK, N_SEL_PAD), 1)
        onehot = jnp.where(blk_lane == 2 * kt + key_blk, 1.0, 0.0).astype(bf16)
        rhs = jnp.concatenate([onehot, ks_ref[0, key_rows(kt, n_tiles), :]], axis=1)
        return _dot_nt(lhs_sel, rhs)

    flash_init()

    n_far = jnp.maximum(qb - 1, 0)

    def far_step(kt, n_tiles):
        flash_step(sel_scores(kt, n_tiles), vs_ref[0, key_rows(kt, n_tiles), :])

    def far_body(i, carry):
        far_step(FAR_TILES * i, FAR_TILES)
        return carry

    lax.fori_loop(0, n_far // FAR_TILES, far_body, 0)
    width = FAR_TILES // 2
    while width >= 1:
        @pl.when((n_far & width) != 0)
        def _(width=width):
            far_step((n_far // (2 * width)) * (2 * width), width)
        width //= 2

    @pl.when(qb >= 1)
    def _():
        flash_step(sel_scores(qb - 1) + tiles_ref[1], vs_ref[0, key_rows(qb - 1), :])

    flash_step(jnp.where(causal, sel_scores(qb) + tiles_ref[0], NEG_BIG), vs_ref[0, key_rows(qb), :])
    o_sel = acc_ref[...] / l_ref[...]

    scores, values = [], []
    for dt in range(WINDOW // Q_BLOCK + 1):
        rows = key_rows(jnp.maximum(qb - dt, 0))
        s = _dot_nt(q_all, kw_ref[0, rows, :])
        if dt < 2:
            s = s + tiles_ref[dt]
        if dt == 0:
            s = jnp.where(causal, s, NEG_BIG)
        else:
            visible = window_edge if dt == WINDOW // Q_BLOCK else True
            s = jnp.where(visible & (qb >= dt), s, NEG_BIG)
        scores.append(s)
        values.append(vw_ref[0, rows, :])
    s = jnp.concatenate(scores, axis=1)
    p = jnp.exp(s - jnp.max(s, axis=-1, keepdims=True))
    o_win = (jnp.dot(p.astype(bf16), jnp.concatenate(values, axis=0), preferred_element_type=f32)
             / jnp.sum(p, axis=-1, keepdims=True))

    for h in range(N_HEADS):
        rs = slice(h * Q_BLOCK, (h + 1) * Q_BLOCK)
        o = (oc_t[:, rs].T * sig[:, 3 * h:3 * h + 1] + o_sel[rs] * sig[:, 3 * h + 1:3 * h + 2]
             + o_win[rs] * sig[:, 3 * h + 2:3 * h + 3])
        o_ref[:, h * LANES:(h + 1) * LANES] = o.astype(bf16)


def _nsa_prompt(qpad, small, kvb, kcb, vct, rel_bias, b, t):
    n_c = t // D_CMP
    n_qb = t // Q_BLOCK
    tiles, bct = _bias_tables(rel_bias, n_c)
    kv3 = kvb.reshape(b, t, 6 * KV_WIDTH)
    rows = lambda width: pl.BlockSpec((Q_BLOCK, width), lambda bi, qi: (bi * n_qb + qi, 0))
    seq = lambda lane_block: pl.BlockSpec((1, t, KV_WIDTH), lambda bi, qi: (bi, 0, lane_block))
    full = lambda a: pl.BlockSpec(a.shape, lambda bi, qi: (0,) * a.ndim)
    rows_all = N_HEADS * Q_BLOCK
    return pl.pallas_call(
        partial(_nsa_prompt_kernel, n_c=n_c),
        out_shape=jax.ShapeDtypeStruct((b * t, QPAD_WIDTH), jnp.bfloat16),
        grid=(b, n_qb),
        in_specs=[rows(QPAD_WIDTH), rows(LANES),
                  pl.BlockSpec((1, n_c, KV_WIDTH), lambda bi, qi: (bi, 0, 0)),
                  pl.BlockSpec((1, KV_WIDTH, n_c), lambda bi, qi: (bi, 0, 0)),
                  seq(2), seq(3), seq(4), seq(5), full(tiles), full(bct)],
        out_specs=rows(QPAD_WIDTH),
        scratch_shapes=[pltpu.VMEM((rows_all, LANES), jnp.float32), pltpu.VMEM((rows_all, LANES), jnp.float32),
                        pltpu.VMEM((rows_all, KV_WIDTH), jnp.float32)],
        compiler_params=pltpu.CompilerParams(dimension_semantics=("arbitrary", "arbitrary"),
                                             vmem_limit_bytes=V7X_VMEM_LIMIT_BYTES),
        name="nsa_prompt",
    )(qpad, small, kcb, vct, kv3, kv3, kv3, kv3, tiles, bct)


def _pad_nsa_out_weight(w_proj_nsa):
    zeros = jnp.zeros((HEAD_DIM, D_MODEL), w_proj_nsa.dtype)
    rows = []
    for h in range(N_HEADS):
        wh = w_proj_nsa[h * HEAD_DIM:(h + 1) * HEAD_DIM]
        rows += [wh, zeros] if h // GROUP == 0 else [zeros, wh]
    return jnp.concatenate(rows, axis=0)


SUB_WIDTH = D_CMP * KV_WIDTH
PHI_HIDDEN = 2 * HEAD_DIM
PAGE_SUBS = PAGE_SIZE // D_CMP
CMP_PAGES_PER_STEP = 32
SUB_PITCH = D_CMP + 8


def _compress_weights(pos_emb, w1, w2):
    eye = jnp.eye(N_KV, dtype=w1.dtype)
    halves = []
    for r in range(CMP_RATIO):
        w1r = w1[r * D_CMP:(r + 1) * D_CMP]
        halves.append(jnp.einsum('ldh,gk->lgdkh', w1r, eye).reshape(SUB_WIDTH, N_KV * PHI_HIDDEN))
    w1big = jnp.concatenate(halves, axis=1).astype(jnp.bfloat16)
    pos = jnp.broadcast_to(pos_emb.reshape(CMP_RATIO, D_CMP, 1, HEAD_DIM), (CMP_RATIO, D_CMP, N_KV, HEAD_DIM))
    w2big = jnp.einsum('hd,gk->ghkd', w2, eye).reshape(N_KV * PHI_HIDDEN, KV_WIDTH).astype(jnp.bfloat16)
    return w1big, pos.reshape(CMP_RATIO, 1, SUB_WIDTH), w2big


def _compress_tokens(lhs0, lhs1, w1big, w2big):
    hw = N_KV * PHI_HIDDEN
    p0 = jnp.dot(lhs0, w1big[:, :hw], preferred_element_type=jnp.float32)
    p1 = jnp.dot(lhs1, w1big[:, hw:], preferred_element_type=jnp.float32)
    n_sub = p1.shape[0]
    hidden = p0 + pltpu.roll(p1, n_sub - 1, 0)
    return jnp.dot(jax.nn.gelu(hidden).astype(jnp.bfloat16), w2big, preferred_element_type=jnp.float32)


def _compress_rows_kernel(rk_ref, rv_ref, w1k_ref, pk_ref, w2k_ref, w1v_ref, pv_ref, w2v_ref, kc_ref, vc_ref):
    for r_ref, w1_ref, p_ref, w2_ref, o_ref in ((rk_ref, w1k_ref, pk_ref, w2k_ref, kc_ref),
                                                 (rv_ref, w1v_ref, pv_ref, w2v_ref, vc_ref)):
        x = r_ref[0]
        lhs = [(x + p_ref[r]).astype(jnp.bfloat16) for r in range(CMP_RATIO)]
        o_ref[0] = _compress_tokens(lhs[0], lhs[1], w1_ref[...], w2_ref[...]).astype(o_ref.dtype)


def _compress_rows(k_rows, v_rows, b, t, wk, wv):
    n_sub = t // D_CMP
    view = lambda a: a.reshape(b, n_sub, SUB_WIDTH)
    seq = pl.BlockSpec((1, n_sub, SUB_WIDTH), lambda bi: (bi, 0, 0))
    full = lambda a: pl.BlockSpec(a.shape, lambda bi: (0,) * a.ndim)
    out = pl.BlockSpec((1, n_sub, KV_WIDTH), lambda bi: (bi, 0, 0))
    return pl.pallas_call(
        _compress_rows_kernel,
        out_shape=[jax.ShapeDtypeStruct((b, n_sub, KV_WIDTH), jnp.bfloat16)] * 2,
        grid=(b,),
        in_specs=[seq, seq] + [full(a) for a in (*wk, *wv)],
        out_specs=[out, out],
        compiler_params=pltpu.CompilerParams(dimension_semantics=("arbitrary",),
                                             vmem_limit_bytes=V7X_VMEM_LIMIT_BYTES),
        name="compress_rows",
    )(view(k_rows), view(v_rows), *wk, *wv)


def _compress_pages_kernel(pt_ref, *refs):
    n_pg = CMP_PAGES_PER_STEP
    k_pages, v_pages = refs[:n_pg], refs[n_pg:2 * n_pg]
    w1k_ref, pk_ref, w2k_ref, w1v_ref, pv_ref, w2v_ref, kc_ref, vc_ref, lhs_ref, rows_ref = refs[2 * n_pg:]
    step = pl.program_id(1)
    for c, (pages, p_ref) in enumerate(((k_pages, pk_ref), (v_pages, pv_ref))):
        for j in range(0, n_pg, 2):
            for jj in (j, j + 1):
                x = pages[jj][0].T
                for m in range(PAGE_SUBS):
                    rows_ref[c, jj, SUB_PITCH * m:SUB_PITCH * m + D_CMP, :] = x[D_CMP * m:D_CMP * (m + 1)]
            rows = pl.ds(pl.multiple_of(step * n_pg * PAGE_SUBS + j * PAGE_SUBS, 2 * PAGE_SUBS), 2 * PAGE_SUBS)
            for l in range(D_CMP):
                x = jnp.concatenate([rows_ref[c, j, pl.ds(l, PAGE_SUBS, stride=SUB_PITCH), :],
                                     rows_ref[c, j + 1, pl.ds(l, PAGE_SUBS, stride=SUB_PITCH), :]], axis=0)
                lanes = slice(l * KV_WIDTH, (l + 1) * KV_WIDTH)
                for r in range(CMP_RATIO):
                    lhs_ref[c, r, rows, lanes] = (x + p_ref[r, :, lanes]).astype(jnp.bfloat16)

    @pl.when(step == pl.num_programs(1) - 1)
    def _():
        for c, (w1_ref, w2_ref, o_ref) in enumerate(((w1k_ref, w2k_ref, kc_ref), (w1v_ref, w2v_ref, vc_ref))):
            o_ref[0] = _compress_tokens(lhs_ref[c, 0], lhs_ref[c, 1], w1_ref[...], w2_ref[...]).astype(o_ref.dtype)


def _compress_pages(cache_k, cache_v, page_table, wk, wv):
    b, n_pages = page_table.shape
    n_pool = cache_k.shape[0]
    n_sub = n_pages * PAGE_SUBS
    n_pg = CMP_PAGES_PER_STEP
    view = lambda c: c.transpose(0, 2, 3, 1).reshape(n_pool, KV_WIDTH, PAGE_SIZE)
    page = lambda j: pl.BlockSpec(
        (1, KV_WIDTH, PAGE_SIZE),
        lambda bi, si, pt: (jnp.clip(pt[jnp.minimum(bi, b - 1), jnp.minimum(si, n_pages // n_pg - 1) * n_pg + j],
                                     0, n_pool - 1), 0, 0))
    full = lambda a: pl.BlockSpec(a.shape, lambda bi, si, pt: (0,) * a.ndim)
    out = pl.BlockSpec((1, n_sub, KV_WIDTH), lambda bi, si, pt: (bi, 0, 0))
    return pl.pallas_call(
        _compress_pages_kernel,
        out_shape=[jax.ShapeDtypeStruct((b, n_sub, KV_WIDTH), jnp.bfloat16)] * 2,
        grid_spec=pltpu.PrefetchScalarGridSpec(
            num_scalar_prefetch=1,
            grid=(b, n_pages // n_pg),
            in_specs=[page(j) for j in range(n_pg)] * 2 + [full(a) for a in (*wk, *wv)],
            out_specs=[out, out],
            scratch_shapes=[pltpu.VMEM((2, CMP_RATIO, n_sub, SUB_WIDTH), jnp.bfloat16),
                            pltpu.VMEM((2, n_pg, PAGE_SUBS * SUB_PITCH, KV_WIDTH), jnp.float32)]),
        compiler_params=pltpu.CompilerParams(dimension_semantics=("arbitrary", "arbitrary"),
                                             vmem_limit_bytes=V7X_VMEM_LIMIT_BYTES),
        name="compress_pages",
    )(page_table, *([view(cache_k)] * n_pg), *([view(cache_v)] * n_pg), *wk, *wv)


N_SEL_DEC = PAST_LEN // L_SEL + 1
N_SEL_DEC_PAD = 256
CUR_BLOCK = PAST_LEN // L_SEL


def _nsa_decode_cmp_kernel(q_ref, kc_ref, vc_ref, bias_ref, oc_ref, imp_ref):
    f32, bf16 = jnp.float32, jnp.bfloat16
    n_c = kc_ref.shape[1]
    s = _dot_nt(q_ref[0], kc_ref[0]) + bias_ref[...]
    mx = jnp.max(s, axis=-1, keepdims=True)
    mx = jnp.where(mx < 0.1 * NEG_BIG, 0.0, mx)
    p = jnp.exp(s - mx)
    p = p / jnp.maximum(jnp.sum(p, axis=-1, keepdims=True), 1e-30)
    pb = p.astype(bf16)
    oc_ref[0] = jnp.dot(pb, vc_ref[0], preferred_element_type=f32)
    ov_c = lax.broadcasted_iota(jnp.int32, (n_c, N_SEL_DEC_PAD), 0)
    ov_s = lax.broadcasted_iota(jnp.int32, (n_c, N_SEL_DEC_PAD), 1)
    overlap = jnp.where((ov_c >= 4 * ov_s - 1) & (ov_c <= 4 * ov_s + 3), 1.0, 0.0).astype(bf16)
    imp_h = jnp.dot(pb, overlap, preferred_element_type=f32)
    head = lax.broadcasted_iota(jnp.int32, imp_h.shape, 0)
    imp_ref[0] = jnp.concatenate(
        [jnp.sum(jnp.where((head >= g * GROUP) & (head < (g + 1) * GROUP), imp_h, 0.0), axis=0, keepdims=True)
         for g in range(N_KV)], axis=1)


def _nsa_decode_topk_kernel(imp_ref, idx_ref, val_ref):
    f32 = jnp.float32
    nb = imp_ref.shape[0]
    s_io = lax.broadcasted_iota(jnp.int32, (N_SEL_DEC_PAD, nb), 0)
    visible = s_io < N_SEL_DEC
    forced = (s_io == 0) | (s_io == CUR_BLOCK) | (s_io == CUR_BLOCK - 1)
    tri_r = lax.broadcasted_iota(jnp.int32, (N_SEL_DEC_PAD, N_SEL_DEC_PAD), 0)
    tri_c = lax.broadcasted_iota(jnp.int32, (N_SEL_DEC_PAD, N_SEL_DEC_PAD), 1)
    before = jnp.where(tri_c < tri_r, 1.0, 0.0).astype(jnp.bfloat16)
    for g in range(N_KV):
        x = imp_ref[:, g * N_SEL_DEC_PAD:(g + 1) * N_SEL_DEC_PAD]
        xt = jnp.concatenate([x[:, i * LANES:(i + 1) * LANES].T for i in range(N_SEL_DEC_PAD // LANES)], axis=0)
        val = jnp.where(visible, xt + jnp.where(forced, FORCE_SCORE, 0.0), -jnp.inf)
        val_ref[...] = val

        def rank_body(s, rank):
            other = val_ref[pl.ds(s, 1), :]
            beats = (other > val) | ((other == val) & (s_io > s))
            return rank + jnp.where(beats, 1, 0)

        rank = lax.fori_loop(0, N_SEL_DEC, rank_body, jnp.zeros(val.shape, jnp.int32))
        sel = (rank < K_SEL) & visible
        n_before = jnp.dot(before, jnp.where(sel, 1.0, 0.0).astype(jnp.bfloat16), preferred_element_type=f32)
        for k in range(K_SEL):
            hit = sel & (n_before == float(k))
            idx_ref[g, pl.ds(k, 1), :] = jnp.sum(jnp.where(hit, s_io.astype(f32), 0.0), axis=0,
                                                 keepdims=True).astype(jnp.int32)


def _nsa_decode_sel_kernel(pt_ref, ix_ref, *refs):
    f32, bf16 = jnp.float32, jnp.bfloat16
    n_blk = N_KV * K_SEL
    k_blocks, v_blocks = refs[:n_blk], refs[n_blk:2 * n_blk]
    (q_ref, oc_ref, gl_ref, ksn_ref, vsn_ref, kwn_ref, vwn_ref, kwin_ref, vwin_ref, near_ref, b0_ref, bw_ref,
     o_ref, kwo_ref, vwo_ref) = refs[2 * n_blk:]
    b = pl.program_id(0)
    q = q_ref[0]
    qf = q.astype(f32)
    round_bf = lambda a: a.astype(bf16).astype(f32)
    group1 = lax.broadcasted_iota(jnp.int32, (N_HEADS, LANES), 0) >= GROUP
    second_half = lax.broadcasted_iota(jnp.int32, (N_HEADS, LANES), 1) >= L_SEL
    neg_tile = jnp.full((N_HEADS, LANES), NEG_BIG, f32)
    no_rows = jnp.zeros((HEAD_DIM, PAGE_SIZE), bf16)

    def block_bias(s):
        tile = jnp.where(s == CUR_BLOCK, neg_tile,
                         jnp.where(s == CUR_BLOCK - 1, near_ref[1],
                                   jnp.where(s == CUR_BLOCK - 2, near_ref[0], near_ref[2])))
        return jnp.where(second_half == ((s & 1) == 1), tile, neg_tile)

    def group_rows(x, g):
        x = x.astype(bf16)
        return jnp.concatenate([x, no_rows] if g == 0 else [no_rows, x], axis=0)

    scores, values, has_new = [], [], []
    for g in range(N_KV):
        ids = [ix_ref[b, g * K_SEL + k] for k in range(K_SEL)]
        kt = jnp.concatenate([group_rows(k_blocks[g * K_SEL + k][0], g) for k in range(K_SEL)], axis=1)
        values.append(jnp.concatenate([group_rows(v_blocks[g * K_SEL + k][0], g) for k in range(K_SEL)], axis=1))
        bias = jnp.concatenate([block_bias(ids[k]) for k in range(K_SEL)], axis=1)
        scores.append(jnp.dot(q, kt, preferred_element_type=f32) + bias)
        flag = ids[0] == CUR_BLOCK
        for k in range(1, K_SEL):
            flag = flag | (ids[k] == CUR_BLOCK)
        has_new.append(jnp.where(flag, 0.0, NEG_BIG))
    s = jnp.where(group1[:, :1], scores[1], scores[0])
    s_new = jnp.sum(qf * round_bf(ksn_ref[0]), axis=-1, keepdims=True) + b0_ref[...]
    s_new = s_new + jnp.where(group1, has_new[1], has_new[0])
    m = jnp.maximum(jnp.max(s, axis=-1, keepdims=True), s_new)
    p = jnp.exp(s - m[:, :1])
    p_new = jnp.exp(s_new - m)
    l = jnp.sum(p, axis=-1, keepdims=True) + p_new
    pb = p.astype(bf16)
    o_sel = jnp.where(group1, _dot_nt(pb, values[1]), _dot_nt(pb, values[0]))
    o_sel = (o_sel + round_bf(p_new) * round_bf(vsn_ref[0])) / l

    kwin, vwin = kwin_ref[0], vwin_ref[0]
    s = jnp.dot(q, kwin.astype(bf16), preferred_element_type=f32) + bw_ref[...]
    s_new = jnp.sum(qf * round_bf(kwn_ref[0]), axis=-1, keepdims=True) + b0_ref[...]
    m = jnp.maximum(jnp.max(s, axis=-1, keepdims=True), s_new)
    p = jnp.exp(s - m[:, :1])
    p_new = jnp.exp(s_new - m)
    l = jnp.sum(p, axis=-1, keepdims=True) + p_new
    o_win = _dot_nt(p.astype(bf16), vwin.astype(bf16))
    o_win = (o_win + round_bf(p_new) * round_bf(vwn_ref[0])) / l

    gates = jax.nn.sigmoid(gl_ref[0])
    o_ref[0] = (oc_ref[0] * gates[0] + o_sel * gates[1] + o_win * gates[2]).astype(bf16)

    wb = kwin.shape[1]
    last = lax.broadcasted_iota(jnp.int32, kwin.shape, 1) == wb - 1
    as_column = lambda row: jnp.broadcast_to(row, (KV_WIDTH, KV_WIDTH)).T[:, :1]
    kwo_ref[0] = jnp.where(last, as_column(kwn_ref[0]), pltpu.roll(kwin, wb - 1, 1))
    vwo_ref[0] = jnp.where(last, as_column(vwn_ref[0]), pltpu.roll(vwin, wb - 1, 1))


def _nsa_decode(qpad, small, kc, vc, ks_new, vs_new, kw_new, vw_new, cache_k_sel, cache_v_sel,
                cache_k_win, cache_v_win, page_table, rel_bias):
    f32 = jnp.float32
    b = qpad.shape[0]
    n_c = kc.shape[1]
    n_pool = cache_k_sel.shape[0]
    wb = cache_k_win.shape[1]
    tab = rel_bias.astype(f32)[_bucket_table()].T
    dist_c = PAST_LEN - (L_CMP - 1) - D_CMP * np.arange(n_c)
    bias_c = jnp.where(jnp.asarray(dist_c >= 0)[None, :], tab[:, np.clip(dist_c, 0, MAX_DIST)], NEG_BIG)
    j = np.arange(LANES) % L_SEL
    near = jnp.stack([tab[:, 2 * L_SEL - j], tab[:, L_SEL - j],
                      jnp.broadcast_to(tab[:, MAX_DIST:], (N_HEADS, LANES))])
    bias0 = jnp.broadcast_to(tab[:, :1], (N_HEADS, LANES))
    bias_w = tab[:, np.minimum(wb - np.arange(wb), MAX_DIST)]
    q3 = qpad.reshape(b, N_HEADS, LANES)

    per_seq = lambda *shape: pl.BlockSpec((1,) + shape, lambda bi, *_: (bi,) + (0,) * len(shape))
    full = lambda a: pl.BlockSpec(a.shape, lambda bi, *_: (0,) * a.ndim)
    oc, imp = pl.pallas_call(
        _nsa_decode_cmp_kernel,
        out_shape=[jax.ShapeDtypeStruct((b, N_HEADS, LANES), f32),
                   jax.ShapeDtypeStruct((b, 1, N_KV * N_SEL_DEC_PAD), f32)],
        grid=(b,),
        in_specs=[per_seq(N_HEADS, LANES), per_seq(n_c, KV_WIDTH), per_seq(n_c, KV_WIDTH), full(bias_c)],
        out_specs=[per_seq(N_HEADS, LANES), per_seq(1, N_KV * N_SEL_DEC_PAD)],
        compiler_params=pltpu.CompilerParams(dimension_semantics=("arbitrary",)),
        name="nsa_decode_cmp",
    )(q3, kc, vc, bias_c)

    idx = pl.pallas_call(
        _nsa_decode_topk_kernel,
        out_shape=jax.ShapeDtypeStruct((N_KV, K_SEL, b), jnp.int32),
        scratch_shapes=[pltpu.VMEM((N_SEL_DEC_PAD, b), f32)],
        name="nsa_decode_topk",
    )(imp.reshape(b, N_KV * N_SEL_DEC_PAD))
    idx = idx.reshape(N_KV * K_SEL, b).T

    halves = PAGE_SIZE // L_SEL

    def sel_block(i):
        def index_map(bi, pt, ix):
            bs = jnp.minimum(bi, b - 1)
            s = jnp.clip(ix[bs, i], 0, CUR_BLOCK - 1)
            return (jnp.clip(pt[bs, s // halves], 0, n_pool - 1) * N_KV + i // K_SEL, 0, 0)
        return pl.BlockSpec((1, HEAD_DIM, PAGE_SIZE), index_map)

    keys_minor = lambda c: c.transpose(0, 2, 3, 1)
    page_groups = lambda c: keys_minor(c).reshape(n_pool * N_KV, HEAD_DIM, PAGE_SIZE)
    win_t = lambda c: keys_minor(c).reshape(b, KV_WIDTH, wb)
    glog = jnp.broadcast_to(small[:, :3 * N_HEADS].reshape(b, N_HEADS, 3).transpose(0, 2, 1)[..., None],
                            (b, 3, N_HEADS, LANES))
    new_row = lambda a: a.reshape(b, 1, KV_WIDTH)
    n_blk = N_KV * K_SEL
    o_pad, k_win, v_win = pl.pallas_call(
        _nsa_decode_sel_kernel,
        out_shape=[jax.ShapeDtypeStruct((b, N_HEADS, LANES), jnp.bfloat16),
                   jax.ShapeDtypeStruct((b, KV_WIDTH, wb), f32), jax.ShapeDtypeStruct((b, KV_WIDTH, wb), f32)],
        grid_spec=pltpu.PrefetchScalarGridSpec(
            num_scalar_prefetch=2,
            grid=(b,),
            in_specs=[sel_block(i) for i in range(n_blk)] * 2
                     + [per_seq(N_HEADS, LANES), per_seq(N_HEADS, LANES), per_seq(3, N_HEADS, LANES)]
                     + [per_seq(1, KV_WIDTH)] * 4 + [per_seq(KV_WIDTH, wb)] * 2
                     + [full(near), full(bias0), full(bias_w)],
            out_specs=[per_seq(N_HEADS, LANES), per_seq(KV_WIDTH, wb), per_seq(KV_WIDTH, wb)]),
        compiler_params=pltpu.CompilerParams(dimension_semantics=("arbitrary",)),
        name="nsa_decode_sel",
    )(page_table, idx, *([page_groups(cache_k_sel)] * n_blk), *([page_groups(cache_v_sel)] * n_blk),
      q3, oc, glog, new_row(ks_new), new_row(vs_new), new_row(kw_new), new_row(vw_new),
      win_t(cache_k_win), win_t(cache_v_win), near, bias0, bias_w)
    rows_major = lambda a: a.reshape(b, N_KV, HEAD_DIM, wb).transpose(0, 3, 1, 2)
    return o_pad.reshape(b, QPAD_WIDTH), rows_major(k_win), rows_major(v_win)


MLSTM_CHUNK = 128
MLSTM_DEC_TILE = 8
CONV_TAIL = 8


def _mlstm_norm_gate(h, zo, norm_g):
    return h * lax.rsqrt(jnp.mean(h * h, axis=-1, keepdims=True) + EPS) * norm_g * jax.nn.sigmoid(zo)


def _mlstm_prompt_kernel(zqk_ref, zv_ref, zo_ref, sm_ref, cw_ref, cb_ref, wq_ref, wk_ref, gb_ref, ng_ref,
                         o_ref, co_ref, no_ref, mo_ref, xbuf_ref, c_ref, n_ref, m_ref):
    f32, bf16 = jnp.float32, jnp.bfloat16
    L = MLSTM_CHUNK
    c = pl.program_id(1)

    @pl.when(c == 0)
    def _():
        xbuf_ref[:CONV_TAIL] = jnp.zeros((CONV_TAIL, M_WIDTH), f32)
        c_ref[...] = jnp.zeros(c_ref.shape, f32)
        n_ref[...] = jnp.zeros(n_ref.shape, f32)
        m_ref[...] = jnp.zeros(m_ref.shape, f32)

    x = zqk_ref[...]
    xbuf_ref[CONV_TAIL:] = x
    conv = cb_ref[...]
    for j in range(CONV_W):
        conv = conv + xbuf_ref[pl.ds(CONV_TAIL - (CONV_W - 1) + j, L), :] * cw_ref[j:j + 1, :]
    xbuf_ref[:CONV_TAIL] = x[L - CONV_TAIL:]
    a = jax.nn.silu(conv).astype(bf16)

    t_io = lax.broadcasted_iota(jnp.int32, (L, L), 0)
    s_io = lax.broadcasted_iota(jnp.int32, (L, L), 1)
    causal = t_io >= s_io
    sm = sm_ref[...]
    for h in range(M_HEADS):
        hs = slice(h * M_DH, (h + 1) * M_DH)
        q = jnp.dot(a[:, hs], wq_ref[h], preferred_element_type=f32)
        k = jnp.dot(a[:, hs], wk_ref[h], preferred_element_type=f32) * (M_DH ** -0.5)
        v = zv_ref[:, hs]
        qb, kb, vb = q.astype(bf16), k.astype(bf16), v.astype(bf16)
        col = 3 * N_HEADS + h
        ig = jnp.broadcast_to(sm[:, col:col + 1], (L, L)) + gb_ref[0:1, hs]
        lf = jax.nn.log_sigmoid(jnp.broadcast_to(sm[:, col + M_HEADS:col + M_HEADS + 1], (L, L)) + gb_ref[1:2, hs])
        bcum = lf
        sh = 1
        while sh < L:
            bcum = bcum + jnp.where(t_io >= sh, pltpu.roll(bcum, sh, 0), 0.0)
            sh *= 2
        m_old = m_ref[h:h + 1, :]
        c_old = c_ref[h]
        n_old = n_ref[h:h + 1, :]
        dmat = jnp.where(causal, bcum - bcum.T + ig.T, -jnp.inf)
        inter = bcum + m_old
        m_t = jnp.maximum(jnp.max(dmat, axis=-1, keepdims=True), inter)
        sc = _dot_nt(qb, kb) * jnp.exp(dmat - m_t)
        decay = jnp.exp(inter - m_t)
        num = decay * _dot_nt(qb, c_old.astype(bf16)) + jnp.dot(sc.astype(bf16), vb, preferred_element_type=f32)
        den = decay * jnp.sum(q * n_old, axis=-1, keepdims=True) + jnp.sum(sc, axis=-1, keepdims=True)
        hh = num / jnp.maximum(jnp.abs(den), jnp.exp(-m_t))
        o_ref[:, hs] = _mlstm_norm_gate(hh, zo_ref[:, hs], ng_ref[:, hs])

        b_last = bcum[L - 1:L, :]
        w_log = b_last - bcum + ig
        m_new = jnp.maximum(b_last + m_old, jnp.max(w_log, axis=0, keepdims=True))
        w = jnp.exp(w_log - m_new)
        carry = jnp.exp(b_last + m_old - m_new)
        c_ref[h] = carry * c_old + jnp.dot((w * v).T.astype(bf16), kb, preferred_element_type=f32)
        n_ref[h:h + 1, :] = carry * n_old + jnp.sum(w * k, axis=0, keepdims=True)
        m_ref[h:h + 1, :] = m_new

    @pl.when(c == pl.num_programs(1) - 1)
    def _():
        co_ref[0] = c_ref[...]
        no_ref[0] = n_ref[...]
        mo_ref[0] = m_ref[...]


def _mlstm_weights(conv_w, conv_b, wq, wk, gate_bias, norm_g):
    gb = jnp.repeat(gate_bias.astype(jnp.float32), M_DH, axis=1)
    return (conv_w, conv_b.reshape(1, M_WIDTH), wq.astype(jnp.bfloat16), wk.astype(jnp.bfloat16), gb,
            norm_g.reshape(1, M_WIDTH))


def _mlstm_prompt(zmqk, zmv, zmo, small, weights, b, t):
    L = MLSTM_CHUNK
    n_ch = t // L
    rows = lambda width: pl.BlockSpec((L, width), lambda bi, ci: (bi * n_ch + ci, 0))
    full = lambda a: pl.BlockSpec(a.shape, lambda bi, ci: (0,) * a.ndim)
    state = lambda *shape: pl.BlockSpec((1,) + shape, lambda bi, ci: (bi,) + (0,) * len(shape))
    f32 = jnp.float32
    return pl.pallas_call(
        _mlstm_prompt_kernel,
        out_shape=[jax.ShapeDtypeStruct((b * t, M_WIDTH), f32), jax.ShapeDtypeStruct((b, M_HEADS, M_DH, M_DH), f32),
                   jax.ShapeDtypeStruct((b, 8, M_DH), f32), jax.ShapeDtypeStruct((b, 8, M_DH), f32)],
        grid=(b, n_ch),
        in_specs=[rows(M_WIDTH), rows(M_WIDTH), rows(M_WIDTH), rows(LANES)] + [full(a) for a in weights],
        out_specs=[rows(M_WIDTH), state(M_HEADS, M_DH, M_DH), state(8, M_DH), state(8, M_DH)],
        scratch_shapes=[pltpu.VMEM((CONV_TAIL + L, M_WIDTH), f32), pltpu.VMEM((M_HEADS, M_DH, M_DH), f32),
                        pltpu.VMEM((8, M_DH), f32), pltpu.VMEM((8, M_DH), f32)],
        compiler_params=pltpu.CompilerParams(dimension_semantics=("arbitrary", "arbitrary")),
        name="mlstm_prompt",
    )(zmqk, zmv, zmo, small, *weights)


def _mlstm_decode_kernel(zqk_ref, conv_ref, zv_ref, zo_ref, sm_ref, c_ref, n_ref, m_ref,
                         cw_ref, cb_ref, wq_ref, wk_ref, gb_ref, ng_ref, o_ref, co_ref, no_ref, mo_ref):
    f32, bf16 = jnp.float32, jnp.bfloat16
    nt = MLSTM_DEC_TILE
    conv = cb_ref[...] + zqk_ref[...] * cw_ref[CONV_W - 1:CONV_W, :]
    for j in range(CONV_W - 1):
        conv = conv + conv_ref[j] * cw_ref[j:j + 1, :]
    a = jax.nn.silu(conv).astype(bf16)
    sm = sm_ref[...]
    lane = lax.broadcasted_iota(jnp.int32, (M_DH, M_DH), 1)
    pad_rows = jnp.zeros((M_DH - nt, M_DH), f32)
    for h in range(M_HEADS):
        hs = slice(h * M_DH, (h + 1) * M_DH)
        q = jnp.dot(a[:, hs], wq_ref[h], preferred_element_type=f32)
        k = jnp.dot(a[:, hs], wk_ref[h], preferred_element_type=f32) * (M_DH ** -0.5)
        v = zv_ref[:, hs]
        col = 3 * N_HEADS + h
        ig = jnp.broadcast_to(sm[:, col:col + 1], (nt, M_DH)) + gb_ref[0:1, hs]
        lf = jax.nn.log_sigmoid(jnp.broadcast_to(sm[:, col + M_HEADS:col + M_HEADS + 1], (nt, M_DH)) + gb_ref[1:2, hs])
        m_old = m_ref[:, hs]
        n_old = n_ref[:, hs]
        m_new = jnp.maximum(lf + m_old, ig)
        decay = jnp.exp(lf + m_old - m_new)
        w = jnp.exp(ig - m_new)
        sc = jnp.sum(q * k, axis=-1, keepdims=True) * w
        den = decay * jnp.sum(n_old * q, axis=-1, keepdims=True) + sc
        wv_t = jnp.concatenate([w * v, pad_rows], axis=0).T
        cq_t = jnp.zeros((M_DH, M_DH), f32)
        for i in range(nt):
            c_old = c_ref[i, h]
            cq = jnp.sum(c_old * q[i:i + 1, :], axis=-1, keepdims=True)
            cq_t = jnp.where(lane == i, cq, cq_t)
            co_ref[i, h] = decay[i:i + 1, :] * c_old + wv_t[:, i:i + 1] * k[i:i + 1, :]
        num = decay * cq_t.T[:nt] + sc * v
        hh = num / jnp.maximum(jnp.abs(den), jnp.exp(-m_new))
        o_ref[:, hs] = _mlstm_norm_gate(hh, zo_ref[:, hs], ng_ref[:, hs])
        no_ref[:, hs] = decay * n_old + w * k
        mo_ref[:, hs] = m_new


def _mlstm_decode(zmqk, zmv, zmo, small, state_c, state_n, state_m, state_conv, weights):
    b = zmqk.shape[0]
    nt = MLSTM_DEC_TILE
    f32 = jnp.float32
    rows = lambda width: pl.BlockSpec((nt, width), lambda i: (i, 0))
    full = lambda a: pl.BlockSpec(a.shape, lambda i: (0,) * a.ndim)
    cspec = pl.BlockSpec((nt, M_HEADS, M_DH, M_DH), lambda i: (i, 0, 0, 0))
    conv_t = state_conv.transpose(1, 0, 2)
    m_rep = jnp.repeat(state_m, M_DH, axis=1)
    return pl.pallas_call(
        _mlstm_decode_kernel,
        out_shape=[jax.ShapeDtypeStruct((b, M_WIDTH), f32), jax.ShapeDtypeStruct(state_c.shape, f32),
                   jax.ShapeDtypeStruct((b, M_WIDTH), f32), jax.ShapeDtypeStruct((b, M_WIDTH), f32)],
        grid=(b // nt,),
        in_specs=[rows(M_WIDTH), pl.BlockSpec((CONV_W - 1, nt, M_WIDTH), lambda i: (0, i, 0)), rows(M_WIDTH),
                  rows(M_WIDTH), rows(LANES), cspec, rows(M_WIDTH), rows(M_WIDTH)] + [full(a) for a in weights],
        out_specs=[rows(M_WIDTH), cspec, rows(M_WIDTH), rows(M_WIDTH)],
        compiler_params=pltpu.CompilerParams(dimension_semantics=("arbitrary",)),
        name="mlstm_decode",
    )(zmqk, conv_t, zmv, zmo, small, state_c, state_n.reshape(b, M_WIDTH), m_rep, *weights)


def _merge_kernel(x_ref, on_ref, om_ref, zm_ref, wn_ref, wm_ref, wo_ref, o_ref):
    zm = zm_ref[...]
    g_a = jax.nn.sigmoid(zm[:, :D_MODEL])
    g_b = jax.nn.sigmoid(zm[:, D_MODEL:])
    ya = jnp.dot(on_ref[...], wn_ref[...], preferred_element_type=jnp.float32)
    yb = jnp.dot(om_ref[...].astype(jnp.bfloat16), wm_ref[...], preferred_element_type=jnp.float32)
    y = (g_a * ya + g_b * yb).astype(jnp.bfloat16)
    o_ref[...] = x_ref[...] + jnp.dot(y, wo_ref[...], preferred_element_type=jnp.float32)


def _merge(x, o_nsa, o_mlstm, zmerge, w_proj_nsa, w_proj_mlstm, w_out):
    n = x.shape[0]
    tm = min(PROJ_TOKEN_TILE, n)
    row = lambda width: pl.BlockSpec((tm, width), lambda i: (i, 0))
    full = lambda a: pl.BlockSpec(a.shape, lambda i: (0, 0))
    return pl.pallas_call(
        _merge_kernel,
        out_shape=jax.ShapeDtypeStruct((n, D_MODEL), jnp.float32),
        grid=(n // tm,),
        in_specs=[row(D_MODEL), row(o_nsa.shape[1]), row(M_WIDTH), row(2 * D_MODEL),
                  full(w_proj_nsa), full(w_proj_mlstm), full(w_out)],
        out_specs=row(D_MODEL),
        compiler_params=pltpu.CompilerParams(dimension_semantics=("arbitrary",),
                                             vmem_limit_bytes=V7X_VMEM_LIMIT_BYTES),
        name="merge_out_proj",
    )(x, o_nsa, o_mlstm, zmerge, w_proj_nsa, w_proj_mlstm, w_out)


def _kv_rows(a, b, t):
    return a.reshape(b, t, N_KV, HEAD_DIM)


def _mix_prompt(proj, b, t, cmp_k, cmp_v, mlstm_w, rel_bias):
    qpad, kvb, kc_rows, vc_rows, ks_rows, vs_rows, kw_rows, vw_rows, zmqk, zmv, zmo, zmerge, small = proj
    kc, vc = _compress_rows(kc_rows, vc_rows, b, t, cmp_k, cmp_v)
    o_nsa = _nsa_prompt(qpad, small, kvb, kc, vc.transpose(0, 2, 1), rel_bias, b, t)
    o_mlstm, c_f, n_f, m_f = _mlstm_prompt(zmqk, zmv, zmo, small, mlstm_w, b, t)
    n_keep = min(WINDOW, t)
    states = (_kv_rows(kc_rows, b, t), _kv_rows(vc_rows, b, t), _kv_rows(ks_rows, b, t), _kv_rows(vs_rows, b, t),
              _kv_rows(kw_rows, b, t)[:, t - n_keep:], _kv_rows(vw_rows, b, t)[:, t - n_keep:],
              c_f, n_f[:, :M_HEADS], m_f[:, :M_HEADS, 0], zmqk.reshape(b, t, M_WIDTH)[:, t - (CONV_W - 1):])
    return o_nsa, o_mlstm, zmerge, states


def _mix_decode(proj, b, caches, mlstm_state, page_table, cmp_k, cmp_v, mlstm_w, rel_bias):
    cache_k_cmp, cache_v_cmp, cache_k_sel, cache_v_sel, cache_k_win, cache_v_win = caches
    state_c, state_n, state_m, state_conv = mlstm_state
    qpad, _, kc_new, vc_new, ks_new, vs_new, kw_new, vw_new, zmqk, zmv, zmo, zmerge, small = proj
    kc, vc = _compress_pages(cache_k_cmp, cache_v_cmp, page_table, cmp_k, cmp_v)
    o_nsa, k_win, v_win = _nsa_decode(qpad, small, kc, vc, ks_new, vs_new, kw_new, vw_new, cache_k_sel, cache_v_sel,
                                      cache_k_win, cache_v_win, page_table, rel_bias)
    o_mlstm, c_n, n_n, m_rep = _mlstm_decode(zmqk, zmv, zmo, small, state_c, state_n, state_m, state_conv, mlstm_w)
    wb = cache_k_win.shape[1]
    conv_new = jnp.concatenate([state_conv[:, 1:], zmqk[:, None, :]], axis=1)
    states = (_kv_rows(kc_new, b, 1), _kv_rows(vc_new, b, 1), _kv_rows(ks_new, b, 1), _kv_rows(vs_new, b, 1),
              _kv_rows(k_win, b, wb), _kv_rows(v_win, b, wb),
              c_n, n_n.reshape(b, M_HEADS, M_DH), m_rep[:, ::M_DH], conv_new)
    return o_nsa, o_mlstm, zmerge, states


def kernel(x_prompt, x_sample, cache_k_cmp, cache_v_cmp, cache_k_sel, cache_v_sel, cache_k_win, cache_v_win, state_mlstm_C, state_mlstm_n, state_mlstm_m, state_mlstm_conv, page_table, norm_ffn1, ffn1_w_in, ffn1_w_out, norm_mix, w_mix_in, cmp_pos_k, cmp_pos_v, cmp_phi_k1, cmp_phi_k2, cmp_phi_v1, cmp_phi_v2, rel_bias, mlstm_conv_w, mlstm_conv_b, mlstm_wq, mlstm_wk, mlstm_gate_bias, mlstm_norm, w_proj_nsa, w_proj_mlstm, w_out, norm_ffn2, ffn2_w_in, ffn2_w_out, norm_final):
    assert x_sample.shape[1] == DEC_SEQ == 1
    bf = lambda w: w.astype(jnp.bfloat16)
    w1i, w1o, w2i, w2o = bf(ffn1_w_in), bf(ffn1_w_out), bf(ffn2_w_in), bf(ffn2_w_out)
    w_mix = _regroup_mix_weight(w_mix_in)
    w_nsa_out, wm, wo = bf(_pad_nsa_out_weight(w_proj_nsa)), bf(w_proj_mlstm), bf(w_out)
    cmp_k = _compress_weights(cmp_pos_k, cmp_phi_k1, cmp_phi_k2)
    cmp_v = _compress_weights(cmp_pos_v, cmp_phi_v1, cmp_phi_v2)
    mlstm_w = _mlstm_weights(mlstm_conv_w, mlstm_conv_b, mlstm_wq, mlstm_wk, mlstm_gate_bias, mlstm_norm)

    def layer(x3, mix_fn):
        b, t, _ = x3.shape
        x = x3.reshape(b * t, D_MODEL)
        x1 = _ffn(x, norm_ffn1, w1i, w1o, norm_final, final_norm=False)
        proj = _project(x1, norm_mix, w_mix)
        o_nsa, o_mlstm, zmerge, states = mix_fn(proj, b, t)
        x2 = _merge(x1, o_nsa, o_mlstm, zmerge, w_nsa_out, wm, wo)
        y = _ffn(x2, norm_ffn2, w2i, w2o, norm_final, final_norm=True)
        return y.reshape(b, t, D_MODEL), states

    y_prompt, st_p = layer(x_prompt, lambda proj, b, t: _mix_prompt(proj, b, t, cmp_k, cmp_v, mlstm_w, rel_bias))
    y_sample, st_s = layer(x_sample, lambda proj, b, t: _mix_decode(
        proj, b, (cache_k_cmp, cache_v_cmp, cache_k_sel, cache_v_sel, cache_k_win, cache_v_win),
        (state_mlstm_C, state_mlstm_n, state_mlstm_m, state_mlstm_conv), page_table, cmp_k, cmp_v, mlstm_w, rel_bias))
    k_cmp_p, v_cmp_p, k_sel_p, v_sel_p, k_win_p, v_win_p, C_p, n_p, m_p, conv_p = st_p
    k_cmp_s, v_cmp_s, k_sel_s, v_sel_s, k_win_s, v_win_s, C_s, n_s, m_s, conv_s = st_s
    return (y_prompt, y_sample, k_cmp_p, k_cmp_s, v_cmp_p, v_cmp_s, k_sel_p, k_sel_s, v_sel_p, v_sel_s,
            k_win_p, k_win_s, v_win_p, v_win_s, C_p, C_s, n_p, n_s, m_p, m_s, conv_p, conv_s)
```

```python
import math
from functools import partial

import numpy as np
import jax
import jax.numpy as jnp
from jax import lax
from jax.experimental import pallas as pl
from jax.experimental.pallas import tpu as pltpu

D_MODEL = 1024
SEQ = 8192
DEC_SEQ = 1
PAST_LEN = 8192
PAGE_SIZE = 128
N_HEADS = 8
N_KV = 2
GROUP = N_HEADS // N_KV
HEAD_DIM = 64
NSA_WIDTH = N_HEADS * HEAD_DIM
KV_WIDTH = N_KV * HEAD_DIM
L_CMP = 32
D_CMP = 16
CMP_RATIO = L_CMP // D_CMP
L_SEL = 64
K_SEL = 16
WINDOW = 512
Q_BLOCK = 128
FORCE_SCORE = 1000.0
N_BUCKETS = 32
MAX_DIST = 128
M_HEADS = 4
M_DH = 128
M_WIDTH = M_HEADS * M_DH
CONV_W = 4
M_CHUNK = 64
D_FF = 2816
EPS = 1e-6

V7X_VMEM_LIMIT_BYTES = 56 * 1024 * 1024
LANES = 128

QPAD_WIDTH = N_HEADS * LANES
N_SEL_PAD = 128
FAR_TILES = 8
NEG_BIG = -1e30
SEL_OFF = -32768.0

FFN_TOKEN_TILE = 512
FFN_FF_TILE = 1408
PROJ_TOKEN_TILE = 512

_MIX_SIZES = (NSA_WIDTH, KV_WIDTH, KV_WIDTH, KV_WIDTH, KV_WIDTH, KV_WIDTH, KV_WIDTH, 3 * N_HEADS,
              M_WIDTH, M_WIDTH, 2 * M_HEADS, M_WIDTH, 2 * D_MODEL)
_MIX_OFFS = np.concatenate([[0], np.cumsum(_MIX_SIZES)]).tolist()
_PROJ_SEGS = (("kv", 6 * KV_WIDTH), ("mqk", M_WIDTH), ("mv", M_WIDTH),
              ("mo", M_WIDTH), ("merge", 2 * D_MODEL), ("small", LANES))


def _rms(x, g):
    return x * lax.rsqrt(jnp.mean(x * x, axis=-1, keepdims=True) + EPS) * g


def _ffn_kernel(x_ref, g_ref, wg_ref, wu_ref, wo_ref, gf_ref, o_ref, h_ref, acc_ref, *, final_norm):
    j = pl.program_id(1)

    @pl.when(j == 0)
    def _():
        h_ref[...] = _rms(x_ref[...], g_ref[...]).astype(jnp.bfloat16)
        acc_ref[...] = jnp.zeros_like(acc_ref)

    h = h_ref[...]
    gate = jnp.dot(h, wg_ref[...], preferred_element_type=jnp.float32)
    up = jnp.dot(h, wu_ref[...], preferred_element_type=jnp.float32)
    a = (jax.nn.silu(gate) * up).astype(jnp.bfloat16)
    acc_ref[...] += jnp.dot(a, wo_ref[...], preferred_element_type=jnp.float32)

    @pl.when(j == pl.num_programs(1) - 1)
    def _():
        y = x_ref[...] + 0.5 * acc_ref[...]
        if final_norm:
            y = _rms(y, gf_ref[...])
        o_ref[...] = y


def _ffn(x, g, w_in, w_out, g_final, *, final_norm):
    n = x.shape[0]
    tm = min(FFN_TOKEN_TILE, n)
    nj = D_FF // FFN_FF_TILE
    return pl.pallas_call(
        partial(_ffn_kernel, final_norm=final_norm),
        out_shape=jax.ShapeDtypeStruct((n, D_MODEL), jnp.float32),
        grid=(n // tm, nj),
        in_specs=[
            pl.BlockSpec((tm, D_MODEL), lambda i, j: (i, 0)),
            pl.BlockSpec((1, D_MODEL), lambda i, j: (0, 0)),
            pl.BlockSpec((D_MODEL, FFN_FF_TILE), lambda i, j: (0, j)),
            pl.BlockSpec((D_MODEL, FFN_FF_TILE), lambda i, j: (0, j + D_FF // FFN_FF_TILE)),
            pl.BlockSpec((FFN_FF_TILE, D_MODEL), lambda i, j: (j, 0)),
            pl.BlockSpec((1, D_MODEL), lambda i, j: (0, 0)),
        ],
        out_specs=pl.BlockSpec((tm, D_MODEL), lambda i, j: (i, 0)),
        scratch_shapes=[pltpu.VMEM((tm, D_MODEL), jnp.bfloat16), pltpu.VMEM((tm, D_MODEL), jnp.float32)],
        compiler_params=pltpu.CompilerParams(dimension_semantics=("arbitrary", "arbitrary"),
                                             vmem_limit_bytes=V7X_VMEM_LIMIT_BYTES),
        name="ffn",
    )(x, g.reshape(1, D_MODEL), w_in, w_in, w_out, g_final.reshape(1, D_MODEL))


def _proj_kernel(x_ref, g_ref, w_ref, qpad_ref, kvb_ref, *o_refs):
    h = _rms(x_ref[...], g_ref[...]).astype(jnp.bfloat16)
    zq = jnp.dot(h, w_ref[:, :QPAD_WIDTH], preferred_element_type=jnp.float32)
    qpad_ref[...] = (zq * (HEAD_DIM ** -0.5)).astype(jnp.bfloat16)
    off = QPAD_WIDTH
    o_refs = list(o_refs)
    for name, width in _PROJ_SEGS:
        z = jnp.dot(h, w_ref[:, off:off + width], preferred_element_type=jnp.float32)
        if name == "kv":
            kvb_ref[...] = z.astype(jnp.bfloat16)
            for i in range(6):
                o_refs.pop(0)[...] = z[:, i * KV_WIDTH:(i + 1) * KV_WIDTH]
        else:
            o_refs.pop(0)[...] = z
        off += width


def _regroup_mix_weight(w_mix_in):
    seg = lambda i: w_mix_in[:, _MIX_OFFS[i]:_MIX_OFFS[i + 1]]
    wq = seg(0)
    zeros = jnp.zeros((D_MODEL, HEAD_DIM), w_mix_in.dtype)
    qpad = []
    for h in range(N_HEADS):
        wh = wq[:, h * HEAD_DIM:(h + 1) * HEAD_DIM]
        qpad += [wh, zeros] if h // GROUP == 0 else [zeros, wh]
    small = jnp.concatenate([seg(7), seg(10), jnp.zeros((D_MODEL, LANES - 3 * N_HEADS - 2 * M_HEADS), w_mix_in.dtype)], axis=1)
    cols = qpad + [seg(i) for i in range(1, 7)] + [seg(8), seg(9), seg(11), seg(12), small]
    return jnp.concatenate(cols, axis=1).astype(jnp.bfloat16)


def _project(x, g, w_regrouped):
    n = x.shape[0]
    tm = min(PROJ_TOKEN_TILE, n)
    wcols = w_regrouped.shape[1]
    row = lambda width: pl.BlockSpec((tm, width), lambda i: (i, 0))
    widths = []
    for name, width in _PROJ_SEGS:
        widths += [KV_WIDTH] * 6 if name == "kv" else [width]
    return pl.pallas_call(
        _proj_kernel,
        out_shape=[jax.ShapeDtypeStruct((n, QPAD_WIDTH), jnp.bfloat16),
                   jax.ShapeDtypeStruct((n, 6 * KV_WIDTH), jnp.bfloat16)]
                  + [jax.ShapeDtypeStruct((n, width), jnp.float32) for width in widths],
        grid=(n // tm,),
        in_specs=[
            row(D_MODEL),
            pl.BlockSpec((1, D_MODEL), lambda i: (0, 0)),
            pl.BlockSpec((D_MODEL, wcols), lambda i: (0, 0)),
        ],
        out_specs=[row(QPAD_WIDTH), row(6 * KV_WIDTH)] + [row(width) for width in widths],
        compiler_params=pltpu.CompilerParams(dimension_semantics=("arbitrary",),
                                             vmem_limit_bytes=V7X_VMEM_LIMIT_BYTES),
        name="mix_in_proj",
    )(x, g.reshape(1, D_MODEL), w_regrouped)


def _dot_nt(a, b):
    return lax.dot_general(a, b, (((1,), (1,)), ((), ())), preferred_element_type=jnp.float32)


def _bucket_table():
    d = np.arange(MAX_DIST + 1)
    max_exact = N_BUCKETS // 2
    nf = np.maximum(d, 1).astype(np.float64)
    large = max_exact + (np.log(nf / max_exact) / math.log(MAX_DIST / max_exact) * (N_BUCKETS - max_exact)).astype(np.int64)
    return np.where(d < max_exact, d, np.minimum(large, N_BUCKETS - 1)).astype(np.int32)


def _bias_tables(rel_bias, n_c):
    tab = rel_bias.astype(jnp.float32)[_bucket_table()].T
    i = np.arange(Q_BLOCK)[:, None]
    j = np.arange(Q_BLOCK)[None, :]
    far = tab[:, MAX_DIST][:, None, None]
    t0 = tab[:, np.clip(i - j, 0, MAX_DIST)] - far
    t1 = tab[:, np.minimum(Q_BLOCK + i - j, MAX_DIST)] - far
    tiles = jnp.stack([t0, t1]).reshape(2, N_HEADS * Q_BLOCK, Q_BLOCK)
    lo, hi = -2 * Q_BLOCK // D_CMP, Q_BLOCK // D_CMP
    dist = np.arange(Q_BLOCK)[None, :] - (L_CMP - 1) - D_CMP * np.arange(lo, hi)[:, None]
    band = jnp.where(jnp.asarray(dist >= 0), tab[:, np.clip(dist, 0, MAX_DIST)], NEG_BIG)
    bc = jnp.concatenate([jnp.broadcast_to(tab[:, MAX_DIST][:, None, None], (N_HEADS, n_c + lo, Q_BLOCK)), band,
                          jnp.full((N_HEADS, n_c - hi, Q_BLOCK), NEG_BIG, jnp.float32)], axis=1)
    bct = bc.transpose(1, 0, 2).reshape(2 * n_c, N_HEADS * Q_BLOCK)
    return tiles, bct


def _nsa_prompt_kernel(q_ref, gate_ref, kc_ref, vct_ref, ks_ref, vs_ref, kw_ref, vw_ref, tiles_ref, bct_ref,
                       o_ref, m_ref, l_ref, acc_ref, *, n_c):
    f32, bf16 = jnp.float32, jnp.bfloat16
    qb = pl.program_id(1)
    rows_all = N_HEADS * Q_BLOCK
    qi = lax.broadcasted_iota(jnp.int32, (rows_all, Q_BLOCK), 0) & (Q_BLOCK - 1)
    kj = lax.broadcasted_iota(jnp.int32, (rows_all, Q_BLOCK), 1)
    causal = qi >= kj
    window_edge = kj >= qi
    sig = jax.nn.sigmoid(gate_ref[...])

    s_io = lax.broadcasted_iota(jnp.int32, (N_SEL_PAD, N_KV * Q_BLOCK), 0)
    i_io = lax.broadcasted_iota(jnp.int32, (N_SEL_PAD, N_KV * Q_BLOCK), 1) & (Q_BLOCK - 1)
    qpos = qb * Q_BLOCK + i_io
    cur = 2 * qb + (i_io >= L_SEL).astype(jnp.int32)
    blk_valid = s_io * L_SEL <= qpos
    blk_forced = (s_io == 0) | (s_io == cur) | (s_io == cur - 1)
    ov_s = lax.broadcasted_iota(jnp.int32, (N_SEL_PAD, n_c), 0)
    ov_c = lax.broadcasted_iota(jnp.int32, (N_SEL_PAD, n_c), 1)
    overlap_t = jnp.where((ov_c >= 4 * ov_s - 1) & (ov_c <= 4 * ov_s + 3), 1.0, 0.0).astype(bf16)

    def flash_init():
        m_ref[...] = jnp.full(m_ref.shape, NEG_BIG, f32)
        l_ref[...] = jnp.zeros(l_ref.shape, f32)
        acc_ref[...] = jnp.zeros(acc_ref.shape, f32)

    def flash_step(s, v_tile):
        m_old = m_ref[...]
        m_new = jnp.maximum(m_old, jnp.max(s, axis=-1, keepdims=True))
        alpha = jnp.exp(m_old - m_new)
        p = jnp.exp(s - jnp.concatenate([m_new] * (s.shape[1] // LANES), axis=1))
        l_ref[...] = alpha * l_ref[...] + jnp.sum(p, axis=-1, keepdims=True)
        acc_ref[...] = alpha * acc_ref[...] + jnp.dot(p.astype(bf16), v_tile, preferred_element_type=f32)
        m_ref[...] = m_new

    def key_rows(kt, n_tiles=1):
        return pl.ds(pl.multiple_of(kt * Q_BLOCK, Q_BLOCK), n_tiles * Q_BLOCK)

    q_all = jnp.concatenate([q_ref[:, h * LANES:(h + 1) * LANES] for h in range(N_HEADS)], axis=0)

    st = _dot_nt(kc_ref[0], q_all)
    st = st + bct_ref[pl.ds(pl.multiple_of(n_c - 8 * qb, 8), n_c), :]
    mx = jnp.max(st, axis=0, keepdims=True)
    mx = jnp.where(mx < 0.1 * NEG_BIG, 0.0, mx)
    p = jnp.exp(st - mx)
    p = p / jnp.maximum(jnp.sum(p, axis=0, keepdims=True), 1e-30)
    pb = p.astype(bf16)
    oc_t = jnp.dot(vct_ref[0], pb, preferred_element_type=f32)

    imps = []
    for g in range(N_KV):
        imp_g = None
        for r in range(GROUP):
            h = GROUP * g + r
            part = jnp.dot(overlap_t, pb[:, h * Q_BLOCK:(h + 1) * Q_BLOCK], preferred_element_type=f32)
            imp_g = part if imp_g is None else imp_g + part
        imps.append(imp_g)
    imp = jnp.concatenate(imps, axis=1)
    imp = jnp.where(blk_valid, imp + jnp.where(blk_forced, FORCE_SCORE, 0.0), -jnp.inf)

    s_f = s_io.astype(f32)

    def pick_body(_, carry):
        vals, sel_off = carry
        best = jnp.max(vals, axis=0, keepdims=True)
        first = jnp.min(jnp.where(vals == best, s_f, float(N_SEL_PAD)), axis=0, keepdims=True)
        hit = s_f == first
        return jnp.where(hit, -jnp.inf, vals), jnp.where(hit, 0.0, sel_off)

    _, sel_off = lax.fori_loop(0, K_SEL, pick_body, (imp, jnp.full(imp.shape, SEL_OFF, f32)))
    sel_rows = []
    for g in range(N_KV):
        sel_rows += [sel_off[:, g * Q_BLOCK:(g + 1) * Q_BLOCK].T.astype(bf16)] * GROUP
    lhs_sel = jnp.concatenate([jnp.concatenate(sel_rows, axis=0), q_all], axis=1)

    def sel_scores(kt, n_tiles=1):
        key_blk = jnp.right_shift(lax.broadcasted_iota(jnp.int32, (n_tiles * Q_BLOCK, N_SEL_PAD), 0),
                                  L_SEL.bit_length() - 1)
        blk_lane = lax.broadcasted_iota(jnp.int32, (n_tiles * Q_BLOCK, N_SEL_PAD), 1)
        onehot = jnp.where(blk_lane == 2 * kt + key_blk, 1.0, 0.0).astype(bf16)
        rhs = jnp.concatenate([onehot, ks_ref[0, key_rows(kt, n_tiles), :]], axis=1)
        return _dot_nt(lhs_sel, rhs)

    flash_init()

    n_far = jnp.maximum(qb - 1, 0)

    def far_step(kt, n_tiles):
        flash_step(sel_scores(kt, n_tiles), vs_ref[0, key_rows(kt, n_tiles), :])

    def far_body(i, carry):
        far_step(FAR_TILES * i, FAR_TILES)
        return carry

    lax.fori_loop(0, n_far // FAR_TILES, far_body, 0)
    width = FAR_TILES // 2
    while width >= 1:
        @pl.when((n_far & width) != 0)
        def _(width=width):
            far_step((n_far // (2 * width)) * (2 * width), width)
        width //= 2

    @pl.when(qb >= 1)
    def _():
        flash_step(sel_scores(qb - 1) + tiles_ref[1], vs_ref[0, key_rows(qb - 1), :])

    flash_step(jnp.where(causal, sel_scores(qb) + tiles_ref[0], NEG_BIG), vs_ref[0, key_rows(qb), :])
    o_sel = acc_ref[...] / l_ref[...]

    scores, values = [], []
    for dt in range(WINDOW // Q_BLOCK + 1):
        rows = key_rows(jnp.maximum(qb - dt, 0))
        s = _dot_nt(q_all, kw_ref[0, rows, :])
        if dt < 2:
            s = s + tiles_ref[dt]
        if dt == 0:
            s = jnp.where(causal, s, NEG_BIG)
        else:
            visible = window_edge if dt == WINDOW // Q_BLOCK else True
            s = jnp.where(visible & (qb >= dt), s, NEG_BIG)
        scores.append(s)
        values.append(vw_ref[0, rows, :])
    s = jnp.concatenate(scores, axis=1)
    p = jnp.exp(s - jnp.max(s, axis=-1, keepdims=True))
    o_win = (jnp.dot(p.astype(bf16), jnp.concatenate(values, axis=0), preferred_element_type=f32)
             / jnp.sum(p, axis=-1, keepdims=True))

    for h in range(N_HEADS):
        rs = slice(h * Q_BLOCK, (h + 1) * Q_BLOCK)
        o = (oc_t[:, rs].T * sig[:, 3 * h:3 * h + 1] + o_sel[rs] * sig[:, 3 * h + 1:3 * h + 2]
             + o_win[rs] * sig[:, 3 * h + 2:3 * h + 3])
        o_ref[:, h * LANES:(h + 1) * LANES] = o.astype(bf16)


def _nsa_prompt(qpad, small, kvb, kcb, vct, rel_bias, b, t):
    n_c = t // D_CMP
    n_qb = t // Q_BLOCK
    tiles, bct = _bias_tables(rel_bias, n_c)
    kv3 = kvb.reshape(b, t, 6 * KV_WIDTH)
    rows = lambda width: pl.BlockSpec((Q_BLOCK, width), lambda bi, qi: (bi * n_qb + qi, 0))
    seq = lambda lane_block: pl.BlockSpec((1, t, KV_WIDTH), lambda bi, qi: (bi, 0, lane_block))
    full = lambda a: pl.BlockSpec(a.shape, lambda bi, qi: (0,) * a.ndim)
    rows_all = N_HEADS * Q_BLOCK
    return pl.pallas_call(
        partial(_nsa_prompt_kernel, n_c=n_c),
        out_shape=jax.ShapeDtypeStruct((b * t, QPAD_WIDTH), jnp.bfloat16),
        grid=(b, n_qb),
        in_specs=[rows(QPAD_WIDTH), rows(LANES),
                  pl.BlockSpec((1, n_c, KV_WIDTH), lambda bi, qi: (bi, 0, 0)),
                  pl.BlockSpec((1, KV_WIDTH, n_c), lambda bi, qi: (bi, 0, 0)),
                  seq(2), seq(3), seq(4), seq(5), full(tiles), full(bct)],
        out_specs=rows(QPAD_WIDTH),
        scratch_shapes=[pltpu.VMEM((rows_all, LANES), jnp.float32), pltpu.VMEM((rows_all, LANES), jnp.float32),
                        pltpu.VMEM((rows_all, KV_WIDTH), jnp.float32)],
        compiler_params=pltpu.CompilerParams(dimension_semantics=("arbitrary", "arbitrary"),
                                             vmem_limit_bytes=V7X_VMEM_LIMIT_BYTES),
        name="nsa_prompt",
    )(qpad, small, kcb, vct, kv3, kv3, kv3, kv3, tiles, bct)


def _pad_nsa_out_weight(w_proj_nsa):
    zeros = jnp.zeros((HEAD_DIM, D_MODEL), w_proj_nsa.dtype)
    rows = []
    for h in range(N_HEADS):
        wh = w_proj_nsa[h * HEAD_DIM:(h + 1) * HEAD_DIM]
        rows += [wh, zeros] if h // GROUP == 0 else [zeros, wh]
    return jnp.concatenate(rows, axis=0)


SUB_WIDTH = D_CMP * KV_WIDTH
PHI_HIDDEN = 2 * HEAD_DIM
PAGE_SUBS = PAGE_SIZE // D_CMP
SUB_PITCH = D_CMP + 8


def _compress_weights(pos_emb, w1, w2):
    eye = jnp.eye(N_KV, dtype=w1.dtype)
    halves = []
    for r in range(CMP_RATIO):
        w1r = w1[r * D_CMP:(r + 1) * D_CMP]
        halves.append(jnp.einsum('ldh,gk->lgdkh', w1r, eye).reshape(SUB_WIDTH, N_KV * PHI_HIDDEN))
    w1big = jnp.concatenate(halves, axis=1).astype(jnp.bfloat16)
    pos = jnp.broadcast_to(pos_emb.reshape(CMP_RATIO, D_CMP, 1, HEAD_DIM), (CMP_RATIO, D_CMP, N_KV, HEAD_DIM))
    w2big = jnp.einsum('hd,gk->ghkd', w2, eye).reshape(N_KV * PHI_HIDDEN, KV_WIDTH).astype(jnp.bfloat16)
    return w1big, pos.reshape(CMP_RATIO, 1, SUB_WIDTH), w2big


def _compress_tokens(lhs0, lhs1, w1big, w2big):
    hw = N_KV * PHI_HIDDEN
    p0 = jnp.dot(lhs0, w1big[:, :hw], preferred_element_type=jnp.float32)
    p1 = jnp.dot(lhs1, w1big[:, hw:], preferred_element_type=jnp.float32)
    n_sub = p1.shape[0]
    hidden = p0 + pltpu.roll(p1, n_sub - 1, 0)
    return jnp.dot(jax.nn.gelu(hidden).astype(jnp.bfloat16), w2big, preferred_element_type=jnp.float32)


def _compress_rows_kernel(rk_ref, rv_ref, w1k_ref, pk_ref, w2k_ref, w1v_ref, pv_ref, w2v_ref, kc_ref, vc_ref):
    for r_ref, w1_ref, p_ref, w2_ref, o_ref in ((rk_ref, w1k_ref, pk_ref, w2k_ref, kc_ref),
                                                 (rv_ref, w1v_ref, pv_ref, w2v_ref, vc_ref)):
        x = r_ref[0]
        lhs = [(x + p_ref[r]).astype(jnp.bfloat16) for r in range(CMP_RATIO)]
        o_ref[0] = _compress_tokens(lhs[0], lhs[1], w1_ref[...], w2_ref[...]).astype(o_ref.dtype)


def _compress_rows(k_rows, v_rows, b, t, wk, wv):
    n_sub = t // D_CMP
    view = lambda a: a.reshape(b, n_sub, SUB_WIDTH)
    seq = pl.BlockSpec((1, n_sub, SUB_WIDTH), lambda bi: (bi, 0, 0))
    full = lambda a: pl.BlockSpec(a.shape, lambda bi: (0,) * a.ndim)
    out = pl.BlockSpec((1, n_sub, KV_WIDTH), lambda bi: (bi, 0, 0))
    return pl.pallas_call(
        _compress_rows_kernel,
        out_shape=[jax.ShapeDtypeStruct((b, n_sub, KV_WIDTH), jnp.bfloat16)] * 2,
        grid=(b,),
        in_specs=[seq, seq] + [full(a) for a in (*wk, *wv)],
        out_specs=[out, out],
        compiler_params=pltpu.CompilerParams(dimension_semantics=("arbitrary",),
                                             vmem_limit_bytes=V7X_VMEM_LIMIT_BYTES),
        name="compress_rows",
    )(view(k_rows), view(v_rows), *wk, *wv)


def _compress_pages_kernel(pt_ref, ck_hbm, cv_hbm, w1k_ref, pk_ref, w2k_ref, w1v_ref, pv_ref, w2v_ref, kc_ref, vc_ref,
                           pages_ref, lhs_ref, rows_ref, sem_ref, *, n_pages):
    b = pl.program_id(0)
    caches = (ck_hbm, cv_hbm)

    def page_copy(c, seq, j, slot):
        return pltpu.make_async_copy(caches[c].at[pt_ref[seq, j]], pages_ref.at[slot, c, j], sem_ref.at[slot, c])

    def start_fetch(seq, slot):
        for c in range(2):
            for j in range(n_pages):
                page_copy(c, seq, j, slot).start(priority=j % 2)

    @pl.when(b == 0)
    def _():
        start_fetch(0, 0)

    @pl.when(b + 1 < pl.num_programs(0))
    def _():
        start_fetch(b + 1, (b + 1) % 2)

    slot = b % 2
    for c in range(2):
        for j in range(n_pages):
            page_copy(c, b, j, slot).wait()

    for c, p_ref in enumerate((pk_ref, pv_ref)):
        for j in range(0, n_pages, 2):
            for jj in range(2):
                x = pages_ref[slot, c, j + jj].T
                for m in range(PAGE_SUBS):
                    rows_ref[c, jj, SUB_PITCH * m:SUB_PITCH * m + D_CMP, :] = x[D_CMP * m:D_CMP * (m + 1)]
            rows = slice(j * PAGE_SUBS, (j + 2) * PAGE_SUBS)
            for l in range(D_CMP):
                x = jnp.concatenate([rows_ref[c, 0, pl.ds(l, PAGE_SUBS, stride=SUB_PITCH), :],
                                     rows_ref[c, 1, pl.ds(l, PAGE_SUBS, stride=SUB_PITCH), :]], axis=0)
                lanes = slice(l * KV_WIDTH, (l + 1) * KV_WIDTH)
                for r in range(CMP_RATIO):
                    lhs_ref[c, r, rows, lanes] = (x + p_ref[r, :, lanes]).astype(jnp.bfloat16)

    for c, (w1_ref, w2_ref, o_ref) in enumerate(((w1k_ref, w2k_ref, kc_ref), (w1v_ref, w2v_ref, vc_ref))):
        o_ref[0] = _compress_tokens(lhs_ref[c, 0], lhs_ref[c, 1], w1_ref[...], w2_ref[...]).astype(o_ref.dtype)


def _compress_pages(cache_k, cache_v, page_table, wk, wv):
    b, n_pages = page_table.shape
    n_pool = cache_k.shape[0]
    n_sub = n_pages * PAGE_SUBS
    view = lambda c: c.transpose(0, 2, 3, 1).reshape(n_pool, KV_WIDTH, PAGE_SIZE)
    full = lambda a: pl.BlockSpec(a.shape, lambda bi, pt: (0,) * a.ndim)
    out = pl.BlockSpec((1, n_sub, KV_WIDTH), lambda bi, pt: (bi, 0, 0))
    in_hbm = pl.BlockSpec(memory_space=pl.ANY)
    return pl.pallas_call(
        partial(_compress_pages_kernel, n_pages=n_pages),
        out_shape=[jax.ShapeDtypeStruct((b, n_sub, KV_WIDTH), jnp.bfloat16)] * 2,
        grid_spec=pltpu.PrefetchScalarGridSpec(
            num_scalar_prefetch=1,
            grid=(b,),
            in_specs=[in_hbm, in_hbm] + [full(a) for a in (*wk, *wv)],
            out_specs=[out, out],
            scratch_shapes=[pltpu.VMEM((2, 2, n_pages, KV_WIDTH, PAGE_SIZE), jnp.float32),
                            pltpu.VMEM((2, CMP_RATIO, n_sub, SUB_WIDTH), jnp.bfloat16),
                            pltpu.VMEM((2, 2, PAGE_SUBS * SUB_PITCH, KV_WIDTH), jnp.float32),
                            pltpu.SemaphoreType.DMA((2, 2))]),
        compiler_params=pltpu.CompilerParams(dimension_semantics=("arbitrary",),
                                             vmem_limit_bytes=V7X_VMEM_LIMIT_BYTES),
        name="compress_pages",
    )(page_table, view(cache_k), view(cache_v), *wk, *wv)


N_SEL_DEC = PAST_LEN // L_SEL + 1
N_SEL_DEC_PAD = 256
CUR_BLOCK = PAST_LEN // L_SEL


def _nsa_decode_cmp_kernel(q_ref, kc_ref, vc_ref, bias_ref, oc_ref, imp_ref):
    f32, bf16 = jnp.float32, jnp.bfloat16
    n_c = kc_ref.shape[1]
    s = _dot_nt(q_ref[0], kc_ref[0]) + bias_ref[...]
    mx = jnp.max(s, axis=-1, keepdims=True)
    mx = jnp.where(mx < 0.1 * NEG_BIG, 0.0, mx)
    p = jnp.exp(s - mx)
    p = p / jnp.maximum(jnp.sum(p, axis=-1, keepdims=True), 1e-30)
    pb = p.astype(bf16)
    oc_ref[0] = jnp.dot(pb, vc_ref[0], preferred_element_type=f32)
    ov_c = lax.broadcasted_iota(jnp.int32, (n_c, N_SEL_DEC_PAD), 0)
    ov_s = lax.broadcasted_iota(jnp.int32, (n_c, N_SEL_DEC_PAD), 1)
    overlap = jnp.where((ov_c >= 4 * ov_s - 1) & (ov_c <= 4 * ov_s + 3), 1.0, 0.0).astype(bf16)
    imp_h = jnp.dot(pb, overlap, preferred_element_type=f32)
    head = lax.broadcasted_iota(jnp.int32, imp_h.shape, 0)
    imp_ref[0] = jnp.concatenate(
        [jnp.sum(jnp.where((head >= g * GROUP) & (head < (g + 1) * GROUP), imp_h, 0.0), axis=0, keepdims=True)
         for g in range(N_KV)], axis=1)


def _nsa_decode_topk_kernel(imp_ref, idx_ref, val_ref):
    f32 = jnp.float32
    nb = imp_ref.shape[0]
    s_io = lax.broadcasted_iota(jnp.int32, (N_SEL_DEC_PAD, nb), 0)
    visible = s_io < N_SEL_DEC
    forced = (s_io == 0) | (s_io == CUR_BLOCK) | (s_io == CUR_BLOCK - 1)
    tri_r = lax.broadcasted_iota(jnp.int32, (N_SEL_DEC_PAD, N_SEL_DEC_PAD), 0)
    tri_c = lax.broadcasted_iota(jnp.int32, (N_SEL_DEC_PAD, N_SEL_DEC_PAD), 1)
    before = jnp.where(tri_c < tri_r, 1.0, 0.0).astype(jnp.bfloat16)
    for g in range(N_KV):
        x = imp_ref[:, g * N_SEL_DEC_PAD:(g + 1) * N_SEL_DEC_PAD]
        xt = jnp.concatenate([x[:, i * LANES:(i + 1) * LANES].T for i in range(N_SEL_DEC_PAD // LANES)], axis=0)
        val = jnp.where(visible, xt + jnp.where(forced, FORCE_SCORE, 0.0), -jnp.inf)
        val_ref[...] = val

        def rank_body(s, rank):
            other = val_ref[pl.ds(s, 1), :]
            beats = (other > val) | ((other == val) & (s_io > s))
            return rank + jnp.where(beats, 1, 0)

        rank = lax.fori_loop(0, N_SEL_DEC, rank_body, jnp.zeros(val.shape, jnp.int32))
        sel = (rank < K_SEL) & visible
        n_before = jnp.dot(before, jnp.where(sel, 1.0, 0.0).astype(jnp.bfloat16), preferred_element_type=f32)
        for k in range(K_SEL):
            hit = sel & (n_before == float(k))
            idx_ref[g, pl.ds(k, 1), :] = jnp.sum(jnp.where(hit, s_io.astype(f32), 0.0), axis=0,
                                                 keepdims=True).astype(jnp.int32)


def _nsa_decode_sel_kernel(pt_ref, ix_ref, *refs):
    f32, bf16 = jnp.float32, jnp.bfloat16
    n_blk = N_KV * K_SEL
    k_blocks, v_blocks = refs[:n_blk], refs[n_blk:2 * n_blk]
    (q_ref, oc_ref, gl_ref, ksn_ref, vsn_ref, kwn_ref, vwn_ref, kwin_ref, vwin_ref, near_ref, b0_ref, bw_ref,
     o_ref, kwo_ref, vwo_ref) = refs[2 * n_blk:]
    b = pl.program_id(0)
    q = q_ref[0]
    qf = q.astype(f32)
    round_bf = lambda a: a.astype(bf16).astype(f32)
    group1 = lax.broadcasted_iota(jnp.int32, (N_HEADS, LANES), 0) >= GROUP
    second_half = lax.broadcasted_iota(jnp.int32, (N_HEADS, LANES), 1) >= L_SEL
    neg_tile = jnp.full((N_HEADS, LANES), NEG_BIG, f32)
    no_rows = jnp.zeros((HEAD_DIM, PAGE_SIZE), bf16)

    def block_bias(s):
        tile = jnp.where(s == CUR_BLOCK, neg_tile,
                         jnp.where(s == CUR_BLOCK - 1, near_ref[1],
                                   jnp.where(s == CUR_BLOCK - 2, near_ref[0], near_ref[2])))
        return jnp.where(second_half == ((s & 1) == 1), tile, neg_tile)

    def group_rows(x, g):
        x = x.astype(bf16)
        return jnp.concatenate([x, no_rows] if g == 0 else [no_rows, x], axis=0)

    scores, values, has_new = [], [], []
    for g in range(N_KV):
        ids = [ix_ref[b, g * K_SEL + k] for k in range(K_SEL)]
        kt = jnp.concatenate([group_rows(k_blocks[g * K_SEL + k][0], g) for k in range(K_SEL)], axis=1)
        values.append(jnp.concatenate([group_rows(v_blocks[g * K_SEL + k][0], g) for k in range(K_SEL)], axis=1))
        bias = jnp.concatenate([block_bias(ids[k]) for k in range(K_SEL)], axis=1)
        scores.append(jnp.dot(q, kt, preferred_element_type=f32) + bias)
        flag = ids[0] == CUR_BLOCK
        for k in range(1, K_SEL):
            flag = flag | (ids[k] == CUR_BLOCK)
        has_new.append(jnp.where(flag, 0.0, NEG_BIG))
    s = jnp.where(group1[:, :1], scores[1], scores[0])
    s_new = jnp.sum(qf * round_bf(ksn_ref[0]), axis=-1, keepdims=True) + b0_ref[...]
    s_new = s_new + jnp.where(group1, has_new[1], has_new[0])
    m = jnp.maximum(jnp.max(s, axis=-1, keepdims=True), s_new)
    p = jnp.exp(s - m[:, :1])
    p_new = jnp.exp(s_new - m)
    l = jnp.sum(p, axis=-1, keepdims=True) + p_new
    pb = p.astype(bf16)
    o_sel = jnp.where(group1, _dot_nt(pb, values[1]), _dot_nt(pb, values[0]))
    o_sel = (o_sel + round_bf(p_new) * round_bf(vsn_ref[0])) / l

    kwin, vwin = kwin_ref[0], vwin_ref[0]
    s = jnp.dot(q, kwin.astype(bf16), preferred_element_type=f32) + bw_ref[...]
    s_new = jnp.sum(qf * round_bf(kwn_ref[0]), axis=-1, keepdims=True) + b0_ref[...]
    m = jnp.maximum(jnp.max(s, axis=-1, keepdims=True), s_new)
    p = jnp.exp(s - m[:, :1])
    p_new = jnp.exp(s_new - m)
    l = jnp.sum(p, axis=-1, keepdims=True) + p_new
    o_win = _dot_nt(p.astype(bf16), vwin.astype(bf16))
    o_win = (o_win + round_bf(p_new) * round_bf(vwn_ref[0])) / l

    gates = jax.nn.sigmoid(gl_ref[0])
    o_ref[0] = (oc_ref[0] * gates[0] + o_sel * gates[1] + o_win * gates[2]).astype(bf16)

    wb = kwin.shape[1]
    last = lax.broadcasted_iota(jnp.int32, kwin.shape, 1) == wb - 1
    as_column = lambda row: jnp.broadcast_to(row, (KV_WIDTH, KV_WIDTH)).T[:, :1]
    kwo_ref[0] = jnp.where(last, as_column(kwn_ref[0]), pltpu.roll(kwin, wb - 1, 1))
    vwo_ref[0] = jnp.where(last, as_column(vwn_ref[0]), pltpu.roll(vwin, wb - 1, 1))


def _nsa_decode(qpad, small, kc, vc, ks_new, vs_new, kw_new, vw_new, cache_k_sel, cache_v_sel,
                cache_k_win, cache_v_win, page_table, rel_bias):
    f32 = jnp.float32
    b = qpad.shape[0]
    n_c = kc.shape[1]
    n_pool = cache_k_sel.shape[0]
    wb = cache_k_win.shape[1]
    tab = rel_bias.astype(f32)[_bucket_table()].T
    dist_c = PAST_LEN - (L_CMP - 1) - D_CMP * np.arange(n_c)
    bias_c = jnp.where(jnp.asarray(dist_c >= 0)[None, :], tab[:, np.clip(dist_c, 0, MAX_DIST)], NEG_BIG)
    j = np.arange(LANES) % L_SEL
    near = jnp.stack([tab[:, 2 * L_SEL - j], tab[:, L_SEL - j],
                      jnp.broadcast_to(tab[:, MAX_DIST:], (N_HEADS, LANES))])
    bias0 = jnp.broadcast_to(tab[:, :1], (N_HEADS, LANES))
    bias_w = tab[:, np.minimum(wb - np.arange(wb), MAX_DIST)]
    q3 = qpad.reshape(b, N_HEADS, LANES)

    per_seq = lambda *shape: pl.BlockSpec((1,) + shape, lambda bi, *_: (bi,) + (0,) * len(shape))
    full = lambda a: pl.BlockSpec(a.shape, lambda bi, *_: (0,) * a.ndim)
    oc, imp = pl.pallas_call(
        _nsa_decode_cmp_kernel,
        out_shape=[jax.ShapeDtypeStruct((b, N_HEADS, LANES), f32),
                   jax.ShapeDtypeStruct((b, 1, N_KV * N_SEL_DEC_PAD), f32)],
        grid=(b,),
        in_specs=[per_seq(N_HEADS, LANES), per_seq(n_c, KV_WIDTH), per_seq(n_c, KV_WIDTH), full(bias_c)],
        out_specs=[per_seq(N_HEADS, LANES), per_seq(1, N_KV * N_SEL_DEC_PAD)],
        compiler_params=pltpu.CompilerParams(dimension_semantics=("arbitrary",)),
        name="nsa_decode_cmp",
    )(q3, kc, vc, bias_c)

    idx = pl.pallas_call(
        _nsa_decode_topk_kernel,
        out_shape=jax.ShapeDtypeStruct((N_KV, K_SEL, b), jnp.int32),
        scratch_shapes=[pltpu.VMEM((N_SEL_DEC_PAD, b), f32)],
        name="nsa_decode_topk",
    )(imp.reshape(b, N_KV * N_SEL_DEC_PAD))
    idx = idx.reshape(N_KV * K_SEL, b).T

    halves = PAGE_SIZE // L_SEL

    def sel_block(i):
        def index_map(bi, pt, ix):
            bs = jnp.minimum(bi, b - 1)
            s = jnp.clip(ix[bs, i], 0, CUR_BLOCK - 1)
            return (jnp.clip(pt[bs, s // halves], 0, n_pool - 1) * N_KV + i // K_SEL, 0, 0)
        return pl.BlockSpec((1, HEAD_DIM, PAGE_SIZE), index_map)

    keys_minor = lambda c: c.transpose(0, 2, 3, 1)
    page_groups = lambda c: keys_minor(c).reshape(n_pool * N_KV, HEAD_DIM, PAGE_SIZE)
    win_t = lambda c: keys_minor(c).reshape(b, KV_WIDTH, wb)
    glog = jnp.broadcast_to(small[:, :3 * N_HEADS].reshape(b, N_HEADS, 3).transpose(0, 2, 1)[..., None],
                            (b, 3, N_HEADS, LANES))
    new_row = lambda a: a.reshape(b, 1, KV_WIDTH)
    n_blk = N_KV * K_SEL
    o_pad, k_win, v_win = pl.pallas_call(
        _nsa_decode_sel_kernel,
        out_shape=[jax.ShapeDtypeStruct((b, N_HEADS, LANES), jnp.bfloat16),
                   jax.ShapeDtypeStruct((b, KV_WIDTH, wb), f32), jax.ShapeDtypeStruct((b, KV_WIDTH, wb), f32)],
        grid_spec=pltpu.PrefetchScalarGridSpec(
            num_scalar_prefetch=2,
            grid=(b,),
            in_specs=[sel_block(i) for i in range(n_blk)] * 2
                     + [per_seq(N_HEADS, LANES), per_seq(N_HEADS, LANES), per_seq(3, N_HEADS, LANES)]
                     + [per_seq(1, KV_WIDTH)] * 4 + [per_seq(KV_WIDTH, wb)] * 2
                     + [full(near), full(bias0), full(bias_w)],
            out_specs=[per_seq(N_HEADS, LANES), per_seq(KV_WIDTH, wb), per_seq(KV_WIDTH, wb)]),
        compiler_params=pltpu.CompilerParams(dimension_semantics=("arbitrary",)),
        name="nsa_decode_sel",
    )(page_table, idx, *([page_groups(cache_k_sel)] * n_blk), *([page_groups(cache_v_sel)] * n_blk),
      q3, oc, glog, new_row(ks_new), new_row(vs_new), new_row(kw_new), new_row(vw_new),
      win_t(cache_k_win), win_t(cache_v_win), near, bias0, bias_w)
    rows_major = lambda a: a.reshape(b, N_KV, HEAD_DIM, wb).transpose(0, 3, 1, 2)
    return o_pad.reshape(b, QPAD_WIDTH), rows_major(k_win), rows_major(v_win)


MLSTM_CHUNK = 128
MLSTM_DEC_TILE = 8
CONV_TAIL = 8


def _mlstm_norm_gate(h, zo, norm_g):
    return h * lax.rsqrt(jnp.mean(h * h, axis=-1, keepdims=True) + EPS) * norm_g * jax.nn.sigmoid(zo)


def _mlstm_prompt_kernel(zqk_ref, zv_ref, zo_ref, sm_ref, cw_ref, cb_ref, wq_ref, wk_ref, gb_ref, ng_ref,
                         o_ref, co_ref, no_ref, mo_ref, xbuf_ref, c_ref, n_ref, m_ref):
    f32, bf16 = jnp.float32, jnp.bfloat16
    L = MLSTM_CHUNK
    c = pl.program_id(1)

    @pl.when(c == 0)
    def _():
        xbuf_ref[:CONV_TAIL] = jnp.zeros((CONV_TAIL, M_WIDTH), f32)
        c_ref[...] = jnp.zeros(c_ref.shape, f32)
        n_ref[...] = jnp.zeros(n_ref.shape, f32)
        m_ref[...] = jnp.zeros(m_ref.shape, f32)

    x = zqk_ref[...]
    xbuf_ref[CONV_TAIL:] = x
    conv = cb_ref[...]
    for j in range(CONV_W):
        conv = conv + xbuf_ref[pl.ds(CONV_TAIL - (CONV_W - 1) + j, L), :] * cw_ref[j:j + 1, :]
    xbuf_ref[:CONV_TAIL] = x[L - CONV_TAIL:]
    a = jax.nn.silu(conv).astype(bf16)

    t_io = lax.broadcasted_iota(jnp.int32, (L, L), 0)
    s_io = lax.broadcasted_iota(jnp.int32, (L, L), 1)
    causal = t_io >= s_io
    sm = sm_ref[...]
    for h in range(M_HEADS):
        hs = slice(h * M_DH, (h + 1) * M_DH)
        q = jnp.dot(a[:, hs], wq_ref[h], preferred_element_type=f32)
        k = jnp.dot(a[:, hs], wk_ref[h], preferred_element_type=f32) * (M_DH ** -0.5)
        v = zv_ref[:, hs]
        qb, kb, vb = q.astype(bf16), k.astype(bf16), v.astype(bf16)
        col = 3 * N_HEADS + h
        ig = jnp.broadcast_to(sm[:, col:col + 1], (L, L)) + gb_ref[0:1, hs]
        lf = jax.nn.log_sigmoid(jnp.broadcast_to(sm[:, col + M_HEADS:col + M_HEADS + 1], (L, L)) + gb_ref[1:2, hs])
        bcum = lf
        sh = 1
        while sh < L:
            bcum = bcum + jnp.where(t_io >= sh, pltpu.roll(bcum, sh, 0), 0.0)
            sh *= 2
        m_old = m_ref[h:h + 1, :]
        c_old = c_ref[h]
        n_old = n_ref[h:h + 1, :]
        dmat = jnp.where(causal, bcum - bcum.T + ig.T, -jnp.inf)
        inter = bcum + m_old
        m_t = jnp.maximum(jnp.max(dmat, axis=-1, keepdims=True), inter)
        sc = _dot_nt(qb, kb) * jnp.exp(dmat - m_t)
        decay = jnp.exp(inter - m_t)
        num = decay * _dot_nt(qb, c_old.astype(bf16)) + jnp.dot(sc.astype(bf16), vb, preferred_element_type=f32)
        den = decay * jnp.sum(q * n_old, axis=-1, keepdims=True) + jnp.sum(sc, axis=-1, keepdims=True)
        hh = num / jnp.maximum(jnp.abs(den), jnp.exp(-m_t))
        o_ref[:, hs] = _mlstm_norm_gate(hh, zo_ref[:, hs], ng_ref[:, hs])

        b_last = bcum[L - 1:L, :]
        w_log = b_last - bcum + ig
        m_new = jnp.maximum(b_last + m_old, jnp.max(w_log, axis=0, keepdims=True))
        w = jnp.exp(w_log - m_new)
        carry = jnp.exp(b_last + m_old - m_new)
        c_ref[h] = carry * c_old + jnp.dot((w * v).T.astype(bf16), kb, preferred_element_type=f32)
        n_ref[h:h + 1, :] = carry * n_old + jnp.sum(w * k, axis=0, keepdims=True)
        m_ref[h:h + 1, :] = m_new

    @pl.when(c == pl.num_programs(1) - 1)
    def _():
        co_ref[0] = c_ref[...]
        no_ref[0] = n_ref[...]
        mo_ref[0] = m_ref[...]


def _mlstm_weights(conv_w, conv_b, wq, wk, gate_bias, norm_g):
    gb = jnp.repeat(gate_bias.astype(jnp.float32), M_DH, axis=1)
    return (conv_w, conv_b.reshape(1, M_WIDTH), wq.astype(jnp.bfloat16), wk.astype(jnp.bfloat16), gb,
            norm_g.reshape(1, M_WIDTH))


def _mlstm_prompt(zmqk, zmv, zmo, small, weights, b, t):
    L = MLSTM_CHUNK
    n_ch = t // L
    rows = lambda width: pl.BlockSpec((L, width), lambda bi, ci: (bi * n_ch + ci, 0))
    full = lambda a: pl.BlockSpec(a.shape, lambda bi, ci: (0,) * a.ndim)
    state = lambda *shape: pl.BlockSpec((1,) + shape, lambda bi, ci: (bi,) + (0,) * len(shape))
    f32 = jnp.float32
    return pl.pallas_call(
        _mlstm_prompt_kernel,
        out_shape=[jax.ShapeDtypeStruct((b * t, M_WIDTH), f32), jax.ShapeDtypeStruct((b, M_HEADS, M_DH, M_DH), f32),
                   jax.ShapeDtypeStruct((b, 8, M_DH), f32), jax.ShapeDtypeStruct((b, 8, M_DH), f32)],
        grid=(b, n_ch),
        in_specs=[rows(M_WIDTH), rows(M_WIDTH), rows(M_WIDTH), rows(LANES)] + [full(a) for a in weights],
        out_specs=[rows(M_WIDTH), state(M_HEADS, M_DH, M_DH), state(8, M_DH), state(8, M_DH)],
        scratch_shapes=[pltpu.VMEM((CONV_TAIL + L, M_WIDTH), f32), pltpu.VMEM((M_HEADS, M_DH, M_DH), f32),
                        pltpu.VMEM((8, M_DH), f32), pltpu.VMEM((8, M_DH), f32)],
        compiler_params=pltpu.CompilerParams(dimension_semantics=("arbitrary", "arbitrary")),
        name="mlstm_prompt",
    )(zmqk, zmv, zmo, small, *weights)


def _mlstm_decode_kernel(zqk_ref, conv_ref, zv_ref, zo_ref, sm_ref, c_ref, n_ref, m_ref,
                         cw_ref, cb_ref, wq_ref, wk_ref, gb_ref, ng_ref, o_ref, co_ref, no_ref, mo_ref):
    f32, bf16 = jnp.float32, jnp.bfloat16
    nt = MLSTM_DEC_TILE
    conv = cb_ref[...] + zqk_ref[...] * cw_ref[CONV_W - 1:CONV_W, :]
    for j in range(CONV_W - 1):
        conv = conv + conv_ref[j] * cw_ref[j:j + 1, :]
    a = jax.nn.silu(conv).astype(bf16)
    sm = sm_ref[...]
    lane = lax.broadcasted_iota(jnp.int32, (M_DH, M_DH), 1)
    pad_rows = jnp.zeros((M_DH - nt, M_DH), f32)
    for h in range(M_HEADS):
        hs = slice(h * M_DH, (h + 1) * M_DH)
        q = jnp.dot(a[:, hs], wq_ref[h], preferred_element_type=f32)
        k = jnp.dot(a[:, hs], wk_ref[h], preferred_element_type=f32) * (M_DH ** -0.5)
        v = zv_ref[:, hs]
        col = 3 * N_HEADS + h
        ig = jnp.broadcast_to(sm[:, col:col + 1], (nt, M_DH)) + gb_ref[0:1, hs]
        lf = jax.nn.log_sigmoid(jnp.broadcast_to(sm[:, col + M_HEADS:col + M_HEADS + 1], (nt, M_DH)) + gb_ref[1:2, hs])
        m_old = m_ref[:, hs]
        n_old = n_ref[:, hs]
        m_new = jnp.maximum(lf + m_old, ig)
        decay = jnp.exp(lf + m_old - m_new)
        w = jnp.exp(ig - m_new)
        sc = jnp.sum(q * k, axis=-1, keepdims=True) * w
        den = decay * jnp.sum(n_old * q, axis=-1, keepdims=True) + sc
        wv_t = jnp.concatenate([w * v, pad_rows], axis=0).T
        cq_t = jnp.zeros((M_DH, M_DH), f32)
        for i in range(nt):
            c_old = c_ref[i, h]
            cq = jnp.sum(c_old * q[i:i + 1, :], axis=-1, keepdims=True)
            cq_t = jnp.where(lane == i, cq, cq_t)
            co_ref[i, h] = decay[i:i + 1, :] * c_old + wv_t[:, i:i + 1] * k[i:i + 1, :]
        num = decay * cq_t.T[:nt] + sc * v
        hh = num / jnp.maximum(jnp.abs(den), jnp.exp(-m_new))
        o_ref[:, hs] = _mlstm_norm_gate(hh, zo_ref[:, hs], ng_ref[:, hs])
        no_ref[:, hs] = decay * n_old + w * k
        mo_ref[:, hs] = m_new


def _mlstm_decode(zmqk, zmv, zmo, small, state_c, state_n, state_m, state_conv, weights):
    b = zmqk.shape[0]
    nt = MLSTM_DEC_TILE
    f32 = jnp.float32
    rows = lambda width: pl.BlockSpec((nt, width), lambda i: (i, 0))
    full = lambda a: pl.BlockSpec(a.shape, lambda i: (0,) * a.ndim)
    cspec = pl.BlockSpec((nt, M_HEADS, M_DH, M_DH), lambda i: (i, 0, 0, 0))
    conv_t = state_conv.transpose(1, 0, 2)
    m_rep = jnp.repeat(state_m, M_DH, axis=1)
    return pl.pallas_call(
        _mlstm_decode_kernel,
        out_shape=[jax.ShapeDtypeStruct((b, M_WIDTH), f32), jax.ShapeDtypeStruct(state_c.shape, f32),
                   jax.ShapeDtypeStruct((b, M_WIDTH), f32), jax.ShapeDtypeStruct((b, M_WIDTH), f32)],
        grid=(b // nt,),
        in_specs=[rows(M_WIDTH), pl.BlockSpec((CONV_W - 1, nt, M_WIDTH), lambda i: (0, i, 0)), rows(M_WIDTH),
                  rows(M_WIDTH), rows(LANES), cspec, rows(M_WIDTH), rows(M_WIDTH)] + [full(a) for a in weights],
        out_specs=[rows(M_WIDTH), cspec, rows(M_WIDTH), rows(M_WIDTH)],
        compiler_params=pltpu.CompilerParams(dimension_semantics=("arbitrary",)),
        name="mlstm_decode",
    )(zmqk, conv_t, zmv, zmo, small, state_c, state_n.reshape(b, M_WIDTH), m_rep, *weights)


def _merge_kernel(x_ref, on_ref, om_ref, zm_ref, wn_ref, wm_ref, wo_ref, o_ref):
    zm = zm_ref[...]
    g_a = jax.nn.sigmoid(zm[:, :D_MODEL])
    g_b = jax.nn.sigmoid(zm[:, D_MODEL:])
    ya = jnp.dot(on_ref[...], wn_ref[...], preferred_element_type=jnp.float32)
    yb = jnp.dot(om_ref[...].astype(jnp.bfloat16), wm_ref[...], preferred_element_type=jnp.float32)
    y = (g_a * ya + g_b * yb).astype(jnp.bfloat16)
    o_ref[...] = x_ref[...] + jnp.dot(y, wo_ref[...], preferred_element_type=jnp.float32)


def _merge(x, o_nsa, o_mlstm, zmerge, w_proj_nsa, w_proj_mlstm, w_out):
    n = x.shape[0]
    tm = min(PROJ_TOKEN_TILE, n)
    row = lambda width: pl.BlockSpec((tm, width), lambda i: (i, 0))
    full = lambda a: pl.BlockSpec(a.shape, lambda i: (0, 0))
    return pl.pallas_call(
        _merge_kernel,
        out_shape=jax.ShapeDtypeStruct((n, D_MODEL), jnp.float32),
        grid=(n // tm,),
        in_specs=[row(D_MODEL), row(o_nsa.shape[1]), row(M_WIDTH), row(2 * D_MODEL),
                  full(w_proj_nsa), full(w_proj_mlstm), full(w_out)],
        out_specs=row(D_MODEL),
        compiler_params=pltpu.CompilerParams(dimension_semantics=("arbitrary",),
                                             vmem_limit_bytes=V7X_VMEM_LIMIT_BYTES),
        name="merge_out_proj",
    )(x, o_nsa, o_mlstm, zmerge, w_proj_nsa, w_proj_mlstm, w_out)


def _kv_rows(a, b, t):
    return a.reshape(b, t, N_KV, HEAD_DIM)


def _mix_prompt(proj, b, t, cmp_k, cmp_v, mlstm_w, rel_bias):
    qpad, kvb, kc_rows, vc_rows, ks_rows, vs_rows, kw_rows, vw_rows, zmqk, zmv, zmo, zmerge, small = proj
    kc, vc = _compress_rows(kc_rows, vc_rows, b, t, cmp_k, cmp_v)
    o_nsa = _nsa_prompt(qpad, small, kvb, kc, vc.transpose(0, 2, 1), rel_bias, b, t)
    o_mlstm, c_f, n_f, m_f = _mlstm_prompt(zmqk, zmv, zmo, small, mlstm_w, b, t)
    n_keep = min(WINDOW, t)
    states = (_kv_rows(kc_rows, b, t), _kv_rows(vc_rows, b, t), _kv_rows(ks_rows, b, t), _kv_rows(vs_rows, b, t),
              _kv_rows(kw_rows, b, t)[:, t - n_keep:], _kv_rows(vw_rows, b, t)[:, t - n_keep:],
              c_f, n_f[:, :M_HEADS], m_f[:, :M_HEADS, 0], zmqk.reshape(b, t, M_WIDTH)[:, t - (CONV_W - 1):])
    return o_nsa, o_mlstm, zmerge, states


def _mix_decode(proj, b, caches, mlstm_state, page_table, cmp_k, cmp_v, mlstm_w, rel_bias):
    cache_k_cmp, cache_v_cmp, cache_k_sel, cache_v_sel, cache_k_win, cache_v_win = caches
    state_c, state_n, state_m, state_conv = mlstm_state
    qpad, _, kc_new, vc_new, ks_new, vs_new, kw_new, vw_new, zmqk, zmv, zmo, zmerge, small = proj
    kc, vc = _compress_pages(cache_k_cmp, cache_v_cmp, page_table, cmp_k, cmp_v)
    o_nsa, k_win, v_win = _nsa_decode(qpad, small, kc, vc, ks_new, vs_new, kw_new, vw_new, cache_k_sel, cache_v_sel,
                                      cache_k_win, cache_v_win, page_table, rel_bias)
    o_mlstm, c_n, n_n, m_rep = _mlstm_decode(zmqk, zmv, zmo, small, state_c, state_n, state_m, state_conv, mlstm_w)
    wb = cache_k_win.shape[1]
    conv_new = jnp.concatenate([state_conv[:, 1:], zmqk[:, None, :]], axis=1)
    states = (_kv_rows(kc_new, b, 1), _kv_rows(vc_new, b, 1), _kv_rows(ks_new, b, 1), _kv_rows(vs_new, b, 1),
              _kv_rows(k_win, b, wb), _kv_rows(v_win, b, wb),
              c_n, n_n.reshape(b, M_HEADS, M_DH), m_rep[:, ::M_DH], conv_new)
    return o_nsa, o_mlstm, zmerge, states


def kernel(x_prompt, x_sample, cache_k_cmp, cache_v_cmp, cache_k_sel, cache_v_sel, cache_k_win, cache_v_win, state_mlstm_C, state_mlstm_n, state_mlstm_m, state_mlstm_conv, page_table, norm_ffn1, ffn1_w_in, ffn1_w_out, norm_mix, w_mix_in, cmp_pos_k, cmp_pos_v, cmp_phi_k1, cmp_phi_k2, cmp_phi_v1, cmp_phi_v2, rel_bias, mlstm_conv_w, mlstm_conv_b, mlstm_wq, mlstm_wk, mlstm_gate_bias, mlstm_norm, w_proj_nsa, w_proj_mlstm, w_out, norm_ffn2, ffn2_w_in, ffn2_w_out, norm_final):
    assert x_sample.shape[1] == DEC_SEQ == 1
    bf = lambda w: w.astype(jnp.bfloat16)
    w1i, w1o, w2i, w2o = bf(ffn1_w_in), bf(ffn1_w_out), bf(ffn2_w_in), bf(ffn2_w_out)
    w_mix = _regroup_mix_weight(w_mix_in)
    w_nsa_out, wm, wo = bf(_pad_nsa_out_weight(w_proj_nsa)), bf(w_proj_mlstm), bf(w_out)
    cmp_k = _compress_weights(cmp_pos_k, cmp_phi_k1, cmp_phi_k2)
    cmp_v = _compress_weights(cmp_pos_v, cmp_phi_v1, cmp_phi_v2)
    mlstm_w = _mlstm_weights(mlstm_conv_w, mlstm_conv_b, mlstm_wq, mlstm_wk, mlstm_gate_bias, mlstm_norm)

    def layer(x3, mix_fn):
        b, t, _ = x3.shape
        x = x3.reshape(b * t, D_MODEL)
        x1 = _ffn(x, norm_ffn1, w1i, w1o, norm_final, final_norm=False)
        proj = _project(x1, norm_mix, w_mix)
        o_nsa, o_mlstm, zmerge, states = mix_fn(proj, b, t)
        x2 = _merge(x1, o_nsa, o_mlstm, zmerge, w_nsa_out, wm, wo)
        y = _ffn(x2, norm_ffn2, w2i, w2o, norm_final, final_norm=True)
        return y.reshape(b, t, D_MODEL), states

    y_prompt, st_p = layer(x_prompt, lambda proj, b, t: _mix_prompt(proj, b, t, cmp_k, cmp_v, mlstm_w, rel_bias))
    y_sample, st_s = layer(x_sample, lambda proj, b, t: _mix_decode(
        proj, b, (cache_k_cmp, cache_v_cmp, cache_k_sel, cache_v_sel, cache_k_win, cache_v_win),
        (state_mlstm_C, state_mlstm_n, state_mlstm_m, state_mlstm_conv), page_table, cmp_k, cmp_v, mlstm_w, rel_bias))
    k_cmp_p, v_cmp_p, k_sel_p, v_sel_p, k_win_p, v_win_p, C_p, n_p, m_p, conv_p = st_p
    k_cmp_s, v_cmp_s, k_sel_s, v_sel_s, k_win_s, v_win_s, C_s, n_s, m_s, conv_s = st_s
    return (y_prompt, y_sample, k_cmp_p, k_cmp_s, v_cmp_p, v_cmp_s, k_sel_p, k_sel_s, v_sel_p, v_sel_s,
            k_win_p, k_win_s, v_win_p, v_win_s, C_p, C_s, n_p, n_s, m_p, m_s, conv_p, conv_s)
```
